```python
import math
import jax, jax.numpy as jnp
from jax import lax
import numpy as np

D_MODEL = 1024
BATCH = 2
SEQ = 8192
DEPTH = 2

CHUNK = 64
N_EVEN = (DEPTH + 1) // 2
N_ODD = DEPTH // 2
HEAD_DIM = 64
POOL_WINDOWS = (2, 4, 8, 16)
POOL_GROUPS = 4
POOL_WIDTH = D_MODEL // 2
POOL_GROUP_DIM = POOL_WIDTH // POOL_GROUPS
ATT_HEADS = (D_MODEL // 2) // HEAD_DIM
ATT_WIDTH = ATT_HEADS * HEAD_DIM
LEFT_CHUNKS = 8
BAND = (LEFT_CHUNKS + 1) * CHUNK
REL_CLIP = 128
N_REL = 2 * REL_CLIP + 1
EVEN_IN = POOL_WIDTH + 3 * ATT_WIDTH
EVEN_OUT = POOL_WIDTH + ATT_WIDTH
SSM_WIDTH = D_MODEL // 2
SSM_GROUP_DIM = 16
SSM_GROUPS = SSM_WIDTH // SSM_GROUP_DIM
SSM_STATE = 64
DT_MIN = 1e-3
DT_MAX = 1e-1
FOX_HEADS = (D_MODEL // 2) // HEAD_DIM
FOX_WIDTH = FOX_HEADS * HEAD_DIM
ODD_IN = SSM_WIDTH + 3 * FOX_WIDTH + FOX_HEADS
ODD_OUT = SSM_WIDTH + FOX_WIDTH
Q_BLOCK = 128
N_EXPERTS = 32
TOP_K = 4
D_EXPERT = D_MODEL
SWIGLU_LIMIT = 7.0
SWIGLU_ALPHA = 1.702
ROW_BLOCK = 128
DN_ALPHA = (2.0 * DEPTH) ** 0.25
DN_BETA = (8.0 * DEPTH) ** -0.25
LN_EPS = 1e-5
NEG_INF = -1e30

kernel_name = "hybrid_pool_chunkattn_s5_fox_moe"


def layer_norm(x, g, b):
    xf = x.astype(jnp.float32)
    mu = jnp.mean(xf, axis=-1, keepdims=True)
    var = jnp.mean(jnp.square(xf - mu), axis=-1, keepdims=True)
    return ((xf - mu) * lax.rsqrt(var + LN_EPS) * g.astype(jnp.float32) + b.astype(jnp.float32)).astype(x.dtype)


def ada_mod(c, w, b):
    m = jax.nn.silu(c) @ w + b
    shift, scale, gate = jnp.split(m[:, None, :], 3, axis=-1)
    return shift, scale, gate


def pool_mixer(u, w_pool, pool_scale):
    Bn, S, _ = u.shape
    ug = u.reshape(Bn, S, POOL_GROUPS, POOL_GROUP_DIM).astype(jnp.float32)
    cs = jnp.concatenate([jnp.zeros_like(ug[:, :1]), jnp.cumsum(ug, axis=1)], axis=1)
    t = jnp.arange(S)[:, None]
    win = jnp.array(POOL_WINDOWS, dtype=jnp.int32)[None, :]
    lo = jnp.maximum(t + 1 - win, 0)
    cnt = (t + 1 - lo).astype(jnp.float32)[None, :, :, None]
    window_sum = cs[:, 1:] - cs[:, lo, jnp.arange(POOL_GROUPS)[None, :]]
    d = (window_sum / cnt - ug).astype(u.dtype)
    y = jnp.einsum('bsgc,gcd->bsgd', d, w_pool)
    return y.reshape(Bn, S, POOL_WIDTH) * pool_scale


def chunk_attention(q, k, v, rel_bias):
    Bn, S, H, Dh = q.shape
    nc = S // CHUNK
    qc = q.reshape(Bn, nc, CHUNK, H, Dh)

    def band(t):
        tp = jnp.pad(t.reshape(Bn, nc, CHUNK, H, Dh), ((0, 0), (LEFT_CHUNKS, 0), (0, 0), (0, 0), (0, 0)))
        return jnp.concatenate([tp[:, j:j + nc] for j in range(LEFT_CHUNKS + 1)], axis=2)

    kb, vb = band(k), band(v)
    scores = jnp.einsum('bnqhd,bnkhd->bnhqk', qc, kb).astype(jnp.float32) * (Dh ** -0.5)
    q_pos = jnp.arange(CHUNK)[:, None]
    k_pos = jnp.arange(BAND)[None, :] - LEFT_CHUNKS * CHUNK
    rel = jnp.clip(q_pos - k_pos, -REL_CLIP, REL_CLIP) + REL_CLIP
    bias = rel_bias.astype(jnp.float32)[:, rel]
    key_chunk = jnp.arange(nc)[:, None] - LEFT_CHUNKS + jnp.arange(BAND)[None, :] // CHUNK
    scores = jnp.where((key_chunk >= 0)[None, :, None, None, :], scores + bias[None, None], NEG_INF)
    p = jax.nn.softmax(scores, axis=-1).astype(v.dtype)
    o = jnp.einsum('bnhqk,bnkhd->bnqhd', p, vb)
    return o.reshape(Bn, S, H * Dh)


def s5_mixer(u, lam_re, lam_im, log_dt, b_re, b_im, c_re, c_im, d_skip, glu_w, glu_b):
    Bn, S, _ = u.shape
    ug = u.reshape(Bn, S, SSM_GROUPS, SSM_GROUP_DIM).astype(jnp.float32)
    dt = jnp.exp(log_dt.astype(jnp.float32))[:, None]
    lre = jnp.minimum(lam_re.astype(jnp.float32), -1e-4)
    lim = lam_im.astype(jnp.float32)
    mag = jnp.exp(lre * dt)
    abar_re, abar_im = mag * jnp.cos(lim * dt), mag * jnp.sin(lim * dt)
    den = lre * lre + lim * lim
    nre, nim = abar_re - 1.0, abar_im
    g_re = (nre * lre + nim * lim) / den
    g_im = (nim * lre - nre * lim) / den
    bre, bim = b_re.astype(jnp.float32), b_im.astype(jnp.float32)
    bb_re = g_re[..., None] * bre - g_im[..., None] * bim
    bb_im = g_re[..., None] * bim + g_im[..., None] * bre
    bu_re = jnp.einsum('bsgc,gpc->bsgp', ug, bb_re)
    bu_im = jnp.einsum('bsgc,gpc->bsgp', ug, bb_im)
    a_re = jnp.broadcast_to(abar_re, bu_re.shape)
    a_im = jnp.broadcast_to(abar_im, bu_im.shape)

    def combine(e1, e2):
        a1r, a1i, b1r, b1i = e1
        a2r, a2i, b2r, b2i = e2
        return (a2r * a1r - a2i * a1i,
                a2r * a1i + a2i * a1r,
                a2r * b1r - a2i * b1i + b2r,
                a2r * b1i + a2i * b1r + b2i)

    _, _, xr, xi = lax.associative_scan(combine, (a_re, a_im, bu_re, bu_im), axis=1)
    y = (jnp.einsum('bsgp,gcp->bsgc', xr, c_re.astype(jnp.float32))
         - jnp.einsum('bsgp,gcp->bsgc', xi, c_im.astype(jnp.float32))
         + d_skip.astype(jnp.float32) * ug)
    y = jax.nn.gelu(y.reshape(Bn, S, SSM_WIDTH))
    y = y * jax.nn.sigmoid(y @ glu_w.astype(jnp.float32) + glu_b.astype(jnp.float32))
    return y.astype(u.dtype)


def forgetting_attention(q, k, v, f_logit):
    Bn, S, H, Dh = q.shape
    F = jnp.cumsum(jax.nn.log_sigmoid(f_logit.astype(jnp.float32)), axis=1)
    Fh = jnp.transpose(F, (0, 2, 1))
    k_pos = jnp.arange(S)
    q_off = jnp.arange(Q_BLOCK)

    def block(i):
        q0 = i * Q_BLOCK
        qb = lax.dynamic_slice_in_dim(q, q0, Q_BLOCK, axis=1)
        fq = lax.dynamic_slice_in_dim(Fh, q0, Q_BLOCK, axis=2)
        s = jnp.einsum('bqhd,bkhd->bhqk', qb, k).astype(jnp.float32) * (Dh ** -0.5)
        s = s + fq[..., None] - Fh[:, :, None, :]
        s = jnp.where(k_pos[None, :] <= (q0 + q_off)[:, None], s, NEG_INF)
        p = jax.nn.softmax(s, axis=-1).astype(v.dtype)
        return jnp.einsum('bhqk,bkhd->bqhd', p, v)

    o = lax.map(block, jnp.arange(S // Q_BLOCK))
    return jnp.transpose(o, (1, 0, 2, 3, 4)).reshape(Bn, S, H * Dh)


def even_mixer(h, w_in, w_pool, pool_scale, rel_bias, w_out):
    Bn, S, _ = h.shape
    z = h @ w_in
    o1, o2, o3 = POOL_WIDTH, POOL_WIDTH + ATT_WIDTH, POOL_WIDTH + 2 * ATT_WIDTH
    heads = lambda t: t.reshape(Bn, S, ATT_HEADS, HEAD_DIM)
    ya = pool_mixer(z[..., :o1], w_pool, pool_scale)
    yb = chunk_attention(heads(z[..., o1:o2]), heads(z[..., o2:o3]), heads(z[..., o3:]), rel_bias)
    return jnp.concatenate([ya, yb], axis=-1) @ w_out


def odd_mixer(h, w_in, forget_b, lam_re, lam_im, log_dt, b_re, b_im, c_re, c_im, d_skip, glu_w, glu_b, w_out):
    Bn, S, _ = h.shape
    z = h @ w_in
    o1 = SSM_WIDTH
    o2, o3, o4 = o1 + FOX_WIDTH, o1 + 2 * FOX_WIDTH, o1 + 3 * FOX_WIDTH
    heads = lambda t: t.reshape(Bn, S, FOX_HEADS, HEAD_DIM)
    yc = s5_mixer(z[..., :o1], lam_re, lam_im, log_dt, b_re, b_im, c_re, c_im, d_skip, glu_w, glu_b)
    yd = forgetting_attention(heads(z[..., o1:o2]), heads(z[..., o2:o3]), heads(z[..., o3:o4]),
                              z[..., o4:] + forget_b)
    return jnp.concatenate([yc, yd], axis=-1) @ w_out


def moe_ffn(h, router_w, router_b, w_gu, b_gu, w_down, b_down):
    Bn, S, D = h.shape
    xt = h.reshape(-1, D)
    T = xt.shape[0]
    logits = (xt @ router_w + router_b).astype(jnp.float32)
    top_v, top_e = lax.top_k(logits, TOP_K)
    gates = jax.nn.softmax(top_v, axis=-1)
    e_flat = top_e.reshape(-1)
    tok_flat = jnp.arange(T * TOP_K) // TOP_K
    order = jnp.argsort(e_flat)
    e_sorted, tok_sorted = e_flat[order], tok_flat[order]
    g_sorted = gates.reshape(-1)[order].astype(h.dtype)
    counts = jnp.zeros((N_EXPERTS,), jnp.int32).at[e_flat].add(1)
    padded = (counts + ROW_BLOCK - 1) // ROW_BLOCK * ROW_BLOCK
    start = jnp.cumsum(counts) - counts
    pend = jnp.cumsum(padded)
    pstart = pend - padded
    dest = pstart[e_sorted] + (jnp.arange(T * TOP_K) - start[e_sorted])
    n_blocks = (T * TOP_K + N_EXPERTS * (ROW_BLOCK - 1) + ROW_BLOCK - 1) // ROW_BLOCK
    n_rows = n_blocks * ROW_BLOCK
    x_disp = jnp.zeros((n_rows, D), h.dtype).at[dest].set(xt[tok_sorted])
    blk_expert = jnp.minimum(jnp.searchsorted(pend, jnp.arange(n_blocks) * ROW_BLOCK, side='right'), N_EXPERTS - 1)

    def expert_block(args):
        xb, e = args
        gu = xb @ w_gu[e] + b_gu[e]
        gate = jnp.minimum(gu[:, :D_EXPERT], SWIGLU_LIMIT)
        up = jnp.clip(gu[:, D_EXPERT:], -SWIGLU_LIMIT, SWIGLU_LIMIT)
        glu = gate * jax.nn.sigmoid(SWIGLU_ALPHA * gate)
        return ((up + 1.0) * glu) @ w_down[e] + b_down[e]

    y_disp = lax.map(expert_block, (x_disp.reshape(n_blocks, ROW_BLOCK, D), blk_expert)).reshape(n_rows, D)
    y = jnp.zeros((T, D), h.dtype).at[tok_sorted].add(y_disp[dest] * g_sorted[:, None])
    return y.reshape(Bn, S, D)


def setup_inputs(seed: int = 0) -> dict:
    key = jax.random.key(seed)
    ks = jax.random.split(key, 30)
    nrm = lambda k, shape, s: jax.random.normal(k, shape, jnp.float32) * s
    D = D_MODEL
    return {
        "x": nrm(ks[0], (BATCH, SEQ, D), 1.0),
        "c": nrm(ks[1], (BATCH, D), 1.0),
        "mod_w": nrm(ks[2], (DEPTH, 2, D, 3 * D), 0.3 * D ** -0.5),
        "mod_b": nrm(ks[3], (DEPTH, 2, 3 * D), 0.02),
        "ln_g": 1.0 + nrm(ks[4], (DEPTH, 2, D), 0.02),
        "ln_b": nrm(ks[5], (DEPTH, 2, D), 0.02),
        "even_w_in": nrm(ks[6], (N_EVEN, D, EVEN_IN), D ** -0.5),
        "pool_w": nrm(ks[7], (N_EVEN, POOL_GROUPS, POOL_GROUP_DIM, POOL_GROUP_DIM), POOL_GROUP_DIM ** -0.5),
        "pool_scale": 1.0 + nrm(ks[8], (N_EVEN, POOL_WIDTH), 0.02),
        "rel_bias": nrm(ks[9], (N_EVEN, ATT_HEADS, N_REL), 0.5),
        "even_w_out": nrm(ks[10], (N_EVEN, EVEN_OUT, D), DN_BETA * EVEN_OUT ** -0.5),
        "odd_w_in": nrm(ks[11], (N_ODD, D, ODD_IN), D ** -0.5),
        "forget_b": jax.random.uniform(ks[12], (N_ODD, FOX_HEADS), jnp.float32, 1.0, 4.0),
        "ssm_lam_re": -0.5 + nrm(ks[13], (N_ODD, SSM_GROUPS, SSM_STATE), 0.01),
        "ssm_lam_im": math.pi * jnp.arange(SSM_STATE, dtype=jnp.float32) + nrm(ks[14], (N_ODD, SSM_GROUPS, SSM_STATE), 0.01),
        "ssm_log_dt": jax.random.uniform(ks[15], (N_ODD, SSM_GROUPS), jnp.float32, math.log(DT_MIN), math.log(DT_MAX)),
        "ssm_b_re": nrm(ks[16], (N_ODD, SSM_GROUPS, SSM_STATE, SSM_GROUP_DIM), (2 * SSM_GROUP_DIM) ** -0.5),
        "ssm_b_im": nrm(ks[17], (N_ODD, SSM_GROUPS, SSM_STATE, SSM_GROUP_DIM), (2 * SSM_GROUP_DIM) ** -0.5),
        "ssm_c_re": nrm(ks[18], (N_ODD, SSM_GROUPS, SSM_GROUP_DIM, SSM_STATE), SSM_STATE ** -0.5),
        "ssm_c_im": nrm(ks[19], (N_ODD, SSM_GROUPS, SSM_GROUP_DIM, SSM_STATE), SSM_STATE ** -0.5),
        "ssm_d": nrm(ks[20], (N_ODD, SSM_GROUPS, SSM_GROUP_DIM), 1.0),
        "ssm_glu_w": nrm(ks[21], (N_ODD, SSM_WIDTH, SSM_WIDTH), SSM_WIDTH ** -0.5),
        "ssm_glu_b": nrm(ks[22], (N_ODD, SSM_WIDTH), 0.02),
        "odd_w_out": nrm(ks[23], (N_ODD, ODD_OUT, D), DN_BETA * ODD_OUT ** -0.5),
        "router_w": nrm(ks[24], (DEPTH, D, N_EXPERTS), D ** -0.5),
        "router_b": nrm(ks[25], (DEPTH, N_EXPERTS), 0.01),
        "exp_w_gu": nrm(ks[26], (DEPTH, N_EXPERTS, D, 2 * D_EXPERT), D ** -0.5),
        "exp_b_gu": nrm(ks[27], (DEPTH, N_EXPERTS, 2 * D_EXPERT), 0.02),
        "exp_w_down": nrm(ks[28], (DEPTH, N_EXPERTS, D_EXPERT, D), DN_BETA * D_EXPERT ** -0.5),
        "exp_b_down": nrm(ks[29], (DEPTH, N_EXPERTS, D), 0.02),
    }


def reference(x, c, mod_w, mod_b, ln_g, ln_b, even_w_in, pool_w, pool_scale, rel_bias, even_w_out,
              odd_w_in, forget_b, ssm_lam_re, ssm_lam_im, ssm_log_dt, ssm_b_re, ssm_b_im, ssm_c_re, ssm_c_im,
              ssm_d, ssm_glu_w, ssm_glu_b, odd_w_out, router_w, router_b, exp_w_gu, exp_b_gu, exp_w_down,
              exp_b_down):
    for layer in range(DEPTH):
        i = layer // 2
        shift, scale, gate = ada_mod(c, mod_w[layer, 0], mod_b[layer, 0])
        h = x * (1.0 + scale) + shift
        if layer % 2 == 0:
            y = even_mixer(h, even_w_in[i], pool_w[i], pool_scale[i], rel_bias[i], even_w_out[i])
        else:
            y = odd_mixer(h, odd_w_in[i], forget_b[i], ssm_lam_re[i], ssm_lam_im[i], ssm_log_dt[i],
                          ssm_b_re[i], ssm_b_im[i], ssm_c_re[i], ssm_c_im[i], ssm_d[i], ssm_glu_w[i],
                          ssm_glu_b[i], odd_w_out[i])
        x = layer_norm(DN_ALPHA * x + (1.0 + gate) * y, ln_g[layer, 0], ln_b[layer, 0])
        shift, scale, gate = ada_mod(c, mod_w[layer, 1], mod_b[layer, 1])
        h = x * (1.0 + scale) + shift
        y = moe_ffn(h, router_w[layer], router_b[layer], exp_w_gu[layer], exp_b_gu[layer],
                    exp_w_down[layer], exp_b_down[layer])
        x = layer_norm(DN_ALPHA * x + (1.0 + gate) * y, ln_g[layer, 1], ln_b[layer, 1])
    return x
```

```python
import functools
import math

import jax
import jax.numpy as jnp
from jax import lax
from jax.experimental import pallas as pl
from jax.experimental.pallas import tpu as pltpu

F32 = jnp.float32
BF16 = jnp.bfloat16
I32 = jnp.int32
HIGHEST = lax.Precision.HIGHEST

LANES = 128
VMEM_LIMIT_BYTES = 60000 * 1024

HEAD_DIM = 64
CHUNK = 64
LEFT_CHUNKS = 8
REL_CLIP = 128
POOL_WINDOWS = (2, 4, 8, 16)
POOL_HALO = 16
SSM_GROUP_DIM = 16
SSM_STATE = 64
N_EXPERTS = 32
TOP_K = 4
SWIGLU_LIMIT = 7.0
SWIGLU_ALPHA = 1.702
LN_EPS = 1e-5
NEG_INF = -1e30

ROW_TILE = 512
CATTN_Q = 2 * CHUNK
CATTN_BAND = (LEFT_CHUNKS + 2) * CHUNK
FOX_TQ = 512
FOX_TK = 512
S5_TILE = 128
FCUM_TILE = 512
MOE_ROW_BLOCK = 256
MOE_TOK_TILE = 256
ROUTE_TILE = 512


def _cparams(sem):
    return pltpu.CompilerParams(dimension_semantics=sem, vmem_limit_bytes=VMEM_LIMIT_BYTES)


def _layer_norm(r, g, b):
    mu = jnp.mean(r, axis=-1, keepdims=True)
    d = r - mu
    var = jnp.mean(d * d, axis=-1, keepdims=True)
    return d * lax.rsqrt(var + LN_EPS) * g + b


def _mod_kernel(c_ref, w_ref, b_ref, o_ref):
    c = c_ref[...]
    s = c * jax.nn.sigmoid(c)
    o_ref[0] = jnp.dot(s, w_ref[0], precision=HIGHEST, preferred_element_type=F32) + b_ref[0]


def _ada_mods(c, mod_w, mod_b):
    bsz, d = c.shape
    m = mod_w.shape[0]
    rows = 8
    c_pad = jnp.zeros((rows, d), F32).at[:bsz].set(c)
    out = pl.pallas_call(
        _mod_kernel,
        grid=(m, 3),
        in_specs=[
            pl.BlockSpec((rows, d), lambda i, j: (0, 0)),
            pl.BlockSpec((1, d, d), lambda i, j: (i, 0, j)),
            pl.BlockSpec((1, 1, d), lambda i, j: (i, 0, j)),
        ],
        out_specs=pl.BlockSpec((1, rows, d), lambda i, j: (i, 0, j)),
        out_shape=jax.ShapeDtypeStruct((m, rows, 3 * d), F32),
        compiler_params=_cparams(("parallel", "parallel")),
        name="ada_mod",
    )(c_pad, mod_w, mod_b.reshape(m, 1, 3 * d))
    return out[:, :bsz].reshape(m, bsz, 3, d)


def _mod_spec(midx, tiles_per_batch, d):
    return pl.BlockSpec((None, None, 3, d), lambda i: (midx, i // tiles_per_batch, 0, 0))


def _inproj_even_kernel(x_ref, mod_ref, w_ref, zp_ref, q_ref, k_ref, v_ref, *, width):
    h = x_ref[...] * (1.0 + mod_ref[1:2, :]) + mod_ref[0:1, :]
    z = jnp.dot(h.astype(BF16), w_ref[...], preferred_element_type=F32)
    zp_ref[...] = z[:, :width]
    q_ref[...] = (z[:, width:2 * width] * (HEAD_DIM ** -0.5)).astype(BF16)
    k_ref[...] = z[:, 2 * width:3 * width].astype(BF16)
    v_ref[...] = z[:, 3 * width:4 * width].astype(BF16)


def _inproj_even(x, mods, midx, w_in_bf, seq):
    t, d = x.shape
    n = w_in_bf.shape[1]
    width = n // 4
    tm = ROW_TILE
    row = lambda i: (i, 0)
    return pl.pallas_call(
        functools.partial(_inproj_even_kernel, width=width),
        grid=(t // tm,),
        in_specs=[
            pl.BlockSpec((tm, d), row),
            _mod_spec(midx, seq // tm, d),
            pl.BlockSpec((d, n), lambda i: (0, 0)),
        ],
        out_specs=[pl.BlockSpec((tm, width), row)] * 4,
        out_shape=[jax.ShapeDtypeStruct((t, width), F32)] + [jax.ShapeDtypeStruct((t, width), BF16)] * 3,
        compiler_params=_cparams(("parallel",)),
        name="inproj_even",
    )(x, mods, w_in_bf)


def _pool_kernel(zp_ref, halo_ref, w_ref, sc_ref, o_ref, xs_ref, *, ts, seq):
    i = pl.program_id(0)
    t0 = lax.rem(i * ts, seq)
    xs_ref[0:POOL_HALO, :] = jnp.where(t0 == 0, 0.0, halo_ref[...])
    xs_ref[POOL_HALO:POOL_HALO + ts, :] = zp_ref[...]
    pos = lax.broadcasted_iota(I32, (ts, LANES), 0) + t0
    for g, win in enumerate(POOL_WINDOWS):
        cols = slice(LANES * g, LANES * (g + 1))
        u = xs_ref[POOL_HALO:POOL_HALO + ts, cols]
        acc = u
        for k in range(1, win):
            acc = acc + xs_ref[POOL_HALO - k:POOL_HALO - k + ts, cols]
        cnt = jnp.minimum(pos + 1, win).astype(F32)
        dlt = acc / cnt - u
        y = jnp.dot(dlt.astype(BF16), w_ref[g], preferred_element_type=F32)
        o_ref[:, cols] = (y * sc_ref[:, cols]).astype(o_ref.dtype)


def _pool_mixer(zp, w_pool_bf, pool_scale, seq):
    t, width = zp.shape
    ts = ROW_TILE
    hb = ts // POOL_HALO
    return pl.pallas_call(
        functools.partial(_pool_kernel, ts=ts, seq=seq),
        grid=(t // ts,),
        in_specs=[
            pl.BlockSpec((ts, width), lambda i: (i, 0)),
            pl.BlockSpec((POOL_HALO, width), lambda i: (jnp.maximum(i * hb - 1, 0), 0)),
            pl.BlockSpec(w_pool_bf.shape, lambda i: (0, 0, 0)),
            pl.BlockSpec((1, width), lambda i: (0, 0)),
        ],
        out_specs=pl.BlockSpec((ts, width), lambda i: (i, 0)),
        out_shape=jax.ShapeDtypeStruct((t, width), BF16),
        scratch_shapes=[pltpu.VMEM((ts + POOL_HALO, width), F32)],
        compiler_params=_cparams(("parallel",)),
        name="pool_mixer",
    )(zp, zp, w_pool_bf, pool_scale.reshape(1, width))


def _cbias_kernel(r_ref, o_ref):
    base = jnp.broadcast_to(r_ref[...], (CATTN_Q, r_ref.shape[-1]))
    rolled = pltpu.roll(base, 0, 1, stride=1, stride_axis=0)[:, :CATTN_BAND]
    qi = lax.broadcasted_iota(I32, (CATTN_Q, CATTN_BAND), 0)
    kp = lax.broadcasted_iota(I32, (CATTN_Q, CATTN_BAND), 1)
    lo = (qi // CHUNK) * CHUNK
    valid = (kp >= lo) & (kp < lo + (LEFT_CHUNKS + 1) * CHUNK)
    o_ref[...] = jnp.where(valid, rolled, NEG_INF)


def _cattn_bias_table(rel_bias):
    h = rel_bias.shape[0]
    wide = 1024
    far = LEFT_CHUNKS * CHUNK - REL_CLIP + 1
    tail = wide - far - (2 * REL_CLIP - 1)
    last = rel_bias[:, 2 * REL_CLIP:]
    base = jnp.concatenate(
        [jnp.broadcast_to(last, (h, far)), rel_bias[:, 1:2 * REL_CLIP][:, ::-1], jnp.broadcast_to(last, (h, tail))],
        axis=1).reshape(h, 1, wide)
    return pl.pallas_call(
        _cbias_kernel,
        grid=(h,),
        in_specs=[pl.BlockSpec((None, 1, wide), lambda i: (i, 0, 0))],
        out_specs=pl.BlockSpec((None, CATTN_Q, CATTN_BAND), lambda i: (i, 0, 0)),
        out_shape=jax.ShapeDtypeStruct((h, CATTN_Q, CATTN_BAND), F32),
        compiler_params=_cparams(("parallel",)),
        name="cattn_bias",
    )(base)


def _cattn_kernel(q_ref, k_ref, v_ref, bias_ref, o_ref, *, n_pairs):
    p = pl.program_id(1)
    start = pl.multiple_of(p * CATTN_Q, CATTN_Q)
    kpos = lax.broadcasted_iota(I32, (CATTN_Q, CATTN_BAND), 1) + start
    real_key = kpos >= LEFT_CHUNKS * CHUNK
    lane = lax.broadcasted_iota(I32, (CATTN_Q, LANES), 1)
    low = lane < HEAD_DIM
    for hp in range(n_pairs):
        cols = slice(LANES * hp, LANES * (hp + 1))
        qp = q_ref[:, cols]
        kb = k_ref[pl.ds(start, CATTN_BAND), cols]
        vb = v_ref[pl.ds(start, CATTN_BAND), cols]
        outs = []
        for hh in range(2):
            qm = jnp.where(low if hh == 0 else jnp.logical_not(low), qp, jnp.zeros_like(qp))
            s = lax.dot_general(qm, kb, (((1,), (1,)), ((), ())), preferred_element_type=F32)
            s = jnp.where(real_key, s + bias_ref[2 * hp + hh], NEG_INF)
            m = jnp.max(s, axis=-1, keepdims=True)
            e = jnp.exp(s - m)
            l = jnp.sum(e, axis=-1, keepdims=True)
            o = jnp.dot(e.astype(BF16), vb, preferred_element_type=F32)
            outs.append(o / l)
        o_ref[:, cols] = jnp.where(low, outs[0], outs[1]).astype(o_ref.dtype)


def _chunk_attention(q, k, v, bias, bsz, seq):
    t, width = q.shape
    n_pairs = width // LANES
    pad = LEFT_CHUNKS * CHUNK
    kp = jnp.pad(k.reshape(bsz, seq, width), ((0, 0), (pad, 0), (0, 0)))
    vp = jnp.pad(v.reshape(bsz, seq, width), ((0, 0), (pad, 0), (0, 0)))
    nq = seq // CATTN_Q
    return pl.pallas_call(
        functools.partial(_cattn_kernel, n_pairs=n_pairs),
        grid=(bsz, nq),
        in_specs=[
            pl.BlockSpec((CATTN_Q, width), lambda b, p: (b * nq + p, 0)),
            pl.BlockSpec((None, seq + pad, width), lambda b, p: (b, 0, 0)),
            pl.BlockSpec((None, seq + pad, width), lambda b, p: (b, 0, 0)),
            pl.BlockSpec(bias.shape, lambda b, p: (0, 0, 0)),
        ],
        out_specs=pl.BlockSpec((CATTN_Q, width), lambda b, p: (b * nq + p, 0)),
        out_shape=jax.ShapeDtypeStruct((t, width), BF16),
        compiler_params=_cparams(("parallel", "arbitrary")),
        name="chunk_attention",
    )(q, kp, vp, bias)


def _outproj_kernel(ya_ref, yb_ref, x_ref, mod_ref, w_ref, g_ref, b_ref, o_ref, *, alpha, half):
    y = jnp.dot(ya_ref[...], w_ref[0:half, :], preferred_element_type=F32)
    y = y + jnp.dot(yb_ref[...], w_ref[half:2 * half, :], preferred_element_type=F32)
    r = alpha * x_ref[...] + (1.0 + mod_ref[2:3, :]) * y
    o_ref[...] = _layer_norm(r, g_ref[...], b_ref[...])


def _outproj_norm(ya, yb, x, mods, midx, w_out_bf, ln_g, ln_b, seq, alpha):
    t, d = x.shape
    half = ya.shape[1]
    tm = ROW_TILE
    row = lambda i: (i, 0)
    return pl.pallas_call(
        functools.partial(_outproj_kernel, alpha=alpha, half=half),
        grid=(t // tm,),
        in_specs=[
            pl.BlockSpec((tm, half), row),
            pl.BlockSpec((tm, half), row),
            pl.BlockSpec((tm, d), row),
            _mod_spec(midx, seq // tm, d),
            pl.BlockSpec((2 * half, d), lambda i: (0, 0)),
            pl.BlockSpec((1, d), lambda i: (0, 0)),
            pl.BlockSpec((1, d), lambda i: (0, 0)),
        ],
        out_specs=pl.BlockSpec((tm, d), row),
        out_shape=jax.ShapeDtypeStruct((t, d), F32),
        compiler_params=_cparams(("parallel",)),
        name="outproj_norm",
    )(ya, yb, x, mods, w_out_bf, ln_g.reshape(1, d), ln_b.reshape(1, d))


def _route_kernel(x_ref, mod_ref, rw_ref, rb_ref, h_ref, e_ref, r_ref, g_ref, c_ref, carry_ref, *, tm):
    @pl.when(pl.program_id(0) == 0)
    def _():
        carry_ref[...] = jnp.zeros_like(carry_ref)

    h = x_ref[...] * (1.0 + mod_ref[1:2, :]) + mod_ref[0:1, :]
    h_ref[...] = h
    logits = lax.dot_general(rw_ref[...], h, (((1,), (1,)), ((), ())), precision=HIGHEST,
                             preferred_element_type=F32) + rb_ref[...]
    eio = lax.broadcasted_iota(I32, (N_EXPERTS, tm), 0)
    vals, hots = [], []
    for k in range(TOP_K):
        m = jnp.max(logits, axis=0, keepdims=True)
        idx = jnp.min(jnp.where(logits == m, eio, N_EXPERTS), axis=0, keepdims=True)
        hot = eio == idx
        e_ref[k:k + 1, :] = idx
        vals.append(m)
        hots.append(hot)
        logits = jnp.where(hot, -jnp.inf, logits)
    exps = [jnp.exp(v - vals[0]) for v in vals]
    denom = exps[0] + exps[1] + exps[2] + exps[3]
    for k in range(TOP_K):
        g_ref[k:k + 1, :] = exps[k] / denom
    cnt = jnp.zeros((N_EXPERTS, tm), F32)
    for hot in hots:
        cnt = cnt + hot.astype(F32)
    before = (lax.broadcasted_iota(I32, (tm, tm), 0) < lax.broadcasted_iota(I32, (tm, tm), 1))
    prefix = jnp.dot(cnt.astype(BF16), before.astype(BF16), preferred_element_type=F32)
    tot = prefix + carry_ref[...]
    for k in range(TOP_K):
        r_ref[k:k + 1, :] = jnp.sum(jnp.where(hots[k], tot, 0.0), axis=0, keepdims=True).astype(I32)
    new_carry = carry_ref[...] + jnp.sum(cnt, axis=1, keepdims=True)
    carry_ref[...] = new_carry
    c_ref[...] = jnp.broadcast_to(new_carry, c_ref.shape)


def _route(x, mods, midx, router_w, router_b, seq):
    t, d = x.shape
    tm = ROUTE_TILE
    tok = lambda i: (0, i)
    return pl.pallas_call(
        functools.partial(_route_kernel, tm=tm),
        grid=(t // tm,),
        in_specs=[
            pl.BlockSpec((tm, d), lambda i: (i, 0)),
            _mod_spec(midx, seq // tm, d),
            pl.BlockSpec((N_EXPERTS, d), lambda i: (0, 0)),
            pl.BlockSpec((N_EXPERTS, 1), lambda i: (0, 0)),
        ],
        out_specs=[
            pl.BlockSpec((tm, d), lambda i: (i, 0)),
            pl.BlockSpec((TOP_K, tm), tok),
            pl.BlockSpec((TOP_K, tm), tok),
            pl.BlockSpec((TOP_K, tm), tok),
            pl.BlockSpec((N_EXPERTS, LANES), lambda i: (0, 0)),
        ],
        out_shape=[
            jax.ShapeDtypeStruct((t, d), F32),
            jax.ShapeDtypeStruct((TOP_K, t), I32),
            jax.ShapeDtypeStruct((TOP_K, t), I32),
            jax.ShapeDtypeStruct((TOP_K, t), F32),
            jax.ShapeDtypeStruct((N_EXPERTS, LANES), F32),
        ],
        scratch_shapes=[pltpu.VMEM((N_EXPERTS, 1), F32)],
        compiler_params=_cparams(("arbitrary",)),
        name="moe_route",
    )(x, mods, router_w.T, router_b.reshape(N_EXPERTS, 1))


def _plan_kernel(e_ref, r_ref, ccol_ref, crow_ref, d_ref, be_ref, nu_ref, *, tm, rb, nbp):
    inv = 1.0 / rb
    pad_col = jnp.floor((ccol_ref[...] + (rb - 1)) * inv) * rb
    pad_row = jnp.floor((crow_ref[0:1, :] + (rb - 1)) * inv) * rb
    ei = lax.broadcasted_iota(I32, (N_EXPERTS, LANES), 0)
    li = lax.broadcasted_iota(I32, (N_EXPERTS, LANES), 1)
    pstart = jnp.sum(jnp.where(li < ei, jnp.broadcast_to(pad_row, (N_EXPERTS, LANES)), 0.0),
                     axis=1, keepdims=True)
    pend = pstart + pad_col[:, 0:1]
    eio = lax.broadcasted_iota(I32, (N_EXPERTS, tm), 0)
    for k in range(TOP_K):
        base = jnp.sum(jnp.where(eio == e_ref[k:k + 1, :], pstart, 0.0), axis=0, keepdims=True)
        d_ref[k:k + 1, :] = base.astype(I32) + r_ref[k:k + 1, :]
    row0 = (lax.broadcasted_iota(I32, (N_EXPERTS, nbp), 1) * rb).astype(F32)
    be = jnp.sum((pend <= row0).astype(F32), axis=0, keepdims=True)
    be_ref[...] = jnp.minimum(be, N_EXPERTS - 1.0).astype(I32)
    used = jnp.sum(pad_col[:, 0:1], axis=0, keepdims=True) * inv
    nu_ref[...] = jnp.broadcast_to(used, nu_ref.shape).astype(I32)


def _plan(eidx, rank, counts_col, rb, nb):
    t = eidx.shape[1]
    tm = ROUTE_TILE
    nbp = -(-nb // LANES) * LANES
    counts_row = jnp.zeros((8, LANES), F32).at[:, :N_EXPERTS].set(counts_col[:, 0][None, :])
    tok = lambda i: (0, i)
    dest, be, nu = pl.pallas_call(
        functools.partial(_plan_kernel, tm=tm, rb=rb, nbp=nbp),
        grid=(t // tm,),
        in_specs=[
            pl.BlockSpec((TOP_K, tm), tok),
            pl.BlockSpec((TOP_K, tm), tok),
            pl.BlockSpec((N_EXPERTS, LANES), lambda i: (0, 0)),
            pl.BlockSpec((8, LANES), lambda i: (0, 0)),
        ],
        out_specs=[
            pl.BlockSpec((TOP_K, tm), tok),
            pl.BlockSpec((1, nbp), lambda i: (0, 0)),
            pl.BlockSpec((1, LANES), lambda i: (0, 0)),
        ],
        out_shape=[
            jax.ShapeDtypeStruct((TOP_K, t), I32),
            jax.ShapeDtypeStruct((1, nbp), I32),
            jax.ShapeDtypeStruct((1, LANES), I32),
        ],
        compiler_params=_cparams(("arbitrary",)),
        name="moe_plan",
    )(eidx, rank, counts_col, counts_row)
    return dest, be.reshape(nbp), nu[0, :1]


def _row_copy(src_ref, src_row, dst_ref, dst_row, sem):
    return pltpu.make_async_copy(src_ref.at[pl.ds(src_row, 1)], dst_ref.at[pl.ds(dst_row, 1)], sem)


def _dispatch_kernel(h_ref, dest_ref, xin_ref, xout_ref, sem, *, tm):
    del xin_ref

    def issue(r, c):
        for k in range(TOP_K):
            _row_copy(h_ref, r, xout_ref, dest_ref[k, r], sem).start()
        return c

    lax.fori_loop(0, tm, issue, 0)

    def drain(r, c):
        for k in range(TOP_K):
            _row_copy(h_ref, 0, xout_ref, 0, sem).wait()
        return c

    lax.fori_loop(0, tm, drain, 0)


def _dispatch(h, dest, n_rows):
    t, d = h.shape
    tm = MOE_TOK_TILE
    zeros = jnp.zeros((n_rows, d), h.dtype)
    return pl.pallas_call(
        functools.partial(_dispatch_kernel, tm=tm),
        grid=(t // tm,),
        in_specs=[
            pl.BlockSpec((tm, d), lambda i: (i, 0)),
            pl.BlockSpec((TOP_K, tm), lambda i: (0, i), memory_space=pltpu.SMEM),
            pl.BlockSpec(memory_space=pl.ANY),
        ],
        out_specs=pl.BlockSpec(memory_space=pl.ANY),
        out_shape=jax.ShapeDtypeStruct((n_rows, d), h.dtype),
        scratch_shapes=[pltpu.SemaphoreType.DMA],
        input_output_aliases={2: 0},
        compiler_params=_cparams(("arbitrary",)),
        name="moe_dispatch",
    )(h, dest, zeros)


def _expert_kernel(be_ref, nu_ref, x_ref, wgu_ref, bgu_ref, wd_ref, bd_ref, o_ref, wgu_s, wd_s, *, dff):
    b = pl.program_id(0)
    e = be_ref[b]
    prev = be_ref[jnp.maximum(b - 1, 0)]

    @pl.when(jnp.logical_or(b == 0, e != prev))
    def _():
        wgu_s[...] = wgu_ref[0].astype(BF16)
        wd_s[...] = wd_ref[0].astype(BF16)

    @pl.when(b < nu_ref[0])
    def _():
        gu = jnp.dot(x_ref[...].astype(BF16), wgu_s[...], preferred_element_type=F32) + bgu_ref[0]
        gate = jnp.minimum(gu[:, :dff], SWIGLU_LIMIT)
        up = jnp.clip(gu[:, dff:], -SWIGLU_LIMIT, SWIGLU_LIMIT)
        act = (up + 1.0) * (gate * jax.nn.sigmoid(SWIGLU_ALPHA * gate))
        o_ref[...] = jnp.dot(act.astype(BF16), wd_s[...], preferred_element_type=F32) + bd_ref[0]

    @pl.when(b >= nu_ref[0])
    def _():
        o_ref[...] = jnp.zeros_like(o_ref)


def _expert_ffn(x_disp, blk_expert, n_used, w_gu, b_gu, w_down, b_down, rb):
    n_rows, d = x_disp.shape
    ne, _, dff2 = w_gu.shape
    dff = dff2 // 2
    nb = n_rows // rb
    return pl.pallas_call(
        functools.partial(_expert_kernel, dff=dff),
        grid_spec=pltpu.PrefetchScalarGridSpec(
            num_scalar_prefetch=2,
            grid=(nb,),
            in_specs=[
                pl.BlockSpec((rb, d), lambda b, be, nu: (b, 0)),
                pl.BlockSpec((1, d, dff2), lambda b, be, nu: (be[b], 0, 0)),
                pl.BlockSpec((1, 1, dff2), lambda b, be, nu: (be[b], 0, 0)),
                pl.BlockSpec((1, dff, d), lambda b, be, nu: (be[b], 0, 0)),
                pl.BlockSpec((1, 1, d), lambda b, be, nu: (be[b], 0, 0)),
            ],
            out_specs=pl.BlockSpec((rb, d), lambda b, be, nu: (b, 0)),
            scratch_shapes=[pltpu.VMEM((d, dff2), BF16), pltpu.VMEM((dff, d), BF16)],
        ),
        out_shape=jax.ShapeDtypeStruct((n_rows, d), F32),
        compiler_params=_cparams(("arbitrary",)),
        name="moe_experts",
    )(blk_expert, n_used, x_disp, w_gu, b_gu.reshape(ne, 1, dff2), w_down, b_down.reshape(ne, 1, d))


def _combine_kernel(dest_ref, gates_ref, x_ref, mod_ref, g_ref, b_ref, y_hbm, o_ref, buf, sem, *, tm, alpha):
    def issue(r, c):
        for k in range(TOP_K):
            _row_copy(y_hbm, dest_ref[k, r], buf.at[k], r, sem).start()
        return c

    lax.fori_loop(0, tm, issue, 0)

    def drain(r, c):
        for k in range(TOP_K):
            _row_copy(y_hbm, 0, buf.at[k], 0, sem).wait()
        return c

    lax.fori_loop(0, tm, drain, 0)
    y = gates_ref[:, 0:1] * buf[0]
    for k in range(1, TOP_K):
        y = y + gates_ref[:, k:k + 1] * buf[k]
    r = alpha * x_ref[...] + (1.0 + mod_ref[2:3, :]) * y
    o_ref[...] = _layer_norm(r, g_ref[...], b_ref[...])


def _combine_norm(y_disp, dest, gates_col, x, mods, midx, ln_g, ln_b, seq, alpha):
    t, d = x.shape
    tm = MOE_TOK_TILE
    row = lambda i: (i, 0)
    return pl.pallas_call(
        functools.partial(_combine_kernel, tm=tm, alpha=alpha),
        grid=(t // tm,),
        in_specs=[
            pl.BlockSpec((TOP_K, tm), lambda i: (0, i), memory_space=pltpu.SMEM),
            pl.BlockSpec((tm, TOP_K), row),
            pl.BlockSpec((tm, d), row),
            _mod_spec(midx, seq // tm, d),
            pl.BlockSpec((1, d), lambda i: (0, 0)),
            pl.BlockSpec((1, d), lambda i: (0, 0)),
            pl.BlockSpec(memory_space=pl.ANY),
        ],
        out_specs=pl.BlockSpec((tm, d), row),
        out_shape=jax.ShapeDtypeStruct((t, d), F32),
        scratch_shapes=[pltpu.VMEM((TOP_K, tm, d), F32), pltpu.SemaphoreType.DMA],
        compiler_params=_cparams(("arbitrary",)),
        name="moe_combine",
    )(dest, gates_col, x, mods, ln_g.reshape(1, d), ln_b.reshape(1, d), y_disp)


def _moe_block(x, mods, midx, router_w, router_b, w_gu, b_gu, w_down, b_down, ln_g, ln_b, seq, alpha):
    t, d = x.shape
    rb = MOE_ROW_BLOCK
    nb = (t * TOP_K + N_EXPERTS * (rb - 1) + rb - 1) // rb
    h, eidx, rank, gates, counts = _route(x, mods, midx, router_w, router_b, seq)
    dest, blk_expert, n_used = _plan(eidx, rank, counts, rb, nb)
    x_disp = _dispatch(h, dest, nb * rb)
    y_disp = _expert_ffn(x_disp, blk_expert, n_used, w_gu, b_gu, w_down, b_down, rb)
    return _combine_norm(y_disp, dest, gates.T, x, mods, midx, ln_g, ln_b, seq, alpha)


def _inproj_odd_kernel(x_ref, mod_ref, w_ref, wf_ref, fb_ref, zs_ref, q_ref, k_ref, v_ref, f_ref, *, width):
    h = (x_ref[...] * (1.0 + mod_ref[1:2, :]) + mod_ref[0:1, :]).astype(BF16)
    z = jnp.dot(h, w_ref[...], preferred_element_type=F32)
    zs_ref[...] = z[:, :width]
    q_ref[...] = (z[:, width:2 * width] * (HEAD_DIM ** -0.5)).astype(BF16)
    k_ref[...] = z[:, 2 * width:3 * width].astype(BF16)
    v_ref[...] = z[:, 3 * width:4 * width].astype(BF16)
    f_ref[...] = lax.dot_general(wf_ref[...], h, (((1,), (1,)), ((), ())),
                                 preferred_element_type=F32) + fb_ref[...]


def _inproj_odd(x, mods, midx, w_main_bf, wf_t_bf, forget_b, seq):
    t, d = x.shape
    n = w_main_bf.shape[1]
    width = n // 4
    nh = wf_t_bf.shape[0]
    tm = ROW_TILE
    row = lambda i: (i, 0)
    return pl.pallas_call(
        functools.partial(_inproj_odd_kernel, width=width),
        grid=(t // tm,),
        in_specs=[
            pl.BlockSpec((tm, d), row),
            _mod_spec(midx, seq // tm, d),
            pl.BlockSpec((d, n), lambda i: (0, 0)),
            pl.BlockSpec((nh, d), lambda i: (0, 0)),
            pl.BlockSpec((nh, 1), lambda i: (0, 0)),
        ],
        out_specs=[pl.BlockSpec((tm, width), row)] * 4 + [pl.BlockSpec((nh, tm), lambda i: (0, i))],
        out_shape=[jax.ShapeDtypeStruct((t, width), F32)] + [jax.ShapeDtypeStruct((t, width), BF16)] * 3
        + [jax.ShapeDtypeStruct((nh, t), F32)],
        compiler_params=_cparams(("parallel",)),
        name="inproj_odd",
    )(x, mods, w_main_bf, wf_t_bf, forget_b.reshape(nh, 1))


def _fcum_kernel(f_ref, o_ref, carry_ref, *, ts):
    @pl.when(pl.program_id(1) == 0)
    def _():
        carry_ref[...] = jnp.zeros_like(carry_ref)

    x = f_ref[...]
    ls = jnp.minimum(x, 0.0) - jnp.log(1.0 + jnp.exp(-jnp.abs(x)))
    upto = (lax.broadcasted_iota(I32, (ts, ts), 0) <= lax.broadcasted_iota(I32, (ts, ts), 1)).astype(F32)
    cs = jnp.dot(ls, upto, precision=HIGHEST, preferred_element_type=F32) + carry_ref[...]
    o_ref[...] = -cs
    carry_ref[...] = cs[:, ts - 1:ts]


def _forget_prefix(flog_t, bsz, seq):
    nh = flog_t.shape[0]
    ts = FCUM_TILE
    nt = seq // ts
    return pl.pallas_call(
        functools.partial(_fcum_kernel, ts=ts),
        grid=(bsz, nt),
        in_specs=[pl.BlockSpec((nh, ts), lambda b, j: (0, b * nt + j))],
        out_specs=pl.BlockSpec((None, nh, ts), lambda b, j: (b, 0, j)),
        out_shape=jax.ShapeDtypeStruct((bsz, nh, seq), F32),
        scratch_shapes=[pltpu.VMEM((nh, 1), F32)],
        compiler_params=_cparams(("parallel", "arbitrary")),
        name="forget_prefix",
    )(flog_t)


def _fox_kernel(q_ref, k_ref, v_ref, nf_ref, o_ref, acc_ref, m_ref, l_ref, *, tq, tk, nk):
    hp = pl.program_id(1)
    qi = pl.program_id(2)
    lane = lax.broadcasted_iota(I32, (tq, LANES), 1)
    low = lane < HEAD_DIM
    qp = q_ref[...]
    qms = [jnp.where(low, qp, jnp.zeros_like(qp)), jnp.where(low, jnp.zeros_like(qp), qp)]
    acc_ref[...] = jnp.zeros_like(acc_ref)
    m_ref[...] = jnp.full_like(m_ref, NEG_INF)
    l_ref[...] = jnp.zeros_like(l_ref)

    def block(kb, masked):
        off = pl.multiple_of(kb * tk, tk)
        kblk = k_ref[pl.ds(off, tk), :]
        vblk = v_ref[pl.ds(off, tk), :]
        pvs, alphas = [], []
        for hh in range(2):
            nf = nf_ref[pl.ds((2 * hp + hh) * nk + kb, 1), :]
            s = lax.dot_general(qms[hh], kblk, (((1,), (1,)), ((), ())), preferred_element_type=F32) + nf
            if masked:
                row = lax.broadcasted_iota(I32, (tq, tk), 0)
                col = lax.broadcasted_iota(I32, (tq, tk), 1)
                s = jnp.where(col <= row, s, NEG_INF)
            m_old = m_ref[hh]
            m_new = jnp.maximum(m_old, jnp.max(s, axis=-1, keepdims=True))
            alpha = jnp.exp(m_old - m_new)
            p = jnp.exp(s - m_new)
            l_ref[hh] = alpha * l_ref[hh] + jnp.sum(p, axis=-1, keepdims=True)
            m_ref[hh] = m_new
            pvs.append(jnp.dot(p.astype(BF16), vblk, preferred_element_type=F32))
            alphas.append(alpha)
        acc_ref[...] = jnp.where(low, alphas[0], alphas[1]) * acc_ref[...] + jnp.where(low, pvs[0], pvs[1])

    def body(kb, c):
        block(kb, False)
        return c

    lax.fori_loop(0, qi, body, 0)
    block(qi, True)
    o_ref[...] = (acc_ref[...] / jnp.where(low, l_ref[0], l_ref[1])).astype(o_ref.dtype)


def _forgetting_attention(q, k, v, neg_f, bsz, seq):
    t, width = q.shape
    n_pairs = width // LANES
    tq, tk = FOX_TQ, FOX_TK
    nq, nk = seq // tq, seq // tk
    nh = neg_f.shape[1]
    nf = neg_f.reshape(bsz, nh * nk, tk)
    return pl.pallas_call(
        functools.partial(_fox_kernel, tq=tq, tk=tk, nk=nk),
        grid=(bsz, n_pairs, nq),
        in_specs=[
            pl.BlockSpec((tq, LANES), lambda b, h, i: (b * nq + i, h)),
            pl.BlockSpec((seq, LANES), lambda b, h, i: (b, h)),
            pl.BlockSpec((seq, LANES), lambda b, h, i: (b, h)),
            pl.BlockSpec((None, nh * nk, tk), lambda b, h, i: (b, 0, 0)),
        ],
        out_specs=pl.BlockSpec((tq, LANES), lambda b, h, i: (b * nq + i, h)),
        out_shape=jax.ShapeDtypeStruct((t, width), BF16),
        scratch_shapes=[pltpu.VMEM((tq, LANES), F32), pltpu.VMEM((2, tq, 1), F32), pltpu.VMEM((2, tq, 1), F32)],
        compiler_params=_cparams(("parallel", "parallel", "arbitrary")),
        name="forgetting_attention",
    )(q, k, v, nf)


def _s5_prep_kernel(lre_ref, lim_ref, ldt_ref, bre_ref, bim_ref, pw_re, pw_im, tb_re, tb_im, bb_re, bb_im,
                    *, ts, nsteps):
    dt = jnp.exp(ldt_ref[...])
    lre = jnp.minimum(lre_ref[...], -1e-4)
    lim = lim_ref[...]
    mag = jnp.exp(lre * dt)
    a_re = mag * jnp.cos(lim * dt)
    a_im = mag * jnp.sin(lim * dt)
    den = lre * lre + lim * lim
    nre, nim = a_re - 1.0, a_im
    g_re = (nre * lre + nim * lim) / den
    g_im = (nim * lre - nre * lim) / den
    n = lre.shape[-1]
    per = n // bre_ref.shape[0]
    for j in range(bre_ref.shape[0]):
        gr = g_re[:, per * j:per * (j + 1)]
        gi = g_im[:, per * j:per * (j + 1)]
        bb_re[j] = (gr * bre_ref[j] - gi * bim_ref[j]).astype(bb_re.dtype)
        bb_im[j] = (gr * bim_ref[j] + gi * bre_ref[j]).astype(bb_im.dtype)
    tb_re[0:1, :] = a_re
    tb_im[0:1, :] = a_im

    def step(j, c):
        pr = tb_re[pl.ds(j - 1, 1), :]
        pi = tb_im[pl.ds(j - 1, 1), :]
        tb_re[pl.ds(j, 1), :] = pr * a_re - pi * a_im
        tb_im[pl.ds(j, 1), :] = pr * a_im + pi * a_re
        return c

    lax.fori_loop(1, ts, step, 0)
    for m in range(nsteps):
        pw_re[m:m + 1, :] = tb_re[2 ** m - 1:2 ** m, :]
        pw_im[m:m + 1, :] = tb_im[2 ** m - 1:2 ** m, :]
    for m in range(nsteps, pw_re.shape[0]):
        pw_re[m:m + 1, :] = jnp.zeros((1, n), F32)
        pw_im[m:m + 1, :] = jnp.zeros((1, n), F32)


def _block_diag_slabs(blocks, slabs):
    g, r, c = blocks.shape
    per = g // slabs
    eye = jnp.eye(per, dtype=bool)[None, :, None, :, None]
    bd = jnp.where(eye, blocks.reshape(slabs, per, r, 1, c), jnp.zeros((), blocks.dtype))
    return bd.reshape(slabs, per * r, per * c)


def _s5_kernel(u_ref, pw_re, pw_im, tb_re, tb_im, bb_re, bb_im, cc_re, cc_im, dsk_ref, gw_ref, gb_ref, o_ref,
               ar, ai, br, bi, cr, ci, *, ts, head, nsteps, slabs):
    @pl.when(pl.program_id(1) == 0)
    def _():
        cr[...] = jnp.zeros_like(cr)
        ci[...] = jnp.zeros_like(ci)

    n = ar.shape[1]
    per = n // slabs
    u = u_ref[...]
    ub = u.astype(BF16)
    zeros_head = jnp.zeros((head, n), F32)
    for ref in (ar, ai, br, bi):
        ref[0:head, :] = zeros_head
    for j in range(slabs):
        uj = ub[:, LANES * j:LANES * (j + 1)]
        ar[head:head + ts, per * j:per * (j + 1)] = jnp.dot(uj, bb_re[j], preferred_element_type=F32)
        ai[head:head + ts, per * j:per * (j + 1)] = jnp.dot(uj, bb_im[j], preferred_element_type=F32)
    src, dst = (ar, ai), (br, bi)
    for m in range(nsteps):
        d = 2 ** m
        for j in range(slabs):
            cols = slice(per * j, per * (j + 1))
            mr = pw_re[m:m + 1, cols]
            mi = pw_im[m:m + 1, cols]
            xr = src[0][head:head + ts, cols]
            xi = src[1][head:head + ts, cols]
            sr = src[0][head - d:head - d + ts, cols]
            si = src[1][head - d:head - d + ts, cols]
            dst[0][head:head + ts, cols] = xr + mr * sr - mi * si
            dst[1][head:head + ts, cols] = xi + mr * si + mi * sr
        src, dst = dst, src
    for j in range(slabs):
        cols = slice(per * j, per * (j + 1))
        c_r = cr[:, cols]
        c_i = ci[:, cols]
        p_r = tb_re[:, cols]
        p_i = tb_im[:, cols]
        xr = src[0][head:head + ts, cols] + p_r * c_r - p_i * c_i
        xi = src[1][head:head + ts, cols] + p_r * c_i + p_i * c_r
        cr[:, cols] = xr[ts - 1:ts, :]
        ci[:, cols] = xi[ts - 1:ts, :]
        yj = jnp.dot(xr.astype(BF16), cc_re[j], preferred_element_type=F32)
        yj = yj - jnp.dot(xi.astype(BF16), cc_im[j], preferred_element_type=F32)
        dst[0][head:head + ts, LANES * j:LANES * (j + 1)] = yj
    width = slabs * LANES
    y = dst[0][head:head + ts, 0:width] + dsk_ref[...] * u
    y = 0.5 * y * (1.0 + jnp.tanh(math.sqrt(2.0 / math.pi) * (y + 0.044715 * (y * y * y))))
    gate = jnp.dot(y.astype(BF16), gw_ref[...], preferred_element_type=F32) + gb_ref[...]
    o_ref[...] = (y * jax.nn.sigmoid(gate)).astype(o_ref.dtype)


def _s5_mixer(u, lam_re, lam_im, log_dt, b_re, b_im, c_re, c_im, d_skip, glu_w, glu_b, bsz, seq):
    t, width = u.shape
    groups, states = lam_re.shape
    n = groups * states
    slabs = width // LANES
    ts = S5_TILE
    nsteps = int(math.log2(ts))
    head = ts // 2
    flat = lambda a: a.reshape(1, n)
    ldt = jnp.repeat(log_dt, states).reshape(1, n)
    bre_bd = _block_diag_slabs(jnp.swapaxes(b_re, 1, 2), slabs)
    bim_bd = _block_diag_slabs(jnp.swapaxes(b_im, 1, 2), slabs)
    cre_bd = _block_diag_slabs(jnp.swapaxes(c_re, 1, 2), slabs).astype(BF16)
    cim_bd = _block_diag_slabs(jnp.swapaxes(c_im, 1, 2), slabs).astype(BF16)
    full = lambda a: pl.BlockSpec(a.shape, lambda *_: (0,) * a.ndim)
    prep_in = (flat(lam_re), flat(lam_im), ldt, bre_bd, bim_bd)
    pw_re, pw_im, tb_re, tb_im, bb_re, bb_im = pl.pallas_call(
        functools.partial(_s5_prep_kernel, ts=ts, nsteps=nsteps),
        grid=(1,),
        in_specs=[full(a) for a in prep_in],
        out_specs=[pl.BlockSpec((8, n), lambda i: (0, 0))] * 2 + [pl.BlockSpec((ts, n), lambda i: (0, 0))] * 2
        + [pl.BlockSpec(bre_bd.shape, lambda i: (0, 0, 0))] * 2,
        out_shape=[jax.ShapeDtypeStruct((8, n), F32)] * 2 + [jax.ShapeDtypeStruct((ts, n), F32)] * 2
        + [jax.ShapeDtypeStruct(bre_bd.shape, BF16)] * 2,
        compiler_params=_cparams(("arbitrary",)),
        name="s5_prep",
    )(*prep_in)
    nt = seq // ts
    consts = (pw_re, pw_im, tb_re, tb_im, bb_re, bb_im, cre_bd, cim_bd, d_skip.reshape(1, width),
              glu_w.astype(BF16), glu_b.reshape(1, width))
    return pl.pallas_call(
        functools.partial(_s5_kernel, ts=ts, head=head, nsteps=nsteps, slabs=slabs),
        grid=(bsz, nt),
        in_specs=[pl.BlockSpec((ts, width), lambda b, j: (b * nt + j, 0))] + [full(a) for a in consts],
        out_specs=pl.BlockSpec((ts, width), lambda b, j: (b * nt + j, 0)),
        out_shape=jax.ShapeDtypeStruct((t, width), BF16),
        scratch_shapes=[pltpu.VMEM((head + ts, n), F32)] * 4 + [pltpu.VMEM((1, n), F32)] * 2,
        compiler_params=_cparams(("parallel", "arbitrary")),
        name="s5_mixer",
    )(u, *consts)


def kernel(x, c, mod_w, mod_b, ln_g, ln_b, even_w_in, pool_w, pool_scale, rel_bias, even_w_out, odd_w_in, forget_b, ssm_lam_re, ssm_lam_im, ssm_log_dt, ssm_b_re, ssm_b_im, ssm_c_re, ssm_c_im, ssm_d, ssm_glu_w, ssm_glu_b, odd_w_out, router_w, router_b, exp_w_gu, exp_b_gu, exp_w_down, exp_b_down):
    bsz, seq, d = x.shape
    depth = mod_w.shape[0]
    alpha = (2.0 * depth) ** 0.25
    mods = _ada_mods(c, mod_w.reshape(depth * 2, d, 3 * d), mod_b.reshape(depth * 2, 3 * d))
    xt = x.reshape(bsz * seq, d)
    for layer in range(depth):
        i = layer // 2
        if layer % 2 == 0:
            zp, q, k, v = _inproj_even(xt, mods, 2 * layer, even_w_in[i].astype(BF16), seq)
            ya = _pool_mixer(zp, pool_w[i].astype(BF16), pool_scale[i], seq)
            yb = _chunk_attention(q, k, v, _cattn_bias_table(rel_bias[i]), bsz, seq)
            w_out = even_w_out[i]
        else:
            width = ssm_d.shape[1] * ssm_d.shape[2]
            w_in = odd_w_in[i]
            n_heads = forget_b.shape[1]
            head_rows = 16
            wf_t = jnp.zeros((head_rows, d), BF16).at[:n_heads].set(w_in[:, 4 * width:].T.astype(BF16))
            fb = jnp.zeros((head_rows,), F32).at[:n_heads].set(forget_b[i])
            zs, q, k, v, flog_t = _inproj_odd(xt, mods, 2 * layer, w_in[:, :4 * width].astype(BF16), wf_t, fb, seq)
            ya = _s5_mixer(zs, ssm_lam_re[i], ssm_lam_im[i], ssm_log_dt[i], ssm_b_re[i], ssm_b_im[i],
                           ssm_c_re[i], ssm_c_im[i], ssm_d[i], ssm_glu_w[i], ssm_glu_b[i], bsz, seq)
            yb = _forgetting_attention(q, k, v, _forget_prefix(flog_t, bsz, seq), bsz, seq)
            w_out = odd_w_out[i]
        xt = _outproj_norm(ya, yb, xt, mods, 2 * layer, w_out.astype(BF16), ln_g[layer, 0], ln_b[layer, 0],
                           seq, alpha)
        xt = _moe_block(xt, mods, 2 * layer + 1, router_w[layer], router_b[layer], exp_w_gu[layer],
                        exp_b_gu[layer], exp_w_down[layer], exp_b_down[layer], ln_g[layer, 1], ln_b[layer, 1],
                        seq, alpha)
    return xt.reshape(bsz, seq, d)
```

```python
import functools
import math

import jax
import jax.numpy as jnp
from jax import lax
from jax.experimental import pallas as pl
from jax.experimental.pallas import tpu as pltpu

F32 = jnp.float32
BF16 = jnp.bfloat16
I32 = jnp.int32
HIGHEST = lax.Precision.HIGHEST

LANES = 128
SUBLANES = 8
VMEM_LIMIT_BYTES = 60000 * 1024

HEAD_DIM = 64
CHUNK = 64
LEFT_CHUNKS = 8
REL_CLIP = 128
POOL_WINDOWS = (2, 4, 8, 16)
POOL_HALO = 16
SSM_GROUP_DIM = 16
SSM_STATE = 64
N_EXPERTS = 32
TOP_K = 4
SWIGLU_LIMIT = 7.0
SWIGLU_ALPHA = 1.702
LN_EPS = 1e-5
NEG_INF = -1e30
LOG2E = math.log2(math.e)
FOX_BIAS_PARTS = 3

ROW_TILE = 512
CATTN_Q = 2 * CHUNK
CATTN_BAND = (LEFT_CHUNKS + 2) * CHUNK
FOX_TQ = 512
FOX_TK = 512
S5_TILE = 128
FCUM_TILE = 512
TILE_ROWS = SUBLANES
MOE_ROW_BLOCK = 256
MOE_TOK_TILE = 256
ROUTE_TILE = 512


def _cparams(sem):
    return pltpu.CompilerParams(dimension_semantics=sem, vmem_limit_bytes=VMEM_LIMIT_BYTES)


def _layer_norm(r, g, b):
    mu = jnp.mean(r, axis=-1, keepdims=True)
    d = r - mu
    var = jnp.mean(d * d, axis=-1, keepdims=True)
    return d * lax.rsqrt(var + LN_EPS) * g + b


def _mod_kernel(c_ref, w_ref, b_ref, o_ref):
    c = c_ref[...]
    s = c * jax.nn.sigmoid(c)
    o_ref[0] = jnp.dot(s, w_ref[0], precision=HIGHEST, preferred_element_type=F32) + b_ref[0]


def _ada_mods(c, mod_w, mod_b):
    bsz, d = c.shape
    m = mod_w.shape[0]
    rows = 8
    c_pad = jnp.zeros((rows, d), F32).at[:bsz].set(c)
    out = pl.pallas_call(
        _mod_kernel,
        grid=(m, 3),
        in_specs=[
            pl.BlockSpec((rows, d), lambda i, j: (0, 0)),
            pl.BlockSpec((1, d, d), lambda i, j: (i, 0, j)),
            pl.BlockSpec((1, 1, d), lambda i, j: (i, 0, j)),
        ],
        out_specs=pl.BlockSpec((1, rows, d), lambda i, j: (i, 0, j)),
        out_shape=jax.ShapeDtypeStruct((m, rows, 3 * d), F32),
        compiler_params=_cparams(("parallel", "parallel")),
        name="ada_mod",
    )(c_pad, mod_w, mod_b.reshape(m, 1, 3 * d))
    return out[:, :bsz].reshape(m, bsz, 3, d)


def _mod_spec(midx, tiles_per_batch, d):
    return pl.BlockSpec((None, None, 3, d), lambda i: (midx, i // tiles_per_batch, 0, 0))


def _inproj_even_kernel(x_ref, mod_ref, w_ref, zp_ref, q_ref, k_ref, v_ref, *, width):
    h = x_ref[...] * (1.0 + mod_ref[1:2, :]) + mod_ref[0:1, :]
    z = jnp.dot(h.astype(BF16), w_ref[...], preferred_element_type=F32)
    zp_ref[...] = z[:, :width]
    q_ref[...] = (z[:, width:2 * width] * (HEAD_DIM ** -0.5)).astype(BF16)
    k_ref[...] = z[:, 2 * width:3 * width].astype(BF16)
    v_ref[...] = z[:, 3 * width:4 * width].astype(BF16)


def _inproj_even(x, mods, midx, w_in_bf, seq):
    t, d = x.shape
    n = w_in_bf.shape[1]
    width = n // 4
    tm = ROW_TILE
    row = lambda i: (i, 0)
    return pl.pallas_call(
        functools.partial(_inproj_even_kernel, width=width),
        grid=(t // tm,),
        in_specs=[
            pl.BlockSpec((tm, d), row),
            _mod_spec(midx, seq // tm, d),
            pl.BlockSpec((d, n), lambda i: (0, 0)),
        ],
        out_specs=[pl.BlockSpec((tm, width), row)] * 4,
        out_shape=[jax.ShapeDtypeStruct((t, width), F32)] + [jax.ShapeDtypeStruct((t, width), BF16)] * 3,
        compiler_params=_cparams(("parallel",)),
        name="inproj_even",
    )(x, mods, w_in_bf)


def _pool_kernel(zp_ref, halo_ref, w_ref, sc_ref, o_ref, xs_ref, *, ts, seq):
    i = pl.program_id(0)
    t0 = lax.rem(i * ts, seq)
    xs_ref[0:POOL_HALO, :] = jnp.where(t0 == 0, 0.0, halo_ref[...])
    xs_ref[POOL_HALO:POOL_HALO + ts, :] = zp_ref[...]
    pos = lax.broadcasted_iota(I32, (ts, LANES), 0) + t0
    for g, win in enumerate(POOL_WINDOWS):
        cols = slice(LANES * g, LANES * (g + 1))
        u = xs_ref[POOL_HALO:POOL_HALO + ts, cols]
        acc = u
        for k in range(1, win):
            acc = acc + xs_ref[POOL_HALO - k:POOL_HALO - k + ts, cols]
        cnt = jnp.minimum(pos + 1, win).astype(F32)
        dlt = acc / cnt - u
        y = jnp.dot(dlt.astype(BF16), w_ref[g], preferred_element_type=F32)
        o_ref[:, cols] = (y * sc_ref[:, cols]).astype(o_ref.dtype)


def _pool_mixer(zp, w_pool_bf, pool_scale, seq):
    t, width = zp.shape
    ts = ROW_TILE
    hb = ts // POOL_HALO
    return pl.pallas_call(
        functools.partial(_pool_kernel, ts=ts, seq=seq),
        grid=(t // ts,),
        in_specs=[
            pl.BlockSpec((ts, width), lambda i: (i, 0)),
            pl.BlockSpec((POOL_HALO, width), lambda i: (jnp.maximum(i * hb - 1, 0), 0)),
            pl.BlockSpec(w_pool_bf.shape, lambda i: (0, 0, 0)),
            pl.BlockSpec((1, width), lambda i: (0, 0)),
        ],
        out_specs=pl.BlockSpec((ts, width), lambda i: (i, 0)),
        out_shape=jax.ShapeDtypeStruct((t, width), BF16),
        scratch_shapes=[pltpu.VMEM((ts + POOL_HALO, width), F32)],
        compiler_params=_cparams(("parallel",)),
        name="pool_mixer",
    )(zp, zp, w_pool_bf, pool_scale.reshape(1, width))


def _cbias_kernel(r_ref, o_ref):
    base = jnp.broadcast_to(r_ref[...], (CATTN_Q, r_ref.shape[-1]))
    rolled = pltpu.roll(base, 0, 1, stride=1, stride_axis=0)[:, :CATTN_BAND]
    qi = lax.broadcasted_iota(I32, (CATTN_Q, CATTN_BAND), 0)
    kp = lax.broadcasted_iota(I32, (CATTN_Q, CATTN_BAND), 1)
    lo = (qi // CHUNK) * CHUNK
    valid = (kp >= lo) & (kp < lo + (LEFT_CHUNKS + 1) * CHUNK)
    o_ref[...] = jnp.where(valid, rolled, NEG_INF)


def _cattn_bias_table(rel_bias):
    h = rel_bias.shape[0]
    wide = 1024
    far = LEFT_CHUNKS * CHUNK - REL_CLIP + 1
    tail = wide - far - (2 * REL_CLIP - 1)
    last = rel_bias[:, 2 * REL_CLIP:]
    base = jnp.concatenate(
        [jnp.broadcast_to(last, (h, far)), rel_bias[:, 1:2 * REL_CLIP][:, ::-1], jnp.broadcast_to(last, (h, tail))],
        axis=1).reshape(h, 1, wide)
    return pl.pallas_call(
        _cbias_kernel,
        grid=(h,),
        in_specs=[pl.BlockSpec((None, 1, wide), lambda i: (i, 0, 0))],
        out_specs=pl.BlockSpec((None, CATTN_Q, CATTN_BAND), lambda i: (i, 0, 0)),
        out_shape=jax.ShapeDtypeStruct((h, CATTN_Q, CATTN_BAND), F32),
        compiler_params=_cparams(("parallel",)),
        name="cattn_bias",
    )(base)


def _cattn_kernel(q_ref, k_ref, v_ref, bias_ref, o_ref, *, n_pairs):
    p = pl.program_id(1)
    start = pl.multiple_of(p * CATTN_Q, CATTN_Q)
    kpos = lax.broadcasted_iota(I32, (CATTN_Q, CATTN_BAND), 1) + start
    real_key = kpos >= LEFT_CHUNKS * CHUNK
    lane = lax.broadcasted_iota(I32, (CATTN_Q, LANES), 1)
    low = lane < HEAD_DIM
    for hp in range(n_pairs):
        cols = slice(LANES * hp, LANES * (hp + 1))
        qp = q_ref[:, cols]
        kb = k_ref[pl.ds(start, CATTN_BAND), cols]
        vb = v_ref[pl.ds(start, CATTN_BAND), cols]
        outs = []
        for hh in range(2):
            qm = jnp.where(low if hh == 0 else jnp.logical_not(low), qp, jnp.zeros_like(qp))
            s = lax.dot_general(qm, kb, (((1,), (1,)), ((), ())), preferred_element_type=F32)
            s = jnp.where(real_key, s + bias_ref[2 * hp + hh], NEG_INF)
            m = jnp.max(s, axis=-1, keepdims=True)
            e = jnp.exp(s - m)
            l = jnp.sum(e, axis=-1, keepdims=True)
            o = jnp.dot(e.astype(BF16), vb, preferred_element_type=F32)
            outs.append(o / l)
        o_ref[:, cols] = jnp.where(low, outs[0], outs[1]).astype(o_ref.dtype)


def _chunk_attention(q, k, v, bias, bsz, seq):
    t, width = q.shape
    n_pairs = width // LANES
    pad = LEFT_CHUNKS * CHUNK
    kp = jnp.pad(k.reshape(bsz, seq, width), ((0, 0), (pad, 0), (0, 0)))
    vp = jnp.pad(v.reshape(bsz, seq, width), ((0, 0), (pad, 0), (0, 0)))
    nq = seq // CATTN_Q
    return pl.pallas_call(
        functools.partial(_cattn_kernel, n_pairs=n_pairs),
        grid=(bsz, nq),
        in_specs=[
            pl.BlockSpec((CATTN_Q, width), lambda b, p: (b * nq + p, 0)),
            pl.BlockSpec((None, seq + pad, width), lambda b, p: (b, 0, 0)),
            pl.BlockSpec((None, seq + pad, width), lambda b, p: (b, 0, 0)),
            pl.BlockSpec(bias.shape, lambda b, p: (0, 0, 0)),
        ],
        out_specs=pl.BlockSpec((CATTN_Q, width), lambda b, p: (b * nq + p, 0)),
        out_shape=jax.ShapeDtypeStruct((t, width), BF16),
        compiler_params=_cparams(("parallel", "arbitrary")),
        name="chunk_attention",
    )(q, kp, vp, bias)


def _outproj_kernel(ya_ref, yb_ref, x_ref, mod_ref, w_ref, g_ref, b_ref, o_ref, *, alpha, half):
    y = jnp.dot(ya_ref[...], w_ref[0:half, :], preferred_element_type=F32)
    y = y + jnp.dot(yb_ref[...], w_ref[half:2 * half, :], preferred_element_type=F32)
    r = alpha * x_ref[...] + (1.0 + mod_ref[2:3, :]) * y
    o_ref[...] = _layer_norm(r, g_ref[...], b_ref[...])


def _outproj_norm(ya, yb, x, mods, midx, w_out_bf, ln_g, ln_b, seq, alpha):
    t, d = x.shape
    half = ya.shape[1]
    tm = ROW_TILE
    row = lambda i: (i, 0)
    return pl.pallas_call(
        functools.partial(_outproj_kernel, alpha=alpha, half=half),
        grid=(t // tm,),
        in_specs=[
            pl.BlockSpec((tm, half), row),
            pl.BlockSpec((tm, half), row),
            pl.BlockSpec((tm, d), row),
            _mod_spec(midx, seq // tm, d),
            pl.BlockSpec((2 * half, d), lambda i: (0, 0)),
            pl.BlockSpec((1, d), lambda i: (0, 0)),
            pl.BlockSpec((1, d), lambda i: (0, 0)),
        ],
        out_specs=pl.BlockSpec((tm, d), row),
        out_shape=jax.ShapeDtypeStruct((t, d), F32),
        compiler_params=_cparams(("parallel",)),
        name="outproj_norm",
    )(ya, yb, x, mods, w_out_bf, ln_g.reshape(1, d), ln_b.reshape(1, d))


def _route_kernel(x_ref, mod_ref, rw_ref, rb_ref, h_ref, e_ref, r_ref, g_ref, c_ref, carry_ref, *, tm):
    @pl.when(pl.program_id(0) == 0)
    def _():
        carry_ref[...] = jnp.zeros_like(carry_ref)

    h = x_ref[...] * (1.0 + mod_ref[1:2, :]) + mod_ref[0:1, :]
    _store_token_tiles(h_ref, h)
    logits = lax.dot_general(rw_ref[...], h, (((1,), (1,)), ((), ())), precision=HIGHEST,
                             preferred_element_type=F32) + rb_ref[...]
    eio = lax.broadcasted_iota(I32, (N_EXPERTS, tm), 0)
    vals, hots = [], []
    for k in range(TOP_K):
        m = jnp.max(logits, axis=0, keepdims=True)
        idx = jnp.min(jnp.where(logits == m, eio, N_EXPERTS), axis=0, keepdims=True)
        hot = eio == idx
        e_ref[k:k + 1, :] = idx
        vals.append(m)
        hots.append(hot)
        logits = jnp.where(hot, -jnp.inf, logits)
    exps = [jnp.exp(v - vals[0]) for v in vals]
    denom = exps[0] + exps[1] + exps[2] + exps[3]
    for k in range(TOP_K):
        g_ref[k:k + 1, :] = exps[k] / denom
    cnt = jnp.zeros((N_EXPERTS, tm), F32)
    for hot in hots:
        cnt = cnt + hot.astype(F32)
    before = (lax.broadcasted_iota(I32, (tm, tm), 0) < lax.broadcasted_iota(I32, (tm, tm), 1))
    prefix = jnp.dot(cnt.astype(BF16), before.astype(BF16), preferred_element_type=F32)
    tot = prefix + carry_ref[...]
    for k in range(TOP_K):
        r_ref[k:k + 1, :] = jnp.sum(jnp.where(hots[k], tot, 0.0), axis=0, keepdims=True).astype(I32)
    new_carry = carry_ref[...] + jnp.sum(cnt, axis=1, keepdims=True)
    carry_ref[...] = new_carry
    c_ref[...] = jnp.broadcast_to(new_carry, c_ref.shape)


def _route(x, mods, midx, router_w, router_b, seq):
    t, d = x.shape
    tm = ROUTE_TILE
    tok = lambda i: (0, i)
    return pl.pallas_call(
        functools.partial(_route_kernel, tm=tm),
        grid=(t // tm,),
        in_specs=[
            pl.BlockSpec((tm, d), lambda i: (i, 0)),
            _mod_spec(midx, seq // tm, d),
            pl.BlockSpec((N_EXPERTS, d), lambda i: (0, 0)),
            pl.BlockSpec((N_EXPERTS, 1), lambda i: (0, 0)),
        ],
        out_specs=[
            pl.BlockSpec((tm * TILE_ROWS, LANES), lambda i: (i, 0)),
            pl.BlockSpec((TOP_K, tm), tok),
            pl.BlockSpec((TOP_K, tm), tok),
            pl.BlockSpec((TOP_K, tm), tok),
            pl.BlockSpec((N_EXPERTS, LANES), lambda i: (0, 0)),
        ],
        out_shape=[
            jax.ShapeDtypeStruct((t * TILE_ROWS, LANES), F32),
            jax.ShapeDtypeStruct((TOP_K, t), I32),
            jax.ShapeDtypeStruct((TOP_K, t), I32),
            jax.ShapeDtypeStruct((TOP_K, t), F32),
            jax.ShapeDtypeStruct((N_EXPERTS, LANES), F32),
        ],
        scratch_shapes=[pltpu.VMEM((N_EXPERTS, 1), F32)],
        compiler_params=_cparams(("arbitrary",)),
        name="moe_route",
    )(x, mods, router_w.T, router_b.reshape(N_EXPERTS, 1))


def _plan_kernel(e_ref, r_ref, ccol_ref, crow_ref, d_ref, be_ref, nu_ref, *, tm, rb, nbp):
    inv = 1.0 / rb
    pad_col = jnp.floor((ccol_ref[...] + (rb - 1)) * inv) * rb
    pad_row = jnp.floor((crow_ref[0:1, :] + (rb - 1)) * inv) * rb
    ei = lax.broadcasted_iota(I32, (N_EXPERTS, LANES), 0)
    li = lax.broadcasted_iota(I32, (N_EXPERTS, LANES), 1)
    pstart = jnp.sum(jnp.where(li < ei, jnp.broadcast_to(pad_row, (N_EXPERTS, LANES)), 0.0),
                     axis=1, keepdims=True)
    pend = pstart + pad_col[:, 0:1]
    eio = lax.broadcasted_iota(I32, (N_EXPERTS, tm), 0)
    for k in range(TOP_K):
        base = jnp.sum(jnp.where(eio == e_ref[k:k + 1, :], pstart, 0.0), axis=0, keepdims=True)
        d_ref[k:k + 1, :] = base.astype(I32) + r_ref[k:k + 1, :]
    row0 = (lax.broadcasted_iota(I32, (N_EXPERTS, nbp), 1) * rb).astype(F32)
    be = jnp.sum((pend <= row0).astype(F32), axis=0, keepdims=True)
    be_ref[...] = jnp.minimum(be, N_EXPERTS - 1.0).astype(I32)
    used = jnp.sum(pad_col[:, 0:1], axis=0, keepdims=True) * inv
    nu_ref[...] = jnp.broadcast_to(used, nu_ref.shape).astype(I32)


def _plan(eidx, rank, counts_col, rb, nb):
    t = eidx.shape[1]
    tm = ROUTE_TILE
    nbp = -(-nb // LANES) * LANES
    counts_row = jnp.zeros((8, LANES), F32).at[:, :N_EXPERTS].set(counts_col[:, 0][None, :])
    tok = lambda i: (0, i)
    dest, be, nu = pl.pallas_call(
        functools.partial(_plan_kernel, tm=tm, rb=rb, nbp=nbp),
        grid=(t // tm,),
        in_specs=[
            pl.BlockSpec((TOP_K, tm), tok),
            pl.BlockSpec((TOP_K, tm), tok),
            pl.BlockSpec((N_EXPERTS, LANES), lambda i: (0, 0)),
            pl.BlockSpec((8, LANES), lambda i: (0, 0)),
        ],
        out_specs=[
            pl.BlockSpec((TOP_K, tm), tok),
            pl.BlockSpec((1, nbp), lambda i: (0, 0)),
            pl.BlockSpec((1, LANES), lambda i: (0, 0)),
        ],
        out_shape=[
            jax.ShapeDtypeStruct((TOP_K, t), I32),
            jax.ShapeDtypeStruct((1, nbp), I32),
            jax.ShapeDtypeStruct((1, LANES), I32),
        ],
        compiler_params=_cparams(("arbitrary",)),
        name="moe_plan",
    )(eidx, rank, counts_col, counts_row)
    return dest, be.reshape(nbp), nu[0, :1]


def _store_token_tiles(ref, x):
    n = x.shape[0]
    for s in range(TILE_ROWS):
        ref[pl.ds(s, n, stride=TILE_ROWS), :] = x[:, LANES * s:LANES * (s + 1)]


def _load_token_tiles(ref, n):
    return jnp.concatenate([ref[pl.ds(s, n, stride=TILE_ROWS), :] for s in range(TILE_ROWS)], axis=1)


def _tile_copy(src_ref, src_tok, dst_ref, dst_tok, sem):
    src = src_ref.at[pl.ds(pl.multiple_of(src_tok * TILE_ROWS, TILE_ROWS), TILE_ROWS)]
    dst = dst_ref.at[pl.ds(pl.multiple_of(dst_tok * TILE_ROWS, TILE_ROWS), TILE_ROWS)]
    return pltpu.make_async_copy(src, dst, sem)


def _dispatch_kernel(h_ref, dest_ref, xin_ref, xout_ref, sem, *, tm):
    del xin_ref

    def issue(r, c):
        for k in range(TOP_K):
            _tile_copy(h_ref, r, xout_ref, dest_ref[k, r], sem).start()
        return c

    lax.fori_loop(0, tm, issue, 0)

    def drain(r, c):
        for k in range(TOP_K):
            _tile_copy(h_ref, 0, xout_ref, 0, sem).wait()
        return c

    lax.fori_loop(0, tm, drain, 0)


def _dispatch(h_tiles, dest, n_rows):
    t = h_tiles.shape[0] // TILE_ROWS
    tm = MOE_TOK_TILE
    zeros = jnp.zeros((n_rows * TILE_ROWS, LANES), h_tiles.dtype)
    return pl.pallas_call(
        functools.partial(_dispatch_kernel, tm=tm),
        grid=(t // tm,),
        in_specs=[
            pl.BlockSpec((tm * TILE_ROWS, LANES), lambda i: (i, 0)),
            pl.BlockSpec((TOP_K, tm), lambda i: (0, i), memory_space=pltpu.SMEM),
            pl.BlockSpec(memory_space=pl.ANY),
        ],
        out_specs=pl.BlockSpec(memory_space=pl.ANY),
        out_shape=jax.ShapeDtypeStruct((n_rows * TILE_ROWS, LANES), h_tiles.dtype),
        scratch_shapes=[pltpu.SemaphoreType.DMA],
        input_output_aliases={2: 0},
        compiler_params=_cparams(("arbitrary",)),
        name="moe_dispatch",
    )(h_tiles, dest, zeros)


def _expert_kernel(be_ref, nu_ref, x_ref, wgu_ref, bgu_ref, wd_ref, bd_ref, o_ref, wgu_s, wd_s, *, dff, rb):
    b = pl.program_id(0)
    e = be_ref[b]
    prev = be_ref[jnp.maximum(b - 1, 0)]

    @pl.when(jnp.logical_or(b == 0, e != prev))
    def _():
        wgu_s[...] = wgu_ref[0].astype(BF16)
        wd_s[...] = wd_ref[0].astype(BF16)

    @pl.when(b < nu_ref[0])
    def _():
        x = _load_token_tiles(x_ref, rb)
        gu = jnp.dot(x.astype(BF16), wgu_s[...], preferred_element_type=F32) + bgu_ref[0]
        gate = jnp.minimum(gu[:, :dff], SWIGLU_LIMIT)
        up = jnp.clip(gu[:, dff:], -SWIGLU_LIMIT, SWIGLU_LIMIT)
        act = (up + 1.0) * (gate * jax.nn.sigmoid(SWIGLU_ALPHA * gate))
        _store_token_tiles(o_ref, jnp.dot(act.astype(BF16), wd_s[...], preferred_element_type=F32) + bd_ref[0])

    @pl.when(b >= nu_ref[0])
    def _():
        o_ref[...] = jnp.zeros_like(o_ref)


def _expert_ffn(x_disp, blk_expert, n_used, layer, w_gu, b_gu, w_down, b_down, rb):
    nl, ne, d, dff2 = w_gu.shape
    dff = dff2 // 2
    nb = x_disp.shape[0] // (rb * TILE_ROWS)
    pick = lambda b, be, nu: (layer, be[b], 0, 0)
    return pl.pallas_call(
        functools.partial(_expert_kernel, dff=dff, rb=rb),
        grid_spec=pltpu.PrefetchScalarGridSpec(
            num_scalar_prefetch=2,
            grid=(nb,),
            in_specs=[
                pl.BlockSpec((rb * TILE_ROWS, LANES), lambda b, be, nu: (b, 0)),
                pl.BlockSpec((None, 1, d, dff2), pick),
                pl.BlockSpec((None, 1, 1, dff2), pick),
                pl.BlockSpec((None, 1, dff, d), pick),
                pl.BlockSpec((None, 1, 1, d), pick),
            ],
            out_specs=pl.BlockSpec((rb * TILE_ROWS, LANES), lambda b, be, nu: (b, 0)),
            scratch_shapes=[pltpu.VMEM((d, dff2), BF16), pltpu.VMEM((dff, d), BF16)],
        ),
        out_shape=jax.ShapeDtypeStruct(x_disp.shape, F32),
        compiler_params=_cparams(("arbitrary",)),
        name="moe_experts",
    )(blk_expert, n_used, x_disp, w_gu, b_gu.reshape(nl, ne, 1, dff2), w_down, b_down.reshape(nl, ne, 1, d))


def _combine_kernel(dest_ref, gates_ref, x_ref, mod_ref, g_ref, b_ref, y_hbm, o_ref, buf, sem, *, tm, alpha):
    def issue(r, c):
        for k in range(TOP_K):
            _tile_copy(y_hbm, dest_ref[k, r], buf.at[k], r, sem).start()
        return c

    lax.fori_loop(0, tm, issue, 0)

    def drain(r, c):
        for k in range(TOP_K):
            _tile_copy(y_hbm, 0, buf.at[k], 0, sem).wait()
        return c

    lax.fori_loop(0, tm, drain, 0)
    y = gates_ref[:, 0:1] * _load_token_tiles(buf.at[0], tm)
    for k in range(1, TOP_K):
        y = y + gates_ref[:, k:k + 1] * _load_token_tiles(buf.at[k], tm)
    r = alpha * x_ref[...] + (1.0 + mod_ref[2:3, :]) * y
    o_ref[...] = _layer_norm(r, g_ref[...], b_ref[...])


def _combine_norm(y_disp, dest, gates_col, x, mods, midx, ln_g, ln_b, seq, alpha):
    t, d = x.shape
    tm = MOE_TOK_TILE
    row = lambda i: (i, 0)
    return pl.pallas_call(
        functools.partial(_combine_kernel, tm=tm, alpha=alpha),
        grid=(t // tm,),
        in_specs=[
            pl.BlockSpec((TOP_K, tm), lambda i: (0, i), memory_space=pltpu.SMEM),
            pl.BlockSpec((tm, TOP_K), row),
            pl.BlockSpec((tm, d), row),
            _mod_spec(midx, seq // tm, d),
            pl.BlockSpec((1, d), lambda i: (0, 0)),
            pl.BlockSpec((1, d), lambda i: (0, 0)),
            pl.BlockSpec(memory_space=pl.ANY),
        ],
        out_specs=pl.BlockSpec((tm, d), row),
        out_shape=jax.ShapeDtypeStruct((t, d), F32),
        scratch_shapes=[pltpu.VMEM((TOP_K, tm * TILE_ROWS, LANES), F32), pltpu.SemaphoreType.DMA],
        compiler_params=_cparams(("arbitrary",)),
        name="moe_combine",
    )(dest, gates_col, x, mods, ln_g.reshape(1, d), ln_b.reshape(1, d), y_disp)


def _moe_block(x, mods, midx, router_w, router_b, layer, w_gu, b_gu, w_down, b_down, ln_g, ln_b, seq, alpha):
    t, d = x.shape
    assert d == TILE_ROWS * LANES
    rb = MOE_ROW_BLOCK
    nb = (t * TOP_K + N_EXPERTS * (rb - 1) + rb - 1) // rb
    h, eidx, rank, gates, counts = _route(x, mods, midx, router_w, router_b, seq)
    dest, blk_expert, n_used = _plan(eidx, rank, counts, rb, nb)
    x_disp = _dispatch(h, dest, nb * rb)
    y_disp = _expert_ffn(x_disp, blk_expert, n_used, layer, w_gu, b_gu, w_down, b_down, rb)
    return _combine_norm(y_disp, dest, gates.T, x, mods, midx, ln_g, ln_b, seq, alpha)


def _inproj_odd_kernel(x_ref, mod_ref, w_ref, wvt_ref, wf_ref, fb_ref, zs_ref, q_ref, k_ref, vt_ref, f_ref,
                       *, width):
    h = (x_ref[...] * (1.0 + mod_ref[1:2, :]) + mod_ref[0:1, :]).astype(BF16)
    z = jnp.dot(h, w_ref[...], preferred_element_type=F32)
    zs_ref[...] = z[:, :width]
    q_ref[...] = (z[:, width:2 * width] * (HEAD_DIM ** -0.5 * LOG2E)).astype(BF16)
    k_ref[...] = z[:, 2 * width:3 * width].astype(BF16)
    vt_ref[...] = lax.dot_general(wvt_ref[...], h, (((1,), (1,)), ((), ())),
                                  preferred_element_type=F32).astype(BF16)
    f_ref[...] = jnp.dot(h, wf_ref[...], preferred_element_type=F32) + fb_ref[...]


def _inproj_odd(x, mods, midx, w_main_bf, wvt_bf, wf_bf, fb, seq):
    t, d = x.shape
    width = wvt_bf.shape[0]
    tm = ROW_TILE
    row = lambda i: (i, 0)
    const = lambda i: (0, 0)
    return pl.pallas_call(
        functools.partial(_inproj_odd_kernel, width=width),
        grid=(t // tm,),
        in_specs=[
            pl.BlockSpec((tm, d), row),
            _mod_spec(midx, seq // tm, d),
            pl.BlockSpec((d, 3 * width), const),
            pl.BlockSpec((width, d), const),
            pl.BlockSpec((d, LANES), const),
            pl.BlockSpec((1, LANES), const),
        ],
        out_specs=[pl.BlockSpec((tm, width), row)] * 3 + [pl.BlockSpec((width, tm), lambda i: (0, i)),
                                                          pl.BlockSpec((tm, LANES), row)],
        out_shape=[jax.ShapeDtypeStruct((t, width), F32)] + [jax.ShapeDtypeStruct((t, width), BF16)] * 2
        + [jax.ShapeDtypeStruct((width, t), BF16), jax.ShapeDtypeStruct((t, LANES), F32)],
        compiler_params=_cparams(("parallel",)),
        name="inproj_odd",
    )(x, mods, w_main_bf, wvt_bf, wf_bf, fb)


def _fkeys_kernel(f_ref, k_ref, sel_ref, o_ref, carry_ref, *, ts, n_pairs):
    @pl.when(pl.program_id(1) == 0)
    def _():
        carry_ref[...] = jnp.zeros_like(carry_ref)

    x = f_ref[...]
    ls = jnp.minimum(x, 0.0) - jnp.log(1.0 + jnp.exp(-jnp.abs(x)))
    upto = (lax.broadcasted_iota(I32, (ts, ts), 1) <= lax.broadcasted_iota(I32, (ts, ts), 0)).astype(F32)
    cs = jnp.dot(upto, ls, precision=HIGHEST, preferred_element_type=F32) + carry_ref[...]
    carry_ref[...] = cs[ts - 1:ts, :]
    rest = -LOG2E * cs
    pieces = []
    for _ in range(FOX_BIAS_PARTS):
        piece = rest.astype(BF16)
        pieces.append(piece)
        rest = rest - piece.astype(F32)
    ext = jnp.dot(jnp.concatenate(pieces, axis=1), sel_ref[...], preferred_element_type=F32).astype(BF16)
    for p in range(n_pairs):
        o_ref[:, 2 * LANES * p:2 * LANES * p + LANES] = k_ref[:, LANES * p:LANES * (p + 1)]
        o_ref[:, 2 * LANES * p + LANES:2 * LANES * (p + 1)] = ext[:, LANES * p:LANES * (p + 1)]


def _forget_keys(flog, k, bsz, seq):
    t, width = k.shape
    n_pairs = width // LANES
    ts = FCUM_TILE
    nt = seq // ts
    sel = [[0.0] * width for _ in range(FOX_BIAS_PARTS * LANES)]
    for h in range(2 * n_pairs):
        for part in range(FOX_BIAS_PARTS):
            sel[part * LANES + h][(h // 2) * LANES + FOX_BIAS_PARTS * (h % 2) + part] = 1.0
    sel = jnp.asarray(sel, BF16)
    tile = lambda b, j: (b * nt + j, 0)
    return pl.pallas_call(
        functools.partial(_fkeys_kernel, ts=ts, n_pairs=n_pairs),
        grid=(bsz, nt),
        in_specs=[pl.BlockSpec((ts, LANES), tile), pl.BlockSpec((ts, width), tile),
                  pl.BlockSpec(sel.shape, lambda b, j: (0, 0))],
        out_specs=pl.BlockSpec((ts, 2 * width), tile),
        out_shape=jax.ShapeDtypeStruct((t, 2 * width), BF16),
        scratch_shapes=[pltpu.VMEM((1, LANES), F32)],
        compiler_params=_cparams(("parallel", "arbitrary")),
        name="forget_keys",
    )(flog, k, sel)


def _fox_kernel(q_ref, ka_ref, vt_ref, o_ref, acc_ref, m_ref, l_ref, qa_ref, st0, st1, p0, p1, al0, al1,
                *, tq, tk):
    st_refs, p_refs, al_refs = (st0, st1), (p0, p1), (al0, al1)
    n_full = pl.program_id(2)
    lane = lax.broadcasted_iota(I32, (tq, LANES), 1)
    qp = q_ref[...]
    for hh in range(2):
        mine = (lane < HEAD_DIM) if hh == 0 else (lane >= HEAD_DIM)
        ones = (lane >= FOX_BIAS_PARTS * hh) & (lane < FOX_BIAS_PARTS * (hh + 1))
        qa_ref[hh] = jnp.concatenate([jnp.where(mine, qp, jnp.zeros_like(qp)),
                                      jnp.where(ones, 1.0, 0.0).astype(BF16)], axis=1)
    acc_ref[...] = jnp.zeros_like(acc_ref)
    m_ref[...] = jnp.full_like(m_ref, NEG_INF)
    l_ref[...] = jnp.zeros_like(l_ref)

    def scores(kb, hh):
        off = pl.multiple_of(kb * tk, tk)
        return lax.dot_general(ka_ref[pl.ds(off, tk), :], qa_ref[hh], (((1,), (1,)), ((), ())),
                               preferred_element_type=F32)

    def softmax_step(st, hh):
        m_old = m_ref[hh:hh + 1, :]
        m_new = jnp.maximum(m_old, jnp.max(st, axis=0, keepdims=True))
        alpha = jnp.exp2(m_old - m_new)
        p = jnp.exp2(st - m_new)
        l_ref[hh:hh + 1, :] = alpha * l_ref[hh:hh + 1, :] + jnp.sum(p, axis=0, keepdims=True)
        m_ref[hh:hh + 1, :] = m_new
        return alpha, p.astype(BF16)

    def accumulate(kb, hh, alpha, p):
        off = pl.multiple_of(kb * tk, tk)
        rows = slice(HEAD_DIM * hh, HEAD_DIM * (hh + 1))
        acc_ref[rows, :] = alpha * acc_ref[rows, :] + jnp.dot(vt_ref[rows, pl.ds(off, tk)], p,
                                                               preferred_element_type=F32)

    keep = lax.broadcasted_iota(I32, (tk, tq), 0) <= lax.broadcasted_iota(I32, (tk, tq), 1)
    for hh in range(2):
        alpha, p = softmax_step(jnp.where(keep, scores(n_full, hh), NEG_INF), hh)
        accumulate(n_full, hh, alpha, p)

    def stage_a(kb, slot):
        for hh in range(2):
            st_refs[slot][hh] = scores(kb, hh)

    def stage_b(slot):
        for hh in range(2):
            alpha, p = softmax_step(st_refs[slot][hh], hh)
            al_refs[slot][hh:hh + 1, :] = alpha
            p_refs[slot][hh] = p

    def stage_c(kb, slot):
        for hh in range(2):
            accumulate(kb, hh, al_refs[slot][hh:hh + 1, :], p_refs[slot][hh])

    def step(i, slot):
        stage_a(i, slot)
        stage_b(1 - slot)
        stage_c(i - 2, slot)

    odd = lax.rem(n_full, 2) == 1

    @pl.when(n_full >= 1)
    def _():
        stage_a(0, 0)

    @pl.when(n_full >= 2)
    def _():
        stage_a(1, 1)
        stage_b(0)

    @pl.when(n_full == 1)
    def _():
        stage_b(0)
        stage_c(0, 0)

    def body(j, c):
        i = 2 + 2 * j
        step(i, 0)
        step(i + 1, 1)
        return c

    lax.fori_loop(0, (n_full - 2) // 2, body, 0)

    @pl.when(jnp.logical_and(n_full >= 2, jnp.logical_not(odd)))
    def _():
        stage_b(1)
        stage_c(n_full - 2, 0)
        stage_c(n_full - 1, 1)

    @pl.when(jnp.logical_and(n_full >= 3, odd))
    def _():
        step(n_full - 1, 0)
        stage_b(0)
        stage_c(n_full - 2, 1)
        stage_c(n_full - 1, 0)

    head0 = lax.broadcasted_iota(I32, (LANES, tq), 0) < HEAD_DIM
    out_t = acc_ref[...] / jnp.where(head0, l_ref[0:1, :], l_ref[1:2, :])
    o_ref[...] = out_t.T.astype(o_ref.dtype)


def _forgetting_attention(q, kaug, vt, bsz, seq):
    t, width = q.shape
    n_pairs = width // LANES
    tq, tk = FOX_TQ, FOX_TK
    nq = seq // tq
    return pl.pallas_call(
        functools.partial(_fox_kernel, tq=tq, tk=tk),
        grid=(bsz, n_pairs, nq),
        in_specs=[
            pl.BlockSpec((tq, LANES), lambda b, h, i: (b * nq + i, h)),
            pl.BlockSpec((seq, 2 * LANES), lambda b, h, i: (b, h)),
            pl.BlockSpec((LANES, seq), lambda b, h, i: (h, b)),
        ],
        out_specs=pl.BlockSpec((tq, LANES), lambda b, h, i: (b * nq + i, h)),
        out_shape=jax.ShapeDtypeStruct((t, width), BF16),
        scratch_shapes=[pltpu.VMEM((LANES, tq), F32), pltpu.VMEM((8, tq), F32), pltpu.VMEM((8, tq), F32),
                        pltpu.VMEM((2, tq, 2 * LANES), BF16)]
        + [pltpu.VMEM((2, tk, tq), F32)] * 2 + [pltpu.VMEM((2, tk, tq), BF16)] * 2 + [pltpu.VMEM((8, tq), F32)] * 2,
        compiler_params=_cparams(("parallel", "parallel", "arbitrary")),
        name="forgetting_attention",
    )(q, kaug, vt)


def _s5_prep_kernel(lre_ref, lim_ref, ldt_ref, bre_ref, bim_ref, pw_re, pw_im, tb_re, tb_im, bb_re, bb_im,
                    *, ts, nsteps):
    dt = jnp.exp(ldt_ref[...])
    lre = jnp.minimum(lre_ref[...], -1e-4)
    lim = lim_ref[...]
    mag = jnp.exp(lre * dt)
    a_re = mag * jnp.cos(lim * dt)
    a_im = mag * jnp.sin(lim * dt)
    den = lre * lre + lim * lim
    nre, nim = a_re - 1.0, a_im
    g_re = (nre * lre + nim * lim) / den
    g_im = (nim * lre - nre * lim) / den
    n = lre.shape[-1]
    per = n // bre_ref.shape[0]
    for j in range(bre_ref.shape[0]):
        gr = g_re[:, per * j:per * (j + 1)]
        gi = g_im[:, per * j:per * (j + 1)]
        bb_re[j] = (gr * bre_ref[j] - gi * bim_ref[j]).astype(bb_re.dtype)
        bb_im[j] = (gr * bim_ref[j] + gi * bre_ref[j]).astype(bb_im.dtype)
    tb_re[0:1, :] = a_re
    tb_im[0:1, :] = a_im

    def step(j, c):
        pr = tb_re[pl.ds(j - 1, 1), :]
        pi = tb_im[pl.ds(j - 1, 1), :]
        tb_re[pl.ds(j, 1), :] = pr * a_re - pi * a_im
        tb_im[pl.ds(j, 1), :] = pr * a_im + pi * a_re
        return c

    lax.fori_loop(1, ts, step, 0)
    for m in range(nsteps):
        pw_re[m:m + 1, :] = tb_re[2 ** m - 1:2 ** m, :]
        pw_im[m:m + 1, :] = tb_im[2 ** m - 1:2 ** m, :]
    for m in range(nsteps, pw_re.shape[0]):
        pw_re[m:m + 1, :] = jnp.zeros((1, n), F32)
        pw_im[m:m + 1, :] = jnp.zeros((1, n), F32)


def _block_diag_slabs(blocks, slabs):
    g, r, c = blocks.shape
    per = g // slabs
    eye = jnp.eye(per, dtype=bool)[None, :, None, :, None]
    bd = jnp.where(eye, blocks.reshape(slabs, per, r, 1, c), jnp.zeros((), blocks.dtype))
    return bd.reshape(slabs, per * r, per * c)


def _s5_kernel(u_ref, pw_re, pw_im, tb_re, tb_im, bb_re, bb_im, cc_re, cc_im, dsk_ref, gw_ref, gb_ref, o_ref,
               ar, ai, br, bi, cr, ci, *, ts, head, nsteps, slabs):
    @pl.when(pl.program_id(1) == 0)
    def _():
        cr[...] = jnp.zeros_like(cr)
        ci[...] = jnp.zeros_like(ci)

    n = ar.shape[1]
    per = n // slabs
    u = u_ref[...]
    ub = u.astype(BF16)
    zeros_head = jnp.zeros((head, n), F32)
    for ref in (ar, ai, br, bi):
        ref[0:head, :] = zeros_head
    for j in range(slabs):
        uj = ub[:, LANES * j:LANES * (j + 1)]
        ar[head:head + ts, per * j:per * (j + 1)] = jnp.dot(uj, bb_re[j], preferred_element_type=F32)
        ai[head:head + ts, per * j:per * (j + 1)] = jnp.dot(uj, bb_im[j], preferred_element_type=F32)
    src, dst = (ar, ai), (br, bi)
    for m in range(nsteps):
        d = 2 ** m
        for j in range(slabs):
            cols = slice(per * j, per * (j + 1))
            mr = pw_re[m:m + 1, cols]
            mi = pw_im[m:m + 1, cols]
            xr = src[0][head:head + ts, cols]
            xi = src[1][head:head + ts, cols]
            sr = src[0][head - d:head - d + ts, cols]
            si = src[1][head - d:head - d + ts, cols]
            dst[0][head:head + ts, cols] = xr + mr * sr - mi * si
            dst[1][head:head + ts, cols] = xi + mr * si + mi * sr
        src, dst = dst, src
    for j in range(slabs):
        cols = slice(per * j, per * (j + 1))
        c_r = cr[:, cols]
        c_i = ci[:, cols]
        p_r = tb_re[:, cols]
        p_i = tb_im[:, cols]
        xr = src[0][head:head + ts, cols] + p_r * c_r - p_i * c_i
        xi = src[1][head:head + ts, cols] + p_r * c_i + p_i * c_r
        cr[:, cols] = xr[ts - 1:ts, :]
        ci[:, cols] = xi[ts - 1:ts, :]
        yj = jnp.dot(xr.astype(BF16), cc_re[j], preferred_element_type=F32)
        yj = yj - jnp.dot(xi.astype(BF16), cc_im[j], preferred_element_type=F32)
        dst[0][head:head + ts, LANES * j:LANES * (j + 1)] = yj
    width = slabs * LANES
    y = dst[0][head:head + ts, 0:width] + dsk_ref[...] * u
    y = 0.5 * y * (1.0 + jnp.tanh(math.sqrt(2.0 / math.pi) * (y + 0.044715 * (y * y * y))))
    gate = jnp.dot(y.astype(BF16), gw_ref[...], preferred_element_type=F32) + gb_ref[...]
    o_ref[...] = (y * jax.nn.sigmoid(gate)).astype(o_ref.dtype)


def _s5_mixer(u, lam_re, lam_im, log_dt, b_re, b_im, c_re, c_im, d_skip, glu_w, glu_b, bsz, seq):
    t, width = u.shape
    groups, states = lam_re.shape
    n = groups * states
    slabs = width // LANES
    ts = S5_TILE
    nsteps = int(math.log2(ts))
    head = ts // 2
    flat = lambda a: a.reshape(1, n)
    ldt = jnp.repeat(log_dt, states).reshape(1, n)
    bre_bd = _block_diag_slabs(jnp.swapaxes(b_re, 1, 2), slabs)
    bim_bd = _block_diag_slabs(jnp.swapaxes(b_im, 1, 2), slabs)
    cre_bd = _block_diag_slabs(jnp.swapaxes(c_re, 1, 2), slabs).astype(BF16)
    cim_bd = _block_diag_slabs(jnp.swapaxes(c_im, 1, 2), slabs).astype(BF16)
    full = lambda a: pl.BlockSpec(a.shape, lambda *_: (0,) * a.ndim)
    prep_in = (flat(lam_re), flat(lam_im), ldt, bre_bd, bim_bd)
    pw_re, pw_im, tb_re, tb_im, bb_re, bb_im = pl.pallas_call(
        functools.partial(_s5_prep_kernel, ts=ts, nsteps=nsteps),
        grid=(1,),
        in_specs=[full(a) for a in prep_in],
        out_specs=[pl.BlockSpec((8, n), lambda i: (0, 0))] * 2 + [pl.BlockSpec((ts, n), lambda i: (0, 0))] * 2
        + [pl.BlockSpec(bre_bd.shape, lambda i: (0, 0, 0))] * 2,
        out_shape=[jax.ShapeDtypeStruct((8, n), F32)] * 2 + [jax.ShapeDtypeStruct((ts, n), F32)] * 2
        + [jax.ShapeDtypeStruct(bre_bd.shape, BF16)] * 2,
        compiler_params=_cparams(("arbitrary",)),
        name="s5_prep",
    )(*prep_in)
    nt = seq // ts
    consts = (pw_re, pw_im, tb_re, tb_im, bb_re, bb_im, cre_bd, cim_bd, d_skip.reshape(1, width),
              glu_w.astype(BF16), glu_b.reshape(1, width))
    return pl.pallas_call(
        functools.partial(_s5_kernel, ts=ts, head=head, nsteps=nsteps, slabs=slabs),
        grid=(bsz, nt),
        in_specs=[pl.BlockSpec((ts, width), lambda b, j: (b * nt + j, 0))] + [full(a) for a in consts],
        out_specs=pl.BlockSpec((ts, width), lambda b, j: (b * nt + j, 0)),
        out_shape=jax.ShapeDtypeStruct((t, width), BF16),
        scratch_shapes=[pltpu.VMEM((head + ts, n), F32)] * 4 + [pltpu.VMEM((1, n), F32)] * 2,
        compiler_params=_cparams(("parallel", "arbitrary")),
        name="s5_mixer",
    )(u, *consts)


def kernel(x, c, mod_w, mod_b, ln_g, ln_b, even_w_in, pool_w, pool_scale, rel_bias, even_w_out, odd_w_in, forget_b, ssm_lam_re, ssm_lam_im, ssm_log_dt, ssm_b_re, ssm_b_im, ssm_c_re, ssm_c_im, ssm_d, ssm_glu_w, ssm_glu_b, odd_w_out, router_w, router_b, exp_w_gu, exp_b_gu, exp_w_down, exp_b_down):
    bsz, seq, d = x.shape
    depth = mod_w.shape[0]
    alpha = (2.0 * depth) ** 0.25
    mods = _ada_mods(c, mod_w.reshape(depth * 2, d, 3 * d), mod_b.reshape(depth * 2, 3 * d))
    xt = x.reshape(bsz * seq, d)
    for layer in range(depth):
        i = layer // 2
        if layer % 2 == 0:
            zp, q, k, v = _inproj_even(xt, mods, 2 * layer, even_w_in[i].astype(BF16), seq)
            ya = _pool_mixer(zp, pool_w[i].astype(BF16), pool_scale[i], seq)
            yb = _chunk_attention(q, k, v, _cattn_bias_table(rel_bias[i]), bsz, seq)
            w_out = even_w_out[i]
        else:
            width = ssm_d.shape[1] * ssm_d.shape[2]
            w_in = odd_w_in[i]
            n_heads = forget_b.shape[1]
            wf = jnp.zeros((d, LANES), BF16).at[:, :n_heads].set(w_in[:, 4 * width:].astype(BF16))
            fb = jnp.zeros((1, LANES), F32).at[0, :n_heads].set(forget_b[i])
            zs, q, k, vt, flog = _inproj_odd(xt, mods, 2 * layer, w_in[:, :3 * width].astype(BF16),
                                             w_in[:, 3 * width:4 * width].T.astype(BF16), wf, fb, seq)
            ya = _s5_mixer(zs, ssm_lam_re[i], ssm_lam_im[i], ssm_log_dt[i], ssm_b_re[i], ssm_b_im[i],
                           ssm_c_re[i], ssm_c_im[i], ssm_d[i], ssm_glu_w[i], ssm_glu_b[i], bsz, seq)
            yb = _forgetting_attention(q, _forget_keys(flog, k, bsz, seq), vt, bsz, seq)
            w_out = odd_w_out[i]
        xt = _outproj_norm(ya, yb, xt, mods, 2 * layer, w_out.astype(BF16), ln_g[layer, 0], ln_b[layer, 0],
                           seq, alpha)
        xt = _moe_block(xt, mods, 2 * layer + 1, router_w[layer], router_b[layer], layer, exp_w_gu, exp_b_gu,
                        exp_w_down, exp_b_down, ln_g[layer, 1], ln_b[layer, 1], seq, alpha)
    return xt.reshape(bsz, seq, d)
```

```python
import functools
import math

import jax
import jax.numpy as jnp
from jax import lax
from jax.experimental import pallas as pl
from jax.experimental.pallas import tpu as pltpu

F32 = jnp.float32
BF16 = jnp.bfloat16
I32 = jnp.int32
HIGHEST = lax.Precision.HIGHEST

LANES = 128
SUBLANES = 8
VMEM_LIMIT_BYTES = 60000 * 1024

HEAD_DIM = 64
CHUNK = 64
LEFT_CHUNKS = 8
REL_CLIP = 128
POOL_WINDOWS = (2, 4, 8, 16)
POOL_HALO = 16
SSM_GROUP_DIM = 16
SSM_STATE = 64
N_EXPERTS = 32
TOP_K = 4
SWIGLU_LIMIT = 7.0
SWIGLU_ALPHA = 1.702
LN_EPS = 1e-5
NEG_INF = -1e30
LOG2E = math.log2(math.e)
FOX_BIAS_PARTS = 3

ROW_TILE = 512
CATTN_Q = 2 * CHUNK
CATTN_BAND = (LEFT_CHUNKS + 2) * CHUNK
FOX_TQ = 512
FOX_TK = 512
S5_TILE = 256
FCUM_TILE = 512
TILE_ROWS = SUBLANES
MOE_ROW_BLOCK = 256
MOE_TOK_TILE = 256
ROUTE_TILE = 512


def _cparams(sem):
    return pltpu.CompilerParams(dimension_semantics=sem, vmem_limit_bytes=VMEM_LIMIT_BYTES)


def _layer_norm(r, g, b):
    mu = jnp.mean(r, axis=-1, keepdims=True)
    d = r - mu
    var = jnp.mean(d * d, axis=-1, keepdims=True)
    return d * lax.rsqrt(var + LN_EPS) * g + b


def _mod_kernel(c_ref, w_ref, b_ref, o_ref):
    c = c_ref[...]
    s = c * jax.nn.sigmoid(c)
    o_ref[0] = jnp.dot(s, w_ref[0], precision=HIGHEST, preferred_element_type=F32) + b_ref[0]


def _ada_mods(c, mod_w, mod_b):
    bsz, d = c.shape
    m = mod_w.shape[0]
    rows = 8
    c_pad = jnp.zeros((rows, d), F32).at[:bsz].set(c)
    out = pl.pallas_call(
        _mod_kernel,
        grid=(m, 3),
        in_specs=[
            pl.BlockSpec((rows, d), lambda i, j: (0, 0)),
            pl.BlockSpec((1, d, d), lambda i, j: (i, 0, j)),
            pl.BlockSpec((1, 1, d), lambda i, j: (i, 0, j)),
        ],
        out_specs=pl.BlockSpec((1, rows, d), lambda i, j: (i, 0, j)),
        out_shape=jax.ShapeDtypeStruct((m, rows, 3 * d), F32),
        compiler_params=_cparams(("parallel", "parallel")),
        name="ada_mod",
    )(c_pad, mod_w, mod_b.reshape(m, 1, 3 * d))
    return out[:, :bsz].reshape(m, bsz, 3, d)


def _mod_spec(midx, tiles_per_batch, d):
    return pl.BlockSpec((None, None, 3, d), lambda i: (midx, i // tiles_per_batch, 0, 0))


def _inproj_even_kernel(x_ref, mod_ref, w_ref, wvt_ref, zp_ref, q_ref, k_ref, vt_ref, *, width):
    h = (x_ref[...] * (1.0 + mod_ref[1:2, :]) + mod_ref[0:1, :]).astype(BF16)
    z = jnp.dot(h, w_ref[...], preferred_element_type=F32)
    zp_ref[...] = z[:, :width]
    q_ref[...] = (z[:, width:2 * width] * (HEAD_DIM ** -0.5 * LOG2E)).astype(BF16)
    k_ref[...] = z[:, 2 * width:3 * width].astype(BF16)
    vt_ref[...] = lax.dot_general(wvt_ref[...], h, (((1,), (1,)), ((), ())),
                                  preferred_element_type=F32).astype(BF16)


def _inproj_even(x, mods, midx, w_main_bf, wvt_bf, seq):
    t, d = x.shape
    width = wvt_bf.shape[0]
    tm = ROW_TILE
    row = lambda i: (i, 0)
    return pl.pallas_call(
        functools.partial(_inproj_even_kernel, width=width),
        grid=(t // tm,),
        in_specs=[
            pl.BlockSpec((tm, d), row),
            _mod_spec(midx, seq // tm, d),
            pl.BlockSpec((d, 3 * width), lambda i: (0, 0)),
            pl.BlockSpec((width, d), lambda i: (0, 0)),
        ],
        out_specs=[pl.BlockSpec((tm, width), row)] * 3
        + [pl.BlockSpec((None, width, tm), lambda i: (i // (seq // tm), 0, i % (seq // tm)))],
        out_shape=[jax.ShapeDtypeStruct((t, width), F32)] + [jax.ShapeDtypeStruct((t, width), BF16)] * 2
        + [jax.ShapeDtypeStruct((t // seq, width, seq), BF16)],
        compiler_params=_cparams(("parallel",)),
        name="inproj_even",
    )(x, mods, w_main_bf, wvt_bf)


def _pool_kernel(zp_ref, halo_ref, w_ref, sc_ref, o_ref, xs_ref, *, ts, seq):
    i = pl.program_id(0)
    t0 = lax.rem(i * ts, seq)
    xs_ref[0:POOL_HALO, :] = jnp.where(t0 == 0, 0.0, halo_ref[...])
    xs_ref[POOL_HALO:POOL_HALO + ts, :] = zp_ref[...]
    pos = lax.broadcasted_iota(I32, (ts, LANES), 0) + t0
    for g, win in enumerate(POOL_WINDOWS):
        cols = slice(LANES * g, LANES * (g + 1))
        u = xs_ref[POOL_HALO:POOL_HALO + ts, cols]
        acc = u
        for k in range(1, win):
            acc = acc + xs_ref[POOL_HALO - k:POOL_HALO - k + ts, cols]
        cnt = jnp.minimum(pos + 1, win).astype(F32)
        dlt = acc / cnt - u
        y = jnp.dot(dlt.astype(BF16), w_ref[g], preferred_element_type=F32)
        o_ref[:, cols] = (y * sc_ref[:, cols]).astype(o_ref.dtype)


def _pool_mixer(zp, w_pool_bf, pool_scale, seq):
    t, width = zp.shape
    ts = ROW_TILE
    hb = ts // POOL_HALO
    return pl.pallas_call(
        functools.partial(_pool_kernel, ts=ts, seq=seq),
        grid=(t // ts,),
        in_specs=[
            pl.BlockSpec((ts, width), lambda i: (i, 0)),
            pl.BlockSpec((POOL_HALO, width), lambda i: (jnp.maximum(i * hb - 1, 0), 0)),
            pl.BlockSpec(w_pool_bf.shape, lambda i: (0, 0, 0)),
            pl.BlockSpec((1, width), lambda i: (0, 0)),
        ],
        out_specs=pl.BlockSpec((ts, width), lambda i: (i, 0)),
        out_shape=jax.ShapeDtypeStruct((t, width), BF16),
        scratch_shapes=[pltpu.VMEM((ts + POOL_HALO, width), F32)],
        compiler_params=_cparams(("parallel",)),
        name="pool_mixer",
    )(zp, zp, w_pool_bf, pool_scale.reshape(1, width))


def _cbias_kernel(r_ref, o_ref):
    base = jnp.broadcast_to(r_ref[...], (CATTN_Q, r_ref.shape[-1]))
    rolled = pltpu.roll(base, 0, 1, stride=1, stride_axis=0)[:, :CATTN_BAND]
    qi = lax.broadcasted_iota(I32, (CATTN_Q, CATTN_BAND), 0)
    kp = lax.broadcasted_iota(I32, (CATTN_Q, CATTN_BAND), 1)
    lo = (qi // CHUNK) * CHUNK
    valid = (kp >= lo) & (kp < lo + (LEFT_CHUNKS + 1) * CHUNK)
    o_ref[...] = jnp.where(valid, rolled * LOG2E, NEG_INF).T


def _cattn_bias_table(rel_bias):
    h = rel_bias.shape[0]
    wide = 1024
    far = LEFT_CHUNKS * CHUNK - REL_CLIP + 1
    tail = wide - far - (2 * REL_CLIP - 1)
    last = rel_bias[:, 2 * REL_CLIP:]
    base = jnp.concatenate(
        [jnp.broadcast_to(last, (h, far)), rel_bias[:, 1:2 * REL_CLIP][:, ::-1], jnp.broadcast_to(last, (h, tail))],
        axis=1).reshape(h, 1, wide)
    return pl.pallas_call(
        _cbias_kernel,
        grid=(h,),
        in_specs=[pl.BlockSpec((None, 1, wide), lambda i: (i, 0, 0))],
        out_specs=pl.BlockSpec((None, CATTN_BAND, CATTN_Q), lambda i: (i, 0, 0)),
        out_shape=jax.ShapeDtypeStruct((h, CATTN_BAND, CATTN_Q), F32),
        compiler_params=_cparams(("parallel",)),
        name="cattn_bias",
    )(base)


def _cattn_kernel(q_ref, k_ref, vt_ref, bias_ref, o_ref, st_ref, e_ref, l_ref, *, n_pairs):
    p = pl.program_id(1)
    start = pl.multiple_of(p * CATTN_Q, CATTN_Q)
    kpos = lax.broadcasted_iota(I32, (CATTN_BAND, CATTN_Q), 0) + start
    real_key = kpos >= LEFT_CHUNKS * CHUNK
    lane = lax.broadcasted_iota(I32, (CATTN_Q, LANES), 1)
    n_heads = 2 * n_pairs
    for h in range(n_heads):
        cols = slice(LANES * (h // 2), LANES * (h // 2 + 1))
        qp = q_ref[:, cols]
        mine = (lane < HEAD_DIM) if h % 2 == 0 else (lane >= HEAD_DIM)
        qm = jnp.where(mine, qp, jnp.zeros_like(qp))
        st_ref[h] = lax.dot_general(k_ref[pl.ds(start, CATTN_BAND), cols], qm, (((1,), (1,)), ((), ())),
                                    preferred_element_type=F32)
    for h in range(n_heads):
        st = jnp.where(real_key, st_ref[h] + bias_ref[h], NEG_INF)
        m = jnp.max(st, axis=0, keepdims=True)
        e = jnp.exp2(st - m)
        l_ref[h:h + 1, :] = jnp.sum(e, axis=0, keepdims=True)
        e_ref[h] = e.astype(BF16)
    for hp in range(n_pairs):
        outs = []
        for hh in range(2):
            h = 2 * hp + hh
            vth = vt_ref[HEAD_DIM * h:HEAD_DIM * (h + 1), pl.ds(start, CATTN_BAND)]
            outs.append(jnp.dot(vth, e_ref[h], preferred_element_type=F32) / l_ref[h:h + 1, :])
        o_ref[:, LANES * hp:LANES * (hp + 1)] = jnp.concatenate(outs, axis=0).T.astype(o_ref.dtype)


def _chunk_attention(q, k, vt, bias_t, bsz, seq):
    t, width = q.shape
    n_pairs = width // LANES
    pad = LEFT_CHUNKS * CHUNK
    kp = jnp.pad(k.reshape(bsz, seq, width), ((0, 0), (pad, 0), (0, 0)))
    vtp = jnp.pad(vt, ((0, 0), (0, 0), (pad, 0)))
    nq = seq // CATTN_Q
    return pl.pallas_call(
        functools.partial(_cattn_kernel, n_pairs=n_pairs),
        grid=(bsz, nq),
        in_specs=[
            pl.BlockSpec((CATTN_Q, width), lambda b, p: (b * nq + p, 0)),
            pl.BlockSpec((None, seq + pad, width), lambda b, p: (b, 0, 0)),
            pl.BlockSpec((None, width, seq + pad), lambda b, p: (b, 0, 0)),
            pl.BlockSpec(bias_t.shape, lambda b, p: (0, 0, 0)),
        ],
        out_specs=pl.BlockSpec((CATTN_Q, width), lambda b, p: (b * nq + p, 0)),
        out_shape=jax.ShapeDtypeStruct((t, width), BF16),
        scratch_shapes=[pltpu.VMEM((2 * n_pairs, CATTN_BAND, CATTN_Q), F32),
                        pltpu.VMEM((2 * n_pairs, CATTN_BAND, CATTN_Q), BF16),
                        pltpu.VMEM((2 * n_pairs, CATTN_Q), F32)],
        compiler_params=_cparams(("parallel", "arbitrary")),
        name="chunk_attention",
    )(q, kp, vtp, bias_t)


def _outproj_kernel(ya_ref, yb_ref, x_ref, mod_ref, w_ref, g_ref, b_ref, o_ref, *, alpha, half):
    y = jnp.dot(ya_ref[...], w_ref[0:half, :], preferred_element_type=F32)
    y = y + jnp.dot(yb_ref[...], w_ref[half:2 * half, :], preferred_element_type=F32)
    r = alpha * x_ref[...] + (1.0 + mod_ref[2:3, :]) * y
    o_ref[...] = _layer_norm(r, g_ref[...], b_ref[...])


def _outproj_norm(ya, yb, x, mods, midx, w_out_bf, ln_g, ln_b, seq, alpha):
    t, d = x.shape
    half = ya.shape[1]
    tm = ROW_TILE
    row = lambda i: (i, 0)
    return pl.pallas_call(
        functools.partial(_outproj_kernel, alpha=alpha, half=half),
        grid=(t // tm,),
        in_specs=[
            pl.BlockSpec((tm, half), row),
            pl.BlockSpec((tm, half), row),
            pl.BlockSpec((tm, d), row),
            _mod_spec(midx, seq // tm, d),
            pl.BlockSpec((2 * half, d), lambda i: (0, 0)),
            pl.BlockSpec((1, d), lambda i: (0, 0)),
            pl.BlockSpec((1, d), lambda i: (0, 0)),
        ],
        out_specs=pl.BlockSpec((tm, d), row),
        out_shape=jax.ShapeDtypeStruct((t, d), F32),
        compiler_params=_cparams(("parallel",)),
        name="outproj_norm",
    )(ya, yb, x, mods, w_out_bf, ln_g.reshape(1, d), ln_b.reshape(1, d))


def _route_kernel(x_ref, mod_ref, rw_ref, rb_ref, h_ref, e_ref, r_ref, g_ref, c_ref, carry_ref, *, tm):
    @pl.when(pl.program_id(0) == 0)
    def _():
        carry_ref[...] = jnp.zeros_like(carry_ref)

    h = x_ref[...] * (1.0 + mod_ref[1:2, :]) + mod_ref[0:1, :]
    _store_token_tiles(h_ref, h)
    logits = lax.dot_general(rw_ref[...], h, (((1,), (1,)), ((), ())), precision=HIGHEST,
                             preferred_element_type=F32) + rb_ref[...]
    eio = lax.broadcasted_iota(I32, (N_EXPERTS, tm), 0)
    vals, hots = [], []
    for k in range(TOP_K):
        m = jnp.max(logits, axis=0, keepdims=True)
        idx = jnp.min(jnp.where(logits == m, eio, N_EXPERTS), axis=0, keepdims=True)
        hot = eio == idx
        e_ref[k:k + 1, :] = idx
        vals.append(m)
        hots.append(hot)
        logits = jnp.where(hot, -jnp.inf, logits)
    exps = [jnp.exp(v - vals[0]) for v in vals]
    denom = exps[0] + exps[1] + exps[2] + exps[3]
    for k in range(TOP_K):
        g_ref[k:k + 1, :] = exps[k] / denom
    cnt = jnp.zeros((N_EXPERTS, tm), F32)
    for hot in hots:
        cnt = cnt + hot.astype(F32)
    before = (lax.broadcasted_iota(I32, (tm, tm), 0) < lax.broadcasted_iota(I32, (tm, tm), 1))
    prefix = jnp.dot(cnt.astype(BF16), before.astype(BF16), preferred_element_type=F32)
    tot = prefix + carry_ref[...]
    for k in range(TOP_K):
        r_ref[k:k + 1, :] = jnp.sum(jnp.where(hots[k], tot, 0.0), axis=0, keepdims=True).astype(I32)
    new_carry = carry_ref[...] + jnp.sum(cnt, axis=1, keepdims=True)
    carry_ref[...] = new_carry
    c_ref[...] = jnp.broadcast_to(new_carry, c_ref.shape)


def _route(x, mods, midx, router_w, router_b, seq):
    t, d = x.shape
    tm = ROUTE_TILE
    tok = lambda i: (0, i)
    return pl.pallas_call(
        functools.partial(_route_kernel, tm=tm),
        grid=(t // tm,),
        in_specs=[
            pl.BlockSpec((tm, d), lambda i: (i, 0)),
            _mod_spec(midx, seq // tm, d),
            pl.BlockSpec((N_EXPERTS, d), lambda i: (0, 0)),
            pl.BlockSpec((N_EXPERTS, 1), lambda i: (0, 0)),
        ],
        out_specs=[
            pl.BlockSpec((tm * TILE_ROWS, LANES), lambda i: (i, 0)),
            pl.BlockSpec((TOP_K, tm), tok),
            pl.BlockSpec((TOP_K, tm), tok),
            pl.BlockSpec((TOP_K, tm), tok),
            pl.BlockSpec((N_EXPERTS, LANES), lambda i: (0, 0)),
        ],
        out_shape=[
            jax.ShapeDtypeStruct((t * TILE_ROWS, LANES), F32),
            jax.ShapeDtypeStruct((TOP_K, t), I32),
            jax.ShapeDtypeStruct((TOP_K, t), I32),
            jax.ShapeDtypeStruct((TOP_K, t), F32),
            jax.ShapeDtypeStruct((N_EXPERTS, LANES), F32),
        ],
        scratch_shapes=[pltpu.VMEM((N_EXPERTS, 1), F32)],
        compiler_params=_cparams(("arbitrary",)),
        name="moe_route",
    )(x, mods, router_w.T, router_b.reshape(N_EXPERTS, 1))


def _plan_kernel(e_ref, r_ref, ccol_ref, crow_ref, d_ref, be_ref, nu_ref, *, tm, rb, nbp):
    inv = 1.0 / rb
    pad_col = jnp.floor((ccol_ref[...] + (rb - 1)) * inv) * rb
    pad_row = jnp.floor((crow_ref[0:1, :] + (rb - 1)) * inv) * rb
    ei = lax.broadcasted_iota(I32, (N_EXPERTS, LANES), 0)
    li = lax.broadcasted_iota(I32, (N_EXPERTS, LANES), 1)
    pstart = jnp.sum(jnp.where(li < ei, jnp.broadcast_to(pad_row, (N_EXPERTS, LANES)), 0.0),
                     axis=1, keepdims=True)
    pend = pstart + pad_col[:, 0:1]
    eio = lax.broadcasted_iota(I32, (N_EXPERTS, tm), 0)
    for k in range(TOP_K):
        base = jnp.sum(jnp.where(eio == e_ref[k:k + 1, :], pstart, 0.0), axis=0, keepdims=True)
        d_ref[k:k + 1, :] = base.astype(I32) + r_ref[k:k + 1, :]
    row0 = (lax.broadcasted_iota(I32, (N_EXPERTS, nbp), 1) * rb).astype(F32)
    be = jnp.sum((pend <= row0).astype(F32), axis=0, keepdims=True)
    be_ref[...] = jnp.minimum(be, N_EXPERTS - 1.0).astype(I32)
    used = jnp.sum(pad_col[:, 0:1], axis=0, keepdims=True) * inv
    nu_ref[...] = jnp.broadcast_to(used, nu_ref.shape).astype(I32)


def _plan(eidx, rank, counts_col, rb, nb):
    t = eidx.shape[1]
    tm = ROUTE_TILE
    nbp = -(-nb // LANES) * LANES
    counts_row = jnp.zeros((8, LANES), F32).at[:, :N_EXPERTS].set(counts_col[:, 0][None, :])
    tok = lambda i: (0, i)
    dest, be, nu = pl.pallas_call(
        functools.partial(_plan_kernel, tm=tm, rb=rb, nbp=nbp),
        grid=(t // tm,),
        in_specs=[
            pl.BlockSpec((TOP_K, tm), tok),
            pl.BlockSpec((TOP_K, tm), tok),
            pl.BlockSpec((N_EXPERTS, LANES), lambda i: (0, 0)),
            pl.BlockSpec((8, LANES), lambda i: (0, 0)),
        ],
        out_specs=[
            pl.BlockSpec((TOP_K, tm), tok),
            pl.BlockSpec((1, nbp), lambda i: (0, 0)),
            pl.BlockSpec((1, LANES), lambda i: (0, 0)),
        ],
        out_shape=[
            jax.ShapeDtypeStruct((TOP_K, t), I32),
            jax.ShapeDtypeStruct((1, nbp), I32),
            jax.ShapeDtypeStruct((1, LANES), I32),
        ],
        compiler_params=_cparams(("arbitrary",)),
        name="moe_plan",
    )(eidx, rank, counts_col, counts_row)
    return dest, be.reshape(nbp), nu[0, :1]


def _store_token_tiles(ref, x):
    n = x.shape[0]
    for s in range(TILE_ROWS):
        ref[pl.ds(s, n, stride=TILE_ROWS), :] = x[:, LANES * s:LANES * (s + 1)]


def _load_token_tiles(ref, n):
    return jnp.concatenate([ref[pl.ds(s, n, stride=TILE_ROWS), :] for s in range(TILE_ROWS)], axis=1)


def _tile_copy(src_ref, src_tok, dst_ref, dst_tok, sem):
    src = src_ref.at[pl.ds(pl.multiple_of(src_tok * TILE_ROWS, TILE_ROWS), TILE_ROWS)]
    dst = dst_ref.at[pl.ds(pl.multiple_of(dst_tok * TILE_ROWS, TILE_ROWS), TILE_ROWS)]
    return pltpu.make_async_copy(src, dst, sem)


def _dispatch_kernel(h_ref, dest_ref, xin_ref, xout_ref, sem, *, tm):
    del xin_ref

    def issue(r, c):
        for k in range(TOP_K):
            _tile_copy(h_ref, r, xout_ref, dest_ref[k, r], sem).start(priority=k % 2)
        return c

    lax.fori_loop(0, tm, issue, 0)

    def drain(r, c):
        for k in range(TOP_K):
            _tile_copy(h_ref, 0, xout_ref, 0, sem).wait()
        return c

    lax.fori_loop(0, tm, drain, 0)


def _dispatch(h_tiles, dest, n_rows):
    t = h_tiles.shape[0] // TILE_ROWS
    tm = MOE_TOK_TILE
    zeros = jnp.zeros((n_rows * TILE_ROWS, LANES), h_tiles.dtype)
    return pl.pallas_call(
        functools.partial(_dispatch_kernel, tm=tm),
        grid=(t // tm,),
        in_specs=[
            pl.BlockSpec((tm * TILE_ROWS, LANES), lambda i: (i, 0)),
            pl.BlockSpec((TOP_K, tm), lambda i: (0, i), memory_space=pltpu.SMEM),
            pl.BlockSpec(memory_space=pl.ANY),
        ],
        out_specs=pl.BlockSpec(memory_space=pl.ANY),
        out_shape=jax.ShapeDtypeStruct((n_rows * TILE_ROWS, LANES), h_tiles.dtype),
        scratch_shapes=[pltpu.SemaphoreType.DMA],
        input_output_aliases={2: 0},
        compiler_params=_cparams(("arbitrary",)),
        name="moe_dispatch",
    )(h_tiles, dest, zeros)


def _expert_kernel(be_ref, nu_ref, x_ref, wgu_ref, bgu_ref, wd_ref, bd_ref, o_ref, wgu_s, wd_s, *, dff, rb):
    b = pl.program_id(0)
    e = be_ref[b]
    prev = be_ref[jnp.maximum(b - 1, 0)]

    @pl.when(jnp.logical_or(b == 0, e != prev))
    def _():
        wgu_s[...] = wgu_ref[0].astype(BF16)
        wd_s[...] = wd_ref[0].astype(BF16)

    @pl.when(b < nu_ref[0])
    def _():
        x = _load_token_tiles(x_ref, rb)
        gu = jnp.dot(x.astype(BF16), wgu_s[...], preferred_element_type=F32) + bgu_ref[0]
        gate = jnp.minimum(gu[:, :dff], SWIGLU_LIMIT)
        up = jnp.clip(gu[:, dff:], -SWIGLU_LIMIT, SWIGLU_LIMIT)
        act = (up + 1.0) * (gate * jax.nn.sigmoid(SWIGLU_ALPHA * gate))
        _store_token_tiles(o_ref, jnp.dot(act.astype(BF16), wd_s[...], preferred_element_type=F32) + bd_ref[0])

    @pl.when(b >= nu_ref[0])
    def _():
        o_ref[...] = jnp.zeros_like(o_ref)


def _expert_ffn(x_disp, blk_expert, n_used, layer, w_gu, b_gu, w_down, b_down, rb):
    nl, ne, d, dff2 = w_gu.shape
    dff = dff2 // 2
    nb = x_disp.shape[0] // (rb * TILE_ROWS)
    pick = lambda b, be, nu: (layer, be[b], 0, 0)
    return pl.pallas_call(
        functools.partial(_expert_kernel, dff=dff, rb=rb),
        grid_spec=pltpu.PrefetchScalarGridSpec(
            num_scalar_prefetch=2,
            grid=(nb,),
            in_specs=[
                pl.BlockSpec((rb * TILE_ROWS, LANES), lambda b, be, nu: (b, 0)),
                pl.BlockSpec((None, 1, d, dff2), pick),
                pl.BlockSpec((None, 1, 1, dff2), pick),
                pl.BlockSpec((None, 1, dff, d), pick),
                pl.BlockSpec((None, 1, 1, d), pick),
            ],
            out_specs=pl.BlockSpec((rb * TILE_ROWS, LANES), lambda b, be, nu: (b, 0)),
            scratch_shapes=[pltpu.VMEM((d, dff2), BF16), pltpu.VMEM((dff, d), BF16)],
        ),
        out_shape=jax.ShapeDtypeStruct(x_disp.shape, F32),
        compiler_params=_cparams(("arbitrary",)),
        name="moe_experts",
    )(blk_expert, n_used, x_disp, w_gu, b_gu.reshape(nl, ne, 1, dff2), w_down, b_down.reshape(nl, ne, 1, d))


def _combine_kernel(dest_ref, gates_ref, x_ref, mod_ref, g_ref, b_ref, y_hbm, o_ref, buf, sem, *, tm, alpha):
    def issue(r, c):
        for k in range(TOP_K):
            _tile_copy(y_hbm, dest_ref[k, r], buf.at[k], r, sem).start(priority=k % 2)
        return c

    lax.fori_loop(0, tm, issue, 0)

    def drain(r, c):
        for k in range(TOP_K):
            _tile_copy(y_hbm, 0, buf.at[k], 0, sem).wait()
        return c

    lax.fori_loop(0, tm, drain, 0)
    y = gates_ref[:, 0:1] * _load_token_tiles(buf.at[0], tm)
    for k in range(1, TOP_K):
        y = y + gates_ref[:, k:k + 1] * _load_token_tiles(buf.at[k], tm)
    r = alpha * x_ref[...] + (1.0 + mod_ref[2:3, :]) * y
    o_ref[...] = _layer_norm(r, g_ref[...], b_ref[...])


def _combine_norm(y_disp, dest, gates_col, x, mods, midx, ln_g, ln_b, seq, alpha):
    t, d = x.shape
    tm = MOE_TOK_TILE
    row = lambda i: (i, 0)
    return pl.pallas_call(
        functools.partial(_combine_kernel, tm=tm, alpha=alpha),
        grid=(t // tm,),
        in_specs=[
            pl.BlockSpec((TOP_K, tm), lambda i: (0, i), memory_space=pltpu.SMEM),
            pl.BlockSpec((tm, TOP_K), row),
            pl.BlockSpec((tm, d), row),
            _mod_spec(midx, seq // tm, d),
            pl.BlockSpec((1, d), lambda i: (0, 0)),
            pl.BlockSpec((1, d), lambda i: (0, 0)),
            pl.BlockSpec(memory_space=pl.ANY),
        ],
        out_specs=pl.BlockSpec((tm, d), row),
        out_shape=jax.ShapeDtypeStruct((t, d), F32),
        scratch_shapes=[pltpu.VMEM((TOP_K, tm * TILE_ROWS, LANES), F32), pltpu.SemaphoreType.DMA],
        compiler_params=_cparams(("arbitrary",)),
        name="moe_combine",
    )(dest, gates_col, x, mods, ln_g.reshape(1, d), ln_b.reshape(1, d), y_disp)


def _moe_block(x, mods, midx, router_w, router_b, layer, w_gu, b_gu, w_down, b_down, ln_g, ln_b, seq, alpha):
    t, d = x.shape
    assert d == TILE_ROWS * LANES
    rb = MOE_ROW_BLOCK
    nb = (t * TOP_K + N_EXPERTS * (rb - 1) + rb - 1) // rb
    h, eidx, rank, gates, counts = _route(x, mods, midx, router_w, router_b, seq)
    dest, blk_expert, n_used = _plan(eidx, rank, counts, rb, nb)
    x_disp = _dispatch(h, dest, nb * rb)
    y_disp = _expert_ffn(x_disp, blk_expert, n_used, layer, w_gu, b_gu, w_down, b_down, rb)
    return _combine_norm(y_disp, dest, gates.T, x, mods, midx, ln_g, ln_b, seq, alpha)


def _inproj_odd_kernel(x_ref, mod_ref, w_ref, wvt_ref, wf_ref, fb_ref, zs_ref, q_ref, k_ref, vt_ref, f_ref,
                       *, width):
    h = (x_ref[...] * (1.0 + mod_ref[1:2, :]) + mod_ref[0:1, :]).astype(BF16)
    z = jnp.dot(h, w_ref[...], preferred_element_type=F32)
    zs_ref[...] = z[:, :width]
    q_ref[...] = (z[:, width:2 * width] * (HEAD_DIM ** -0.5 * LOG2E)).astype(BF16)
    k_ref[...] = z[:, 2 * width:3 * width].astype(BF16)
    vt_ref[...] = lax.dot_general(wvt_ref[...], h, (((1,), (1,)), ((), ())),
                                  preferred_element_type=F32).astype(BF16)
    f_ref[...] = jnp.dot(h, wf_ref[...], preferred_element_type=F32) + fb_ref[...]


def _inproj_odd(x, mods, midx, w_main_bf, wvt_bf, wf_bf, fb, seq):
    t, d = x.shape
    width = wvt_bf.shape[0]
    tm = ROW_TILE
    row = lambda i: (i, 0)
    const = lambda i: (0, 0)
    return pl.pallas_call(
        functools.partial(_inproj_odd_kernel, width=width),
        grid=(t // tm,),
        in_specs=[
            pl.BlockSpec((tm, d), row),
            _mod_spec(midx, seq // tm, d),
            pl.BlockSpec((d, 3 * width), const),
            pl.BlockSpec((width, d), const),
            pl.BlockSpec((d, LANES), const),
            pl.BlockSpec((1, LANES), const),
        ],
        out_specs=[pl.BlockSpec((tm, width), row)] * 3 + [pl.BlockSpec((width, tm), lambda i: (0, i)),
                                                          pl.BlockSpec((tm, LANES), row)],
        out_shape=[jax.ShapeDtypeStruct((t, width), F32)] + [jax.ShapeDtypeStruct((t, width), BF16)] * 2
        + [jax.ShapeDtypeStruct((width, t), BF16), jax.ShapeDtypeStruct((t, LANES), F32)],
        compiler_params=_cparams(("parallel",)),
        name="inproj_odd",
    )(x, mods, w_main_bf, wvt_bf, wf_bf, fb)


def _fkeys_kernel(f_ref, k_ref, sel_ref, o_ref, carry_ref, *, ts, n_pairs):
    @pl.when(pl.program_id(1) == 0)
    def _():
        carry_ref[...] = jnp.zeros_like(carry_ref)

    x = f_ref[...]
    ls = jnp.minimum(x, 0.0) - jnp.log(1.0 + jnp.exp(-jnp.abs(x)))
    upto = (lax.broadcasted_iota(I32, (ts, ts), 1) <= lax.broadcasted_iota(I32, (ts, ts), 0)).astype(F32)
    cs = jnp.dot(upto, ls, precision=HIGHEST, preferred_element_type=F32) + carry_ref[...]
    carry_ref[...] = cs[ts - 1:ts, :]
    rest = -LOG2E * cs
    pieces = []
    for _ in range(FOX_BIAS_PARTS):
        piece = rest.astype(BF16)
        pieces.append(piece)
        rest = rest - piece.astype(F32)
    ext = jnp.dot(jnp.concatenate(pieces, axis=1), sel_ref[...], preferred_element_type=F32).astype(BF16)
    for p in range(n_pairs):
        o_ref[:, 2 * LANES * p:2 * LANES * p + LANES] = k_ref[:, LANES * p:LANES * (p + 1)]
        o_ref[:, 2 * LANES * p + LANES:2 * LANES * (p + 1)] = ext[:, LANES * p:LANES * (p + 1)]


def _forget_keys(flog, k, bsz, seq):
    t, width = k.shape
    n_pairs = width // LANES
    ts = FCUM_TILE
    nt = seq // ts
    sel = [[0.0] * width for _ in range(FOX_BIAS_PARTS * LANES)]
    for h in range(2 * n_pairs):
        for part in range(FOX_BIAS_PARTS):
            sel[part * LANES + h][(h // 2) * LANES + FOX_BIAS_PARTS * (h % 2) + part] = 1.0
    sel = jnp.asarray(sel, BF16)
    tile = lambda b, j: (b * nt + j, 0)
    return pl.pallas_call(
        functools.partial(_fkeys_kernel, ts=ts, n_pairs=n_pairs),
        grid=(bsz, nt),
        in_specs=[pl.BlockSpec((ts, LANES), tile), pl.BlockSpec((ts, width), tile),
                  pl.BlockSpec(sel.shape, lambda b, j: (0, 0))],
        out_specs=pl.BlockSpec((ts, 2 * width), tile),
        out_shape=jax.ShapeDtypeStruct((t, 2 * width), BF16),
        scratch_shapes=[pltpu.VMEM((1, LANES), F32)],
        compiler_params=_cparams(("parallel", "arbitrary")),
        name="forget_keys",
    )(flog, k, sel)


def _fox_kernel(q_ref, ka_ref, vt_ref, o_ref, acc_ref, m_ref, l_ref, qa_ref, st0, st1, p0, p1, al0, al1,
                *, tq, tk):
    st_refs, p_refs, al_refs = (st0, st1), (p0, p1), (al0, al1)
    n_full = pl.program_id(2)
    lane = lax.broadcasted_iota(I32, (tq, LANES), 1)
    qp = q_ref[...]
    for hh in range(2):
        mine = (lane < HEAD_DIM) if hh == 0 else (lane >= HEAD_DIM)
        ones = (lane >= FOX_BIAS_PARTS * hh) & (lane < FOX_BIAS_PARTS * (hh + 1))
        qa_ref[hh] = jnp.concatenate([jnp.where(mine, qp, jnp.zeros_like(qp)),
                                      jnp.where(ones, 1.0, 0.0).astype(BF16)], axis=1)
    acc_ref[...] = jnp.zeros_like(acc_ref)
    m_ref[...] = jnp.full_like(m_ref, NEG_INF)
    l_ref[...] = jnp.zeros_like(l_ref)

    def scores(kb, hh):
        off = pl.multiple_of(kb * tk, tk)
        return lax.dot_general(ka_ref[pl.ds(off, tk), :], qa_ref[hh], (((1,), (1,)), ((), ())),
                               preferred_element_type=F32)

    def softmax_step(st, hh):
        m_old = m_ref[hh:hh + 1, :]
        m_new = jnp.maximum(m_old, jnp.max(st, axis=0, keepdims=True))
        alpha = jnp.exp2(m_old - m_new)
        p = jnp.exp2(st - m_new)
        l_ref[hh:hh + 1, :] = alpha * l_ref[hh:hh + 1, :] + jnp.sum(p, axis=0, keepdims=True)
        m_ref[hh:hh + 1, :] = m_new
        return alpha, p.astype(BF16)

    def accumulate(kb, hh, alpha, p):
        off = pl.multiple_of(kb * tk, tk)
        rows = slice(HEAD_DIM * hh, HEAD_DIM * (hh + 1))
        acc_ref[rows, :] = alpha * acc_ref[rows, :] + jnp.dot(vt_ref[rows, pl.ds(off, tk)], p,
                                                               preferred_element_type=F32)

    keep = lax.broadcasted_iota(I32, (tk, tq), 0) <= lax.broadcasted_iota(I32, (tk, tq), 1)
    for hh in range(2):
        alpha, p = softmax_step(jnp.where(keep, scores(n_full, hh), NEG_INF), hh)
        accumulate(n_full, hh, alpha, p)

    def stage_a(kb, slot):
        for hh in range(2):
            st_refs[slot][hh] = scores(kb, hh)

    def stage_b(slot):
        for hh in range(2):
            alpha, p = softmax_step(st_refs[slot][hh], hh)
            al_refs[slot][hh:hh + 1, :] = alpha
            p_refs[slot][hh] = p

    def stage_c(kb, slot):
        for hh in range(2):
            accumulate(kb, hh, al_refs[slot][hh:hh + 1, :], p_refs[slot][hh])

    def step(i, slot):
        stage_a(i, slot)
        stage_b(1 - slot)
        stage_c(i - 2, slot)

    odd = lax.rem(n_full, 2) == 1

    @pl.when(n_full >= 1)
    def _():
        stage_a(0, 0)

    @pl.when(n_full >= 2)
    def _():
        stage_a(1, 1)
        stage_b(0)

    @pl.when(n_full == 1)
    def _():
        stage_b(0)
        stage_c(0, 0)

    def body(j, c):
        i = 2 + 2 * j
        step(i, 0)
        step(i + 1, 1)
        return c

    lax.fori_loop(0, (n_full - 2) // 2, body, 0)

    @pl.when(jnp.logical_and(n_full >= 2, jnp.logical_not(odd)))
    def _():
        stage_b(1)
        stage_c(n_full - 2, 0)
        stage_c(n_full - 1, 1)

    @pl.when(jnp.logical_and(n_full >= 3, odd))
    def _():
        step(n_full - 1, 0)
        stage_b(0)
        stage_c(n_full - 2, 1)
        stage_c(n_full - 1, 0)

    head0 = lax.broadcasted_iota(I32, (LANES, tq), 0) < HEAD_DIM
    out_t = acc_ref[...] / jnp.where(head0, l_ref[0:1, :], l_ref[1:2, :])
    o_ref[...] = out_t.T.astype(o_ref.dtype)


def _forgetting_attention(q, kaug, vt, bsz, seq):
    t, width = q.shape
    n_pairs = width // LANES
    tq, tk = FOX_TQ, FOX_TK
    nq = seq // tq
    return pl.pallas_call(
        functools.partial(_fox_kernel, tq=tq, tk=tk),
        grid=(bsz, n_pairs, nq),
        in_specs=[
            pl.BlockSpec((tq, LANES), lambda b, h, i: (b * nq + i, h)),
            pl.BlockSpec((seq, 2 * LANES), lambda b, h, i: (b, h)),
            pl.BlockSpec((LANES, seq), lambda b, h, i: (h, b)),
        ],
        out_specs=pl.BlockSpec((tq, LANES), lambda b, h, i: (b * nq + i, h)),
        out_shape=jax.ShapeDtypeStruct((t, width), BF16),
        scratch_shapes=[pltpu.VMEM((LANES, tq), F32), pltpu.VMEM((8, tq), F32), pltpu.VMEM((8, tq), F32),
                        pltpu.VMEM((2, tq, 2 * LANES), BF16)]
        + [pltpu.VMEM((2, tk, tq), F32)] * 2 + [pltpu.VMEM((2, tk, tq), BF16)] * 2 + [pltpu.VMEM((8, tq), F32)] * 2,
        compiler_params=_cparams(("parallel", "parallel", "arbitrary")),
        name="forgetting_attention",
    )(q, kaug, vt)


def _s5_prep_kernel(lre_ref, lim_ref, ldt_ref, bre_ref, bim_ref, are_ref, aim_ref, bb_re, bb_im):
    dt = jnp.exp(ldt_ref[...])
    lre = jnp.minimum(lre_ref[...], -1e-4)
    lim = lim_ref[...]
    mag = jnp.exp(lre * dt)
    a_re = mag * jnp.cos(lim * dt)
    a_im = mag * jnp.sin(lim * dt)
    den = lre * lre + lim * lim
    nre, nim = a_re - 1.0, a_im
    g_re = (nre * lre + nim * lim) / den
    g_im = (nim * lre - nre * lim) / den
    n = lre.shape[-1]
    per = n // bre_ref.shape[0]
    for j in range(bre_ref.shape[0]):
        gr = g_re[:, per * j:per * (j + 1)]
        gi = g_im[:, per * j:per * (j + 1)]
        bb_re[j] = (gr * bre_ref[j] - gi * bim_ref[j]).astype(bb_re.dtype)
        bb_im[j] = (gr * bim_ref[j] + gi * bre_ref[j]).astype(bb_im.dtype)
    are_ref[...] = a_re
    aim_ref[...] = a_im


def _block_diag_slabs(blocks, slabs):
    g, r, c = blocks.shape
    per = g // slabs
    eye = jnp.eye(per, dtype=bool)[None, :, None, :, None]
    bd = jnp.where(eye, blocks.reshape(slabs, per, r, 1, c), jnp.zeros((), blocks.dtype))
    return bd.reshape(slabs, per * r, per * c)


def _s5_kernel(u_ref, are_ref, aim_ref, bb_re, bb_im, cc_re, cc_im, dsk_ref, gw_ref, gb_ref, o_ref,
               xr0, xr1, xi0, xi1, carry_ref, *, ts, slabs):
    @pl.when(pl.program_id(1) == 0)
    def _():
        carry_ref[...] = jnp.zeros_like(carry_ref)

    xr, xi = (xr0, xr1), (xi0, xi1)
    per_half = TILE_ROWS // 2
    u = u_ref[...]
    ub = u.astype(BF16)
    for j in range(slabs):
        uj = ub[:, LANES * j:LANES * (j + 1)]
        bur = jnp.dot(uj, bb_re[j], preferred_element_type=F32)
        bui = jnp.dot(uj, bb_im[j], preferred_element_type=F32)
        half, base = j // 2, (j % 2) * per_half
        for s in range(per_half):
            xr[half][pl.ds(base + s, ts, stride=TILE_ROWS), :] = bur[:, LANES * s:LANES * (s + 1)]
            xi[half][pl.ds(base + s, ts, stride=TILE_ROWS), :] = bui[:, LANES * s:LANES * (s + 1)]
    a_r = [are_ref[TILE_ROWS * h:TILE_ROWS * (h + 1), :] for h in range(2)]
    a_i = [aim_ref[TILE_ROWS * h:TILE_ROWS * (h + 1), :] for h in range(2)]

    def steps(tb, state):
        state = list(state)
        for tt in range(TILE_ROWS):
            row = pl.multiple_of((tb * TILE_ROWS + tt) * TILE_ROWS, TILE_ROWS)
            for h in range(2):
                sr, si = state[h], state[2 + h]
                nr = a_r[h] * sr - a_i[h] * si + xr[h][pl.ds(row, TILE_ROWS), :]
                ni = a_r[h] * si + a_i[h] * sr + xi[h][pl.ds(row, TILE_ROWS), :]
                xr[h][pl.ds(row, TILE_ROWS), :] = nr
                xi[h][pl.ds(row, TILE_ROWS), :] = ni
                state[h], state[2 + h] = nr, ni
        return tuple(state)

    final = lax.fori_loop(0, ts // TILE_ROWS, steps, tuple(carry_ref[i] for i in range(4)))
    for i in range(4):
        carry_ref[i] = final[i]
    ys = []
    for j in range(slabs):
        half, base = j // 2, (j % 2) * per_half
        gather = lambda ref: jnp.concatenate(
            [ref[pl.ds(base + s, ts, stride=TILE_ROWS), :] for s in range(per_half)], axis=1).astype(BF16)
        yj = jnp.dot(gather(xr[half]), cc_re[j], preferred_element_type=F32)
        ys.append(yj - jnp.dot(gather(xi[half]), cc_im[j], preferred_element_type=F32))
    y = jnp.concatenate(ys, axis=1) + dsk_ref[...] * u
    y = 0.5 * y * (1.0 + jnp.tanh(math.sqrt(2.0 / math.pi) * (y + 0.044715 * (y * y * y))))
    gate = jnp.dot(y.astype(BF16), gw_ref[...], preferred_element_type=F32) + gb_ref[...]
    o_ref[...] = (y * jax.nn.sigmoid(gate)).astype(o_ref.dtype)


def _s5_mixer(u, lam_re, lam_im, log_dt, b_re, b_im, c_re, c_im, d_skip, glu_w, glu_b, bsz, seq):
    t, width = u.shape
    groups, states = lam_re.shape
    n = groups * states
    slabs = width // LANES
    ts = S5_TILE
    assert n == 2 * TILE_ROWS * LANES and slabs == 4
    flat = lambda a: a.reshape(1, n)
    ldt = jnp.repeat(log_dt, states).reshape(1, n)
    bre_bd = _block_diag_slabs(jnp.swapaxes(b_re, 1, 2), slabs)
    bim_bd = _block_diag_slabs(jnp.swapaxes(b_im, 1, 2), slabs)
    cre_bd = _block_diag_slabs(jnp.swapaxes(c_re, 1, 2), slabs).astype(BF16)
    cim_bd = _block_diag_slabs(jnp.swapaxes(c_im, 1, 2), slabs).astype(BF16)
    full = lambda a: pl.BlockSpec(a.shape, lambda *_: (0,) * a.ndim)
    prep_in = (flat(lam_re), flat(lam_im), ldt, bre_bd, bim_bd)
    a_re, a_im, bb_re, bb_im = pl.pallas_call(
        _s5_prep_kernel,
        grid=(1,),
        in_specs=[full(a) for a in prep_in],
        out_specs=[pl.BlockSpec((1, n), lambda i: (0, 0))] * 2 + [pl.BlockSpec(bre_bd.shape, lambda i: (0, 0, 0))] * 2,
        out_shape=[jax.ShapeDtypeStruct((1, n), F32)] * 2 + [jax.ShapeDtypeStruct(bre_bd.shape, BF16)] * 2,
        compiler_params=_cparams(("arbitrary",)),
        name="s5_prep",
    )(*prep_in)
    nt = seq // ts
    tiles = lambda a: a.reshape(n // LANES, LANES)
    consts = (tiles(a_re), tiles(a_im), bb_re, bb_im, cre_bd, cim_bd, d_skip.reshape(1, width),
              glu_w.astype(BF16), glu_b.reshape(1, width))
    return pl.pallas_call(
        functools.partial(_s5_kernel, ts=ts, slabs=slabs),
        grid=(bsz, nt),
        in_specs=[pl.BlockSpec((ts, width), lambda b, j: (b * nt + j, 0))] + [full(a) for a in consts],
        out_specs=pl.BlockSpec((ts, width), lambda b, j: (b * nt + j, 0)),
        out_shape=jax.ShapeDtypeStruct((t, width), BF16),
        scratch_shapes=[pltpu.VMEM((ts * TILE_ROWS, LANES), F32)] * 4 + [pltpu.VMEM((4, TILE_ROWS, LANES), F32)],
        compiler_params=_cparams(("parallel", "arbitrary")),
        name="s5_mixer",
    )(u, *consts)


def kernel(x, c, mod_w, mod_b, ln_g, ln_b, even_w_in, pool_w, pool_scale, rel_bias, even_w_out, odd_w_in, forget_b, ssm_lam_re, ssm_lam_im, ssm_log_dt, ssm_b_re, ssm_b_im, ssm_c_re, ssm_c_im, ssm_d, ssm_glu_w, ssm_glu_b, odd_w_out, router_w, router_b, exp_w_gu, exp_b_gu, exp_w_down, exp_b_down):
    bsz, seq, d = x.shape
    depth = mod_w.shape[0]
    alpha = (2.0 * depth) ** 0.25
    mods = _ada_mods(c, mod_w.reshape(depth * 2, d, 3 * d), mod_b.reshape(depth * 2, 3 * d))
    xt = x.reshape(bsz * seq, d)
    for layer in range(depth):
        i = layer // 2
        if layer % 2 == 0:
            w_in = even_w_in[i]
            width = w_in.shape[1] // 4
            zp, q, k, vt = _inproj_even(xt, mods, 2 * layer, w_in[:, :3 * width].astype(BF16),
                                        w_in[:, 3 * width:].T.astype(BF16), seq)
            ya = _pool_mixer(zp, pool_w[i].astype(BF16), pool_scale[i], seq)
            yb = _chunk_attention(q, k, vt, _cattn_bias_table(rel_bias[i]), bsz, seq)
            w_out = even_w_out[i]
        else:
            width = ssm_d.shape[1] * ssm_d.shape[2]
            w_in = odd_w_in[i]
            n_heads = forget_b.shape[1]
            wf = jnp.zeros((d, LANES), BF16).at[:, :n_heads].set(w_in[:, 4 * width:].astype(BF16))
            fb = jnp.zeros((1, LANES), F32).at[0, :n_heads].set(forget_b[i])
            zs, q, k, vt, flog = _inproj_odd(xt, mods, 2 * layer, w_in[:, :3 * width].astype(BF16),
                                             w_in[:, 3 * width:4 * width].T.astype(BF16), wf, fb, seq)
            ya = _s5_mixer(zs, ssm_lam_re[i], ssm_lam_im[i], ssm_log_dt[i], ssm_b_re[i], ssm_b_im[i],
                           ssm_c_re[i], ssm_c_im[i], ssm_d[i], ssm_glu_w[i], ssm_glu_b[i], bsz, seq)
            yb = _forgetting_attention(q, _forget_keys(flog, k, bsz, seq), vt, bsz, seq)
            w_out = odd_w_out[i]
        xt = _outproj_norm(ya, yb, xt, mods, 2 * layer, w_out.astype(BF16), ln_g[layer, 0], ln_b[layer, 0],
                           seq, alpha)
        xt = _moe_block(xt, mods, 2 * layer + 1, router_w[layer], router_b[layer], layer, exp_w_gu, exp_b_gu,
                        exp_w_down, exp_b_down, ln_g[layer, 1], ln_b[layer, 1], seq, alpha)
    return xt.reshape(bsz, seq, d)
```

```python
import functools
import math

import jax
import jax.numpy as jnp
from jax import lax
from jax.experimental import pallas as pl
from jax.experimental.pallas import tpu as pltpu

F32 = jnp.float32
BF16 = jnp.bfloat16
I32 = jnp.int32
HIGHEST = lax.Precision.HIGHEST

LANES = 128
SUBLANES = 8
VMEM_LIMIT_BYTES = 60000 * 1024

HEAD_DIM = 64
CHUNK = 64
LEFT_CHUNKS = 8
REL_CLIP = 128
POOL_WINDOWS = (2, 4, 8, 16)
POOL_HALO = 16
SSM_GROUP_DIM = 16
SSM_STATE = 64
N_EXPERTS = 32
TOP_K = 4
SWIGLU_LIMIT = 7.0
SWIGLU_ALPHA = 1.702
LN_EPS = 1e-5
NEG_INF = -1e30
LOG2E = math.log2(math.e)
FOX_BIAS_PARTS = 3

ROW_TILE = 512
CATTN_Q = 2 * CHUNK
CATTN_BAND = (LEFT_CHUNKS + 2) * CHUNK
FOX_TQ = 512
FOX_TK = 512
S5_TILE = 256
FCUM_TILE = 512
TILE_ROWS = SUBLANES
MOE_ROW_BLOCK = 256
MOE_TOK_TILE = 256
ROUTE_TILE = 512


def _cparams(sem):
    return pltpu.CompilerParams(dimension_semantics=sem, vmem_limit_bytes=VMEM_LIMIT_BYTES)


def _layer_norm(r, g, b):
    mu = jnp.mean(r, axis=-1, keepdims=True)
    d = r - mu
    var = jnp.mean(d * d, axis=-1, keepdims=True)
    return d * lax.rsqrt(var + LN_EPS) * g + b


def _mod_kernel(c_ref, w_ref, b_ref, o_ref):
    c = c_ref[...]
    s = c * jax.nn.sigmoid(c)
    o_ref[0] = jnp.dot(s, w_ref[0], precision=HIGHEST, preferred_element_type=F32) + b_ref[0]


def _ada_mods(c, mod_w, mod_b):
    bsz, d = c.shape
    m = mod_w.shape[0]
    rows = 8
    c_pad = jnp.zeros((rows, d), F32).at[:bsz].set(c)
    out = pl.pallas_call(
        _mod_kernel,
        grid=(m, 3),
        in_specs=[
            pl.BlockSpec((rows, d), lambda i, j: (0, 0)),
            pl.BlockSpec((1, d, d), lambda i, j: (i, 0, j)),
            pl.BlockSpec((1, 1, d), lambda i, j: (i, 0, j)),
        ],
        out_specs=pl.BlockSpec((1, rows, d), lambda i, j: (i, 0, j)),
        out_shape=jax.ShapeDtypeStruct((m, rows, 3 * d), F32),
        compiler_params=_cparams(("parallel", "parallel")),
        name="ada_mod",
    )(c_pad, mod_w, mod_b.reshape(m, 1, 3 * d))
    return out[:, :bsz].reshape(m, bsz, 3, d)


def _mod_spec(midx, tiles_per_batch, d):
    return pl.BlockSpec((None, None, 3, d), lambda i: (midx, i // tiles_per_batch, 0, 0))


def _inproj_even_kernel(x_ref, mod_ref, w_ref, wvt_ref, zp_ref, q_ref, k_ref, vt_ref, *, width):
    h = (x_ref[...] * (1.0 + mod_ref[1:2, :]) + mod_ref[0:1, :]).astype(BF16)
    z = jnp.dot(h, w_ref[...], preferred_element_type=F32)
    zp_ref[...] = z[:, :width]
    q_ref[...] = (z[:, width:2 * width] * (HEAD_DIM ** -0.5 * LOG2E)).astype(BF16)
    k_ref[...] = z[:, 2 * width:3 * width].astype(BF16)
    vt_ref[...] = lax.dot_general(wvt_ref[...], h, (((1,), (1,)), ((), ())),
                                  preferred_element_type=F32).astype(BF16)


def _inproj_even(x, mods, midx, w_main_bf, wvt_bf, seq):
    t, d = x.shape
    width = wvt_bf.shape[0]
    tm = ROW_TILE
    row = lambda i: (i, 0)
    return pl.pallas_call(
        functools.partial(_inproj_even_kernel, width=width),
        grid=(t // tm,),
        in_specs=[
            pl.BlockSpec((tm, d), row),
            _mod_spec(midx, seq // tm, d),
            pl.BlockSpec((d, 3 * width), lambda i: (0, 0)),
            pl.BlockSpec((width, d), lambda i: (0, 0)),
        ],
        out_specs=[pl.BlockSpec((tm, width), row)] * 3
        + [pl.BlockSpec((None, width, tm), lambda i: (i // (seq // tm), 0, i % (seq // tm)))],
        out_shape=[jax.ShapeDtypeStruct((t, width), F32)] + [jax.ShapeDtypeStruct((t, width), BF16)] * 2
        + [jax.ShapeDtypeStruct((t // seq, width, seq), BF16)],
        compiler_params=_cparams(("parallel",)),
        name="inproj_even",
    )(x, mods, w_main_bf, wvt_bf)


def _pool_kernel(zp_ref, halo_ref, w_ref, sc_ref, o_ref, xs_ref, *, ts, seq):
    i = pl.program_id(0)
    t0 = lax.rem(i * ts, seq)
    xs_ref[0:POOL_HALO, :] = jnp.where(t0 == 0, 0.0, halo_ref[...])
    xs_ref[POOL_HALO:POOL_HALO + ts, :] = zp_ref[...]
    pos = lax.broadcasted_iota(I32, (ts, LANES), 0) + t0
    for g, win in enumerate(POOL_WINDOWS):
        cols = slice(LANES * g, LANES * (g + 1))
        u = xs_ref[POOL_HALO:POOL_HALO + ts, cols]
        acc = u
        for k in range(1, win):
            acc = acc + xs_ref[POOL_HALO - k:POOL_HALO - k + ts, cols]
        cnt = jnp.minimum(pos + 1, win).astype(F32)
        dlt = acc / cnt - u
        y = jnp.dot(dlt.astype(BF16), w_ref[g], preferred_element_type=F32)
        o_ref[:, cols] = (y * sc_ref[:, cols]).astype(o_ref.dtype)


def _pool_mixer(zp, w_pool_bf, pool_scale, seq):
    t, width = zp.shape
    ts = ROW_TILE
    hb = ts // POOL_HALO
    return pl.pallas_call(
        functools.partial(_pool_kernel, ts=ts, seq=seq),
        grid=(t // ts,),
        in_specs=[
            pl.BlockSpec((ts, width), lambda i: (i, 0)),
            pl.BlockSpec((POOL_HALO, width), lambda i: (jnp.maximum(i * hb - 1, 0), 0)),
            pl.BlockSpec(w_pool_bf.shape, lambda i: (0, 0, 0)),
            pl.BlockSpec((1, width), lambda i: (0, 0)),
        ],
        out_specs=pl.BlockSpec((ts, width), lambda i: (i, 0)),
        out_shape=jax.ShapeDtypeStruct((t, width), BF16),
        scratch_shapes=[pltpu.VMEM((ts + POOL_HALO, width), F32)],
        compiler_params=_cparams(("parallel",)),
        name="pool_mixer",
    )(zp, zp, w_pool_bf, pool_scale.reshape(1, width))


def _cbias_kernel(r_ref, o_ref):
    base = jnp.broadcast_to(r_ref[...], (CATTN_Q, r_ref.shape[-1]))
    rolled = pltpu.roll(base, 0, 1, stride=1, stride_axis=0)[:, :CATTN_BAND]
    qi = lax.broadcasted_iota(I32, (CATTN_Q, CATTN_BAND), 0)
    kp = lax.broadcasted_iota(I32, (CATTN_Q, CATTN_BAND), 1)
    lo = (qi // CHUNK) * CHUNK
    valid = (kp >= lo) & (kp < lo + (LEFT_CHUNKS + 1) * CHUNK)
    o_ref[...] = jnp.where(valid, rolled * LOG2E, NEG_INF).T


def _cattn_bias_table(rel_bias):
    h = rel_bias.shape[0]
    wide = 1024
    far = LEFT_CHUNKS * CHUNK - REL_CLIP + 1
    tail = wide - far - (2 * REL_CLIP - 1)
    last = rel_bias[:, 2 * REL_CLIP:]
    base = jnp.concatenate(
        [jnp.broadcast_to(last, (h, far)), rel_bias[:, 1:2 * REL_CLIP][:, ::-1], jnp.broadcast_to(last, (h, tail))],
        axis=1).reshape(h, 1, wide)
    return pl.pallas_call(
        _cbias_kernel,
        grid=(h,),
        in_specs=[pl.BlockSpec((None, 1, wide), lambda i: (i, 0, 0))],
        out_specs=pl.BlockSpec((None, CATTN_BAND, CATTN_Q), lambda i: (i, 0, 0)),
        out_shape=jax.ShapeDtypeStruct((h, CATTN_BAND, CATTN_Q), F32),
        compiler_params=_cparams(("parallel",)),
        name="cattn_bias",
    )(base)


def _cattn_kernel(q_ref, k_ref, vt_ref, bias_ref, o_ref, st_ref, e_ref, l_ref, *, n_pairs):
    p = pl.program_id(1)
    start = pl.multiple_of(p * CATTN_Q, CATTN_Q)
    kpos = lax.broadcasted_iota(I32, (CATTN_BAND, CATTN_Q), 0) + start
    real_key = kpos >= LEFT_CHUNKS * CHUNK
    lane = lax.broadcasted_iota(I32, (CATTN_Q, LANES), 1)
    n_heads = 2 * n_pairs
    for h in range(n_heads):
        cols = slice(LANES * (h // 2), LANES * (h // 2 + 1))
        qp = q_ref[:, cols]
        mine = (lane < HEAD_DIM) if h % 2 == 0 else (lane >= HEAD_DIM)
        qm = jnp.where(mine, qp, jnp.zeros_like(qp))
        st_ref[h] = lax.dot_general(k_ref[pl.ds(start, CATTN_BAND), cols], qm, (((1,), (1,)), ((), ())),
                                    preferred_element_type=F32)
    for h in range(n_heads):
        st = jnp.where(real_key, st_ref[h] + bias_ref[h], NEG_INF)
        m = jnp.max(st, axis=0, keepdims=True)
        e = jnp.exp2(st - m)
        l_ref[h:h + 1, :] = jnp.sum(e, axis=0, keepdims=True)
        e_ref[h] = e.astype(BF16)
    for hp in range(n_pairs):
        outs = []
        for hh in range(2):
            h = 2 * hp + hh
            vth = vt_ref[HEAD_DIM * h:HEAD_DIM * (h + 1), pl.ds(start, CATTN_BAND)]
            outs.append(jnp.dot(vth, e_ref[h], preferred_element_type=F32) / l_ref[h:h + 1, :])
        o_ref[:, LANES * hp:LANES * (hp + 1)] = jnp.concatenate(outs, axis=0).T.astype(o_ref.dtype)


def _chunk_attention(q, k, vt, bias_t, bsz, seq):
    t, width = q.shape
    n_pairs = width // LANES
    pad = LEFT_CHUNKS * CHUNK
    kp = jnp.pad(k.reshape(bsz, seq, width), ((0, 0), (pad, 0), (0, 0)))
    vtp = jnp.pad(vt, ((0, 0), (0, 0), (pad, 0)))
    nq = seq // CATTN_Q
    return pl.pallas_call(
        functools.partial(_cattn_kernel, n_pairs=n_pairs),
        grid=(bsz, nq),
        in_specs=[
            pl.BlockSpec((CATTN_Q, width), lambda b, p: (b * nq + p, 0)),
            pl.BlockSpec((None, seq + pad, width), lambda b, p: (b, 0, 0)),
            pl.BlockSpec((None, width, seq + pad), lambda b, p: (b, 0, 0)),
            pl.BlockSpec(bias_t.shape, lambda b, p: (0, 0, 0)),
        ],
        out_specs=pl.BlockSpec((CATTN_Q, width), lambda b, p: (b * nq + p, 0)),
        out_shape=jax.ShapeDtypeStruct((t, width), BF16),
        scratch_shapes=[pltpu.VMEM((2 * n_pairs, CATTN_BAND, CATTN_Q), F32),
                        pltpu.VMEM((2 * n_pairs, CATTN_BAND, CATTN_Q), BF16),
                        pltpu.VMEM((2 * n_pairs, CATTN_Q), F32)],
        compiler_params=_cparams(("parallel", "arbitrary")),
        name="chunk_attention",
    )(q, kp, vtp, bias_t)


def _outproj_kernel(ya_ref, yb_ref, x_ref, mod_ref, w_ref, g_ref, b_ref, o_ref, *, alpha, half):
    y = jnp.dot(ya_ref[...], w_ref[0:half, :], preferred_element_type=F32)
    y = y + jnp.dot(yb_ref[...], w_ref[half:2 * half, :], preferred_element_type=F32)
    r = alpha * x_ref[...] + (1.0 + mod_ref[2:3, :]) * y
    o_ref[...] = _layer_norm(r, g_ref[...], b_ref[...])


def _outproj_norm(ya, yb, x, mods, midx, w_out_bf, ln_g, ln_b, seq, alpha):
    t, d = x.shape
    half = ya.shape[1]
    tm = ROW_TILE
    row = lambda i: (i, 0)
    return pl.pallas_call(
        functools.partial(_outproj_kernel, alpha=alpha, half=half),
        grid=(t // tm,),
        in_specs=[
            pl.BlockSpec((tm, half), row),
            pl.BlockSpec((tm, half), row),
            pl.BlockSpec((tm, d), row),
            _mod_spec(midx, seq // tm, d),
            pl.BlockSpec((2 * half, d), lambda i: (0, 0)),
            pl.BlockSpec((1, d), lambda i: (0, 0)),
            pl.BlockSpec((1, d), lambda i: (0, 0)),
        ],
        out_specs=pl.BlockSpec((tm, d), row),
        out_shape=jax.ShapeDtypeStruct((t, d), F32),
        compiler_params=_cparams(("parallel",)),
        name="outproj_norm",
    )(ya, yb, x, mods, w_out_bf, ln_g.reshape(1, d), ln_b.reshape(1, d))


def _route_kernel(x_ref, mod_ref, rw_ref, rb_ref, h_ref, e_ref, r_ref, g_ref, c_ref, carry_ref, *, tm):
    @pl.when(pl.program_id(0) == 0)
    def _():
        carry_ref[...] = jnp.zeros_like(carry_ref)

    h = x_ref[...] * (1.0 + mod_ref[1:2, :]) + mod_ref[0:1, :]
    _store_token_tiles(h_ref, h)
    logits = lax.dot_general(rw_ref[...], h, (((1,), (1,)), ((), ())), precision=HIGHEST,
                             preferred_element_type=F32) + rb_ref[...]
    eio = lax.broadcasted_iota(I32, (N_EXPERTS, tm), 0)
    vals, hots = [], []
    for k in range(TOP_K):
        m = jnp.max(logits, axis=0, keepdims=True)
        idx = jnp.min(jnp.where(logits == m, eio, N_EXPERTS), axis=0, keepdims=True)
        hot = eio == idx
        e_ref[k:k + 1, :] = idx
        vals.append(m)
        hots.append(hot)
        logits = jnp.where(hot, -jnp.inf, logits)
    exps = [jnp.exp(v - vals[0]) for v in vals]
    denom = exps[0] + exps[1] + exps[2] + exps[3]
    for k in range(TOP_K):
        g_ref[k:k + 1, :] = exps[k] / denom
    cnt = jnp.zeros((N_EXPERTS, tm), F32)
    for hot in hots:
        cnt = cnt + hot.astype(F32)
    before = (lax.broadcasted_iota(I32, (tm, tm), 0) < lax.broadcasted_iota(I32, (tm, tm), 1))
    prefix = jnp.dot(cnt.astype(BF16), before.astype(BF16), preferred_element_type=F32)
    tot = prefix + carry_ref[...]
    for k in range(TOP_K):
        r_ref[k:k + 1, :] = jnp.sum(jnp.where(hots[k], tot, 0.0), axis=0, keepdims=True).astype(I32)
    new_carry = carry_ref[...] + jnp.sum(cnt, axis=1, keepdims=True)
    carry_ref[...] = new_carry
    c_ref[...] = jnp.broadcast_to(new_carry, c_ref.shape)


def _route(x, mods, midx, router_w, router_b, seq):
    t, d = x.shape
    tm = ROUTE_TILE
    tok = lambda i: (0, i)
    return pl.pallas_call(
        functools.partial(_route_kernel, tm=tm),
        grid=(t // tm,),
        in_specs=[
            pl.BlockSpec((tm, d), lambda i: (i, 0)),
            _mod_spec(midx, seq // tm, d),
            pl.BlockSpec((N_EXPERTS, d), lambda i: (0, 0)),
            pl.BlockSpec((N_EXPERTS, 1), lambda i: (0, 0)),
        ],
        out_specs=[
            pl.BlockSpec((tm * TILE_ROWS, LANES), lambda i: (i, 0)),
            pl.BlockSpec((TOP_K, tm), tok),
            pl.BlockSpec((TOP_K, tm), tok),
            pl.BlockSpec((TOP_K, tm), tok),
            pl.BlockSpec((N_EXPERTS, LANES), lambda i: (0, 0)),
        ],
        out_shape=[
            jax.ShapeDtypeStruct((t * TILE_ROWS, LANES), F32),
            jax.ShapeDtypeStruct((TOP_K, t), I32),
            jax.ShapeDtypeStruct((TOP_K, t), I32),
            jax.ShapeDtypeStruct((TOP_K, t), F32),
            jax.ShapeDtypeStruct((N_EXPERTS, LANES), F32),
        ],
        scratch_shapes=[pltpu.VMEM((N_EXPERTS, 1), F32)],
        compiler_params=_cparams(("arbitrary",)),
        name="moe_route",
    )(x, mods, router_w.T, router_b.reshape(N_EXPERTS, 1))


def _plan_kernel(e_ref, r_ref, ccol_ref, crow_ref, d_ref, blk_ref, exp_ref, nu_ref, *, tm, rb, nbp):
    inv = 1.0 / rb
    pad_col = jnp.floor((ccol_ref[...] + (rb - 1)) * inv) * rb
    pad_row = jnp.floor((crow_ref[0:1, :] + (rb - 1)) * inv) * rb
    ei = lax.broadcasted_iota(I32, (N_EXPERTS, LANES), 0)
    li = lax.broadcasted_iota(I32, (N_EXPERTS, LANES), 1)
    pstart = jnp.sum(jnp.where(li < ei, jnp.broadcast_to(pad_row, (N_EXPERTS, LANES)), 0.0),
                     axis=1, keepdims=True)
    padded = pad_col[:, 0:1]
    pend = pstart + padded
    eio = lax.broadcasted_iota(I32, (N_EXPERTS, tm), 0)
    for k in range(TOP_K):
        base = jnp.sum(jnp.where(eio == e_ref[k:k + 1, :], pstart, 0.0), axis=0, keepdims=True)
        d_ref[k:k + 1, :] = base.astype(I32) + r_ref[k:k + 1, :]
    row0 = (lax.broadcasted_iota(I32, (N_EXPERTS, nbp), 1) * rb).astype(F32)
    be = jnp.minimum(jnp.sum((pend <= row0).astype(F32), axis=0, keepdims=True), N_EXPERTS - 1.0)
    ebi = lax.broadcasted_iota(I32, (N_EXPERTS, nbp), 0).astype(F32)
    mine = ebi == be
    my_end = jnp.sum(jnp.where(mine, pend, 0.0), axis=0, keepdims=True)
    nxt = jnp.sum((pend <= my_end).astype(F32), axis=0, keepdims=True)
    order = jnp.sum(jnp.where(jnp.logical_and(padded > 0.0, ebi < be), 1.0, 0.0), axis=0, keepdims=True)
    blk_ref[0:1, :] = be.astype(I32)
    blk_ref[1:2, :] = jnp.where(nxt < N_EXPERTS, nxt, -1.0).astype(I32)
    blk_ref[2:3, :] = (order - 2.0 * jnp.floor(order * 0.5)).astype(I32)
    blk_ref[3:8, :] = jnp.zeros((5, nbp), I32)
    pstart_row = jnp.sum(jnp.where(ei < li, pad_col, 0.0), axis=0, keepdims=True)
    exp_ref[0:1, :] = crow_ref[0:1, :].astype(I32)
    exp_ref[1:2, :] = pstart_row.astype(I32)
    exp_ref[2:3, :] = pad_row.astype(I32)
    exp_ref[3:8, :] = jnp.zeros((5, LANES), I32)
    used = jnp.sum(padded, axis=0, keepdims=True) * inv
    nu_ref[...] = jnp.broadcast_to(used, nu_ref.shape).astype(I32)


def _plan(eidx, rank, counts_col, rb, nb):
    t = eidx.shape[1]
    tm = ROUTE_TILE
    nbp = -(-nb // LANES) * LANES
    counts_row = jnp.zeros((8, LANES), F32).at[:, :N_EXPERTS].set(counts_col[:, 0][None, :])
    tok = lambda i: (0, i)
    dest, blk, exp, nu = pl.pallas_call(
        functools.partial(_plan_kernel, tm=tm, rb=rb, nbp=nbp),
        grid=(t // tm,),
        in_specs=[
            pl.BlockSpec((TOP_K, tm), tok),
            pl.BlockSpec((TOP_K, tm), tok),
            pl.BlockSpec((N_EXPERTS, LANES), lambda i: (0, 0)),
            pl.BlockSpec((8, LANES), lambda i: (0, 0)),
        ],
        out_specs=[
            pl.BlockSpec((TOP_K, tm), tok),
            pl.BlockSpec((8, nbp), lambda i: (0, 0)),
            pl.BlockSpec((8, LANES), lambda i: (0, 0)),
            pl.BlockSpec((1, LANES), lambda i: (0, 0)),
        ],
        out_shape=[
            jax.ShapeDtypeStruct((TOP_K, t), I32),
            jax.ShapeDtypeStruct((8, nbp), I32),
            jax.ShapeDtypeStruct((8, LANES), I32),
            jax.ShapeDtypeStruct((1, LANES), I32),
        ],
        compiler_params=_cparams(("arbitrary",)),
        name="moe_plan",
    )(eidx, rank, counts_col, counts_row)
    return dest, (blk[0], blk[1], blk[2]), (exp[0], exp[1], exp[2]), nu[0, :1]


def _store_token_tiles(ref, x):
    n = x.shape[0]
    for s in range(TILE_ROWS):
        ref[pl.ds(s, n, stride=TILE_ROWS), :] = x[:, LANES * s:LANES * (s + 1)]


def _load_token_tiles(ref, n):
    return jnp.concatenate([ref[pl.ds(s, n, stride=TILE_ROWS), :] for s in range(TILE_ROWS)], axis=1)


def _tile_copy(src_ref, src_tok, dst_ref, dst_tok, sem):
    src = src_ref.at[pl.ds(pl.multiple_of(src_tok * TILE_ROWS, TILE_ROWS), TILE_ROWS)]
    dst = dst_ref.at[pl.ds(pl.multiple_of(dst_tok * TILE_ROWS, TILE_ROWS), TILE_ROWS)]
    return pltpu.make_async_copy(src, dst, sem)


def _dispatch_kernel(cnt_ref, first_ref, padded_ref, h_ref, dest_ref, xout_ref, zero_ref, sem, zsem, *, tm, n_rows):
    @pl.when(pl.program_id(0) == 0)
    def _():
        zero_ref[...] = jnp.zeros_like(zero_ref)

        def per_expert(e, c):
            lo = first_ref[e] + cnt_ref[e]
            hi = jnp.where(e == N_EXPERTS - 1, n_rows, first_ref[e] + padded_ref[e])

            def fill(r, c2):
                _tile_copy(zero_ref, 0, xout_ref, r, zsem).start()
                return c2

            def fill_done(r, c2):
                _tile_copy(zero_ref, 0, xout_ref, 0, zsem).wait()
                return c2

            lax.fori_loop(lo, hi, fill, 0)
            lax.fori_loop(lo, hi, fill_done, 0)
            return c

        lax.fori_loop(0, N_EXPERTS, per_expert, 0)

    def issue(r, c):
        for k in range(TOP_K):
            _tile_copy(h_ref, r, xout_ref, dest_ref[k, r], sem).start(priority=k % 2)
        return c

    lax.fori_loop(0, tm, issue, 0)

    def drain(r, c):
        for k in range(TOP_K):
            _tile_copy(h_ref, 0, xout_ref, 0, sem).wait()
        return c

    lax.fori_loop(0, tm, drain, 0)


def _dispatch(h_tiles, dest, expert_table, n_rows):
    t = h_tiles.shape[0] // TILE_ROWS
    tm = MOE_TOK_TILE
    return pl.pallas_call(
        functools.partial(_dispatch_kernel, tm=tm, n_rows=n_rows),
        grid_spec=pltpu.PrefetchScalarGridSpec(
            num_scalar_prefetch=3,
            grid=(t // tm,),
            in_specs=[
                pl.BlockSpec((tm * TILE_ROWS, LANES), lambda i, *_: (i, 0)),
                pl.BlockSpec((TOP_K, tm), lambda i, *_: (0, i), memory_space=pltpu.SMEM),
            ],
            out_specs=pl.BlockSpec(memory_space=pl.ANY),
            scratch_shapes=[pltpu.VMEM((TILE_ROWS, LANES), h_tiles.dtype), pltpu.SemaphoreType.DMA,
                            pltpu.SemaphoreType.DMA],
        ),
        out_shape=jax.ShapeDtypeStruct((n_rows * TILE_ROWS, LANES), h_tiles.dtype),
        compiler_params=_cparams(("arbitrary",)),
        name="moe_dispatch",
    )(*expert_table, h_tiles, dest)


def _expert_kernel(be_ref, nx_ref, slot_ref, nu_ref, x_ref, bgu_ref, bd_ref, wgu_hbm, wd_hbm, o_ref,
                   wgu_f, wd_f, wgu_s, wd_s, sems, *, dff, rb, layer):
    b = pl.program_id(0)
    e = be_ref[b]
    slot = slot_ref[b]
    used = b < nu_ref[0]
    first = jnp.logical_and(used, jnp.logical_or(b == 0, e != be_ref[jnp.maximum(b - 1, 0)]))

    def fetch(expert, s):
        return (pltpu.make_async_copy(wgu_hbm.at[layer, expert], wgu_f.at[s], sems.at[0, s]),
                pltpu.make_async_copy(wd_hbm.at[layer, expert], wd_f.at[s], sems.at[1, s]))

    @pl.when(b == 0)
    def _():
        for copy in fetch(e, slot):
            copy.start()

    @pl.when(first)
    def _():
        for copy in fetch(e, slot):
            copy.wait()
        wgu_s[...] = wgu_f[slot].astype(BF16)
        wd_s[...] = wd_f[slot].astype(BF16)

        @pl.when(nx_ref[b] >= 0)
        def _():
            for copy in fetch(nx_ref[b], 1 - slot):
                copy.start()

    @pl.when(used)
    def _():
        x = _load_token_tiles(x_ref, rb)
        gu = jnp.dot(x.astype(BF16), wgu_s[...], preferred_element_type=F32) + bgu_ref[0]
        gate = jnp.minimum(gu[:, :dff], SWIGLU_LIMIT)
        up = jnp.clip(gu[:, dff:], -SWIGLU_LIMIT, SWIGLU_LIMIT)
        act = (up + 1.0) * (gate * jax.nn.sigmoid(SWIGLU_ALPHA * gate))
        _store_token_tiles(o_ref, jnp.dot(act.astype(BF16), wd_s[...], preferred_element_type=F32) + bd_ref[0])

    @pl.when(jnp.logical_not(used))
    def _():
        o_ref[...] = jnp.zeros_like(o_ref)


def _expert_ffn(x_disp, block_table, n_used, layer, w_gu, b_gu, w_down, b_down, rb):
    nl, ne, d, dff2 = w_gu.shape
    dff = dff2 // 2
    nb = x_disp.shape[0] // (rb * TILE_ROWS)
    pick = lambda b, be, *_: (layer, be[b], 0, 0)
    blocks = lambda b, *_: (b, 0)
    return pl.pallas_call(
        functools.partial(_expert_kernel, dff=dff, rb=rb, layer=layer),
        grid_spec=pltpu.PrefetchScalarGridSpec(
            num_scalar_prefetch=4,
            grid=(nb,),
            in_specs=[
                pl.BlockSpec((rb * TILE_ROWS, LANES), blocks),
                pl.BlockSpec((None, 1, 1, dff2), pick),
                pl.BlockSpec((None, 1, 1, d), pick),
                pl.BlockSpec(memory_space=pl.ANY),
                pl.BlockSpec(memory_space=pl.ANY),
            ],
            out_specs=pl.BlockSpec((rb * TILE_ROWS, LANES), blocks),
            scratch_shapes=[pltpu.VMEM((2, d, dff2), F32), pltpu.VMEM((2, dff, d), F32),
                            pltpu.VMEM((d, dff2), BF16), pltpu.VMEM((dff, d), BF16),
                            pltpu.SemaphoreType.DMA((2, 2))],
        ),
        out_shape=jax.ShapeDtypeStruct(x_disp.shape, F32),
        compiler_params=_cparams(("arbitrary",)),
        name="moe_experts",
    )(*block_table, n_used, x_disp, b_gu.reshape(nl, ne, 1, dff2), b_down.reshape(nl, ne, 1, d), w_gu, w_down)


def _combine_kernel(dest_ref, next_ref, gates_ref, x_ref, mod_ref, g_ref, b_ref, y_hbm, o_ref, buf, sems,
                    *, tm, alpha):
    i = pl.program_id(0)
    slot = lax.rem(i, 2)

    def gather(idx_ref, s):
        def issue(r, c):
            for k in range(TOP_K):
                _tile_copy(y_hbm, idx_ref[k, r], buf.at[s, k], r, sems.at[s]).start(priority=k % 2)
            return c

        lax.fori_loop(0, tm, issue, 0)

    @pl.when(i == 0)
    def _():
        gather(dest_ref, slot)

    @pl.when(i + 1 < pl.num_programs(0))
    def _():
        gather(next_ref, 1 - slot)

    def drain(r, c):
        for k in range(TOP_K):
            _tile_copy(y_hbm, 0, buf.at[slot, k], 0, sems.at[slot]).wait()
        return c

    lax.fori_loop(0, tm, drain, 0)
    y = gates_ref[:, 0:1] * _load_token_tiles(buf.at[slot, 0], tm)
    for k in range(1, TOP_K):
        y = y + gates_ref[:, k:k + 1] * _load_token_tiles(buf.at[slot, k], tm)
    r = alpha * x_ref[...] + (1.0 + mod_ref[2:3, :]) * y
    o_ref[...] = _layer_norm(r, g_ref[...], b_ref[...])


def _combine_norm(y_disp, dest, gates_col, x, mods, midx, ln_g, ln_b, seq, alpha):
    t, d = x.shape
    tm = MOE_TOK_TILE
    row = lambda i: (i, 0)
    last = t // tm - 1
    return pl.pallas_call(
        functools.partial(_combine_kernel, tm=tm, alpha=alpha),
        grid=(t // tm,),
        in_specs=[
            pl.BlockSpec((TOP_K, tm), lambda i: (0, i), memory_space=pltpu.SMEM),
            pl.BlockSpec((TOP_K, tm), lambda i: (0, jnp.minimum(i + 1, last)), memory_space=pltpu.SMEM),
            pl.BlockSpec((tm, TOP_K), row),
            pl.BlockSpec((tm, d), row),
            _mod_spec(midx, seq // tm, d),
            pl.BlockSpec((1, d), lambda i: (0, 0)),
            pl.BlockSpec((1, d), lambda i: (0, 0)),
            pl.BlockSpec(memory_space=pl.ANY),
        ],
        out_specs=pl.BlockSpec((tm, d), row),
        out_shape=jax.ShapeDtypeStruct((t, d), F32),
        scratch_shapes=[pltpu.VMEM((2, TOP_K, tm * TILE_ROWS, LANES), F32), pltpu.SemaphoreType.DMA((2,))],
        compiler_params=_cparams(("arbitrary",)),
        name="moe_combine",
    )(dest, dest, gates_col, x, mods, ln_g.reshape(1, d), ln_b.reshape(1, d), y_disp)


def _moe_block(x, mods, midx, router_w, router_b, layer, w_gu, b_gu, w_down, b_down, ln_g, ln_b, seq, alpha):
    t, d = x.shape
    assert d == TILE_ROWS * LANES
    rb = MOE_ROW_BLOCK
    nb = (t * TOP_K + N_EXPERTS * (rb - 1) + rb - 1) // rb
    h, eidx, rank, gates, counts = _route(x, mods, midx, router_w, router_b, seq)
    dest, block_table, expert_table, n_used = _plan(eidx, rank, counts, rb, nb)
    x_disp = _dispatch(h, dest, expert_table, nb * rb)
    y_disp = _expert_ffn(x_disp, block_table, n_used, layer, w_gu, b_gu, w_down, b_down, rb)
    return _combine_norm(y_disp, dest, gates.T, x, mods, midx, ln_g, ln_b, seq, alpha)


def _inproj_odd_kernel(x_ref, mod_ref, w_ref, wvt_ref, wf_ref, fb_ref, zs_ref, q_ref, k_ref, vt_ref, f_ref,
                       *, width):
    h = (x_ref[...] * (1.0 + mod_ref[1:2, :]) + mod_ref[0:1, :]).astype(BF16)
    z = jnp.dot(h, w_ref[...], preferred_element_type=F32)
    zs_ref[...] = z[:, :width]
    q_ref[...] = (z[:, width:2 * width] * (HEAD_DIM ** -0.5 * LOG2E)).astype(BF16)
    k_ref[...] = z[:, 2 * width:3 * width].astype(BF16)
    vt_ref[...] = lax.dot_general(wvt_ref[...], h, (((1,), (1,)), ((), ())),
                                  preferred_element_type=F32).astype(BF16)
    f_ref[...] = jnp.dot(h, wf_ref[...], preferred_element_type=F32) + fb_ref[...]


def _inproj_odd(x, mods, midx, w_main_bf, wvt_bf, wf_bf, fb, seq):
    t, d = x.shape
    width = wvt_bf.shape[0]
    tm = ROW_TILE
    row = lambda i: (i, 0)
    const = lambda i: (0, 0)
    return pl.pallas_call(
        functools.partial(_inproj_odd_kernel, width=width),
        grid=(t // tm,),
        in_specs=[
            pl.BlockSpec((tm, d), row),
            _mod_spec(midx, seq // tm, d),
            pl.BlockSpec((d, 3 * width), const),
            pl.BlockSpec((width, d), const),
            pl.BlockSpec((d, LANES), const),
            pl.BlockSpec((1, LANES), const),
        ],
        out_specs=[pl.BlockSpec((tm, width), row)] * 3 + [pl.BlockSpec((width, tm), lambda i: (0, i)),
                                                          pl.BlockSpec((tm, LANES), row)],
        out_shape=[jax.ShapeDtypeStruct((t, width), F32)] + [jax.ShapeDtypeStruct((t, width), BF16)] * 2
        + [jax.ShapeDtypeStruct((width, t), BF16), jax.ShapeDtypeStruct((t, LANES), F32)],
        compiler_params=_cparams(("parallel",)),
        name="inproj_odd",
    )(x, mods, w_main_bf, wvt_bf, wf_bf, fb)


def _fkeys_kernel(f_ref, k_ref, sel_ref, o_ref, carry_ref, *, ts, n_pairs):
    @pl.when(pl.program_id(1) == 0)
    def _():
        carry_ref[...] = jnp.zeros_like(carry_ref)

    x = f_ref[...]
    ls = jnp.minimum(x, 0.0) - jnp.log(1.0 + jnp.exp(-jnp.abs(x)))
    upto = (lax.broadcasted_iota(I32, (ts, ts), 1) <= lax.broadcasted_iota(I32, (ts, ts), 0)).astype(F32)
    cs = jnp.dot(upto, ls, precision=HIGHEST, preferred_element_type=F32) + carry_ref[...]
    carry_ref[...] = cs[ts - 1:ts, :]
    rest = -LOG2E * cs
    pieces = []
    for _ in range(FOX_BIAS_PARTS):
        piece = rest.astype(BF16)
        pieces.append(piece)
        rest = rest - piece.astype(F32)
    ext = jnp.dot(jnp.concatenate(pieces, axis=1), sel_ref[...], preferred_element_type=F32).astype(BF16)
    for p in range(n_pairs):
        o_ref[:, 2 * LANES * p:2 * LANES * p + LANES] = k_ref[:, LANES * p:LANES * (p + 1)]
        o_ref[:, 2 * LANES * p + LANES:2 * LANES * (p + 1)] = ext[:, LANES * p:LANES * (p + 1)]


def _forget_keys(flog, k, bsz, seq):
    t, width = k.shape
    n_pairs = width // LANES
    ts = FCUM_TILE
    nt = seq // ts
    sel = [[0.0] * width for _ in range(FOX_BIAS_PARTS * LANES)]
    for h in range(2 * n_pairs):
        for part in range(FOX_BIAS_PARTS):
            sel[part * LANES + h][(h // 2) * LANES + FOX_BIAS_PARTS * (h % 2) + part] = 1.0
    sel = jnp.asarray(sel, BF16)
    tile = lambda b, j: (b * nt + j, 0)
    return pl.pallas_call(
        functools.partial(_fkeys_kernel, ts=ts, n_pairs=n_pairs),
        grid=(bsz, nt),
        in_specs=[pl.BlockSpec((ts, LANES), tile), pl.BlockSpec((ts, width), tile),
                  pl.BlockSpec(sel.shape, lambda b, j: (0, 0))],
        out_specs=pl.BlockSpec((ts, 2 * width), tile),
        out_shape=jax.ShapeDtypeStruct((t, 2 * width), BF16),
        scratch_shapes=[pltpu.VMEM((1, LANES), F32)],
        compiler_params=_cparams(("parallel", "arbitrary")),
        name="forget_keys",
    )(flog, k, sel)


def _fox_kernel(q_ref, ka_ref, vt_ref, o_ref, acc_ref, m_ref, l_ref, qa_ref, st0, st1, p0, p1, al0, al1,
                *, tq, tk):
    st_refs, p_refs, al_refs = (st0, st1), (p0, p1), (al0, al1)
    n_full = pl.program_id(2)
    lane = lax.broadcasted_iota(I32, (tq, LANES), 1)
    qp = q_ref[...]
    for hh in range(2):
        mine = (lane < HEAD_DIM) if hh == 0 else (lane >= HEAD_DIM)
        ones = (lane >= FOX_BIAS_PARTS * hh) & (lane < FOX_BIAS_PARTS * (hh + 1))
        qa_ref[hh] = jnp.concatenate([jnp.where(mine, qp, jnp.zeros_like(qp)),
                                      jnp.where(ones, 1.0, 0.0).astype(BF16)], axis=1)
    acc_ref[...] = jnp.zeros_like(acc_ref)
    m_ref[...] = jnp.full_like(m_ref, NEG_INF)
    l_ref[...] = jnp.zeros_like(l_ref)

    def scores(kb, hh):
        off = pl.multiple_of(kb * tk, tk)
        return lax.dot_general(ka_ref[pl.ds(off, tk), :], qa_ref[hh], (((1,), (1,)), ((), ())),
                               preferred_element_type=F32)

    def softmax_step(st, hh):
        m_old = m_ref[hh:hh + 1, :]
        m_new = jnp.maximum(m_old, jnp.max(st, axis=0, keepdims=True))
        alpha = jnp.exp2(m_old - m_new)
        p = jnp.exp2(st - m_new)
        l_ref[hh:hh + 1, :] = alpha * l_ref[hh:hh + 1, :] + jnp.sum(p, axis=0, keepdims=True)
        m_ref[hh:hh + 1, :] = m_new
        return alpha, p.astype(BF16)

    def accumulate(kb, hh, alpha, p):
        off = pl.multiple_of(kb * tk, tk)
        rows = slice(HEAD_DIM * hh, HEAD_DIM * (hh + 1))
        acc_ref[rows, :] = alpha * acc_ref[rows, :] + jnp.dot(vt_ref[rows, pl.ds(off, tk)], p,
                                                               preferred_element_type=F32)

    keep = lax.broadcasted_iota(I32, (tk, tq), 0) <= lax.broadcasted_iota(I32, (tk, tq), 1)
    for hh in range(2):
        alpha, p = softmax_step(jnp.where(keep, scores(n_full, hh), NEG_INF), hh)
        accumulate(n_full, hh, alpha, p)

    def stage_a(kb, slot):
        for hh in range(2):
            st_refs[slot][hh] = scores(kb, hh)

    def stage_b(slot):
        for hh in range(2):
            alpha, p = softmax_step(st_refs[slot][hh], hh)
            al_refs[slot][hh:hh + 1, :] = alpha
            p_refs[slot][hh] = p

    def stage_c(kb, slot):
        for hh in range(2):
            accumulate(kb, hh, al_refs[slot][hh:hh + 1, :], p_refs[slot][hh])

    def step(i, slot):
        stage_a(i, slot)
        stage_b(1 - slot)
        stage_c(i - 2, slot)

    odd = lax.rem(n_full, 2) == 1

    @pl.when(n_full >= 1)
    def _():
        stage_a(0, 0)

    @pl.when(n_full >= 2)
    def _():
        stage_a(1, 1)
        stage_b(0)

    @pl.when(n_full == 1)
    def _():
        stage_b(0)
        stage_c(0, 0)

    def body(j, c):
        i = 2 + 2 * j
        step(i, 0)
        step(i + 1, 1)
        return c

    lax.fori_loop(0, (n_full - 2) // 2, body, 0)

    @pl.when(jnp.logical_and(n_full >= 2, jnp.logical_not(odd)))
    def _():
        stage_b(1)
        stage_c(n_full - 2, 0)
        stage_c(n_full - 1, 1)

    @pl.when(jnp.logical_and(n_full >= 3, odd))
    def _():
        step(n_full - 1, 0)
        stage_b(0)
        stage_c(n_full - 2, 1)
        stage_c(n_full - 1, 0)

    head0 = lax.broadcasted_iota(I32, (LANES, tq), 0) < HEAD_DIM
    out_t = acc_ref[...] / jnp.where(head0, l_ref[0:1, :], l_ref[1:2, :])
    o_ref[...] = out_t.T.astype(o_ref.dtype)


def _forgetting_attention(q, kaug, vt, bsz, seq):
    t, width = q.shape
    n_pairs = width // LANES
    tq, tk = FOX_TQ, FOX_TK
    nq = seq // tq
    return pl.pallas_call(
        functools.partial(_fox_kernel, tq=tq, tk=tk),
        grid=(bsz, n_pairs, nq),
        in_specs=[
            pl.BlockSpec((tq, LANES), lambda b, h, i: (b * nq + i, h)),
            pl.BlockSpec((seq, 2 * LANES), lambda b, h, i: (b, h)),
            pl.BlockSpec((LANES, seq), lambda b, h, i: (h, b)),
        ],
        out_specs=pl.BlockSpec((tq, LANES), lambda b, h, i: (b * nq + i, h)),
        out_shape=jax.ShapeDtypeStruct((t, width), BF16),
        scratch_shapes=[pltpu.VMEM((LANES, tq), F32), pltpu.VMEM((8, tq), F32), pltpu.VMEM((8, tq), F32),
                        pltpu.VMEM((2, tq, 2 * LANES), BF16)]
        + [pltpu.VMEM((2, tk, tq), F32)] * 2 + [pltpu.VMEM((2, tk, tq), BF16)] * 2 + [pltpu.VMEM((8, tq), F32)] * 2,
        compiler_params=_cparams(("parallel", "parallel", "arbitrary")),
        name="forgetting_attention",
    )(q, kaug, vt)


def _s5_prep_kernel(lre_ref, lim_ref, ldt_ref, bre_ref, bim_ref, are_ref, aim_ref, bb_re, bb_im):
    dt = jnp.exp(ldt_ref[...])
    lre = jnp.minimum(lre_ref[...], -1e-4)
    lim = lim_ref[...]
    mag = jnp.exp(lre * dt)
    a_re = mag * jnp.cos(lim * dt)
    a_im = mag * jnp.sin(lim * dt)
    den = lre * lre + lim * lim
    nre, nim = a_re - 1.0, a_im
    g_re = (nre * lre + nim * lim) / den
    g_im = (nim * lre - nre * lim) / den
    n = lre.shape[-1]
    per = n // bre_ref.shape[0]
    for j in range(bre_ref.shape[0]):
        gr = g_re[:, per * j:per * (j + 1)]
        gi = g_im[:, per * j:per * (j + 1)]
        bb_re[j] = (gr * bre_ref[j] - gi * bim_ref[j]).astype(bb_re.dtype)
        bb_im[j] = (gr * bim_ref[j] + gi * bre_ref[j]).astype(bb_im.dtype)
    are_ref[...] = a_re
    aim_ref[...] = a_im


def _block_diag_slabs(blocks, slabs):
    g, r, c = blocks.shape
    per = g // slabs
    eye = jnp.eye(per, dtype=bool)[None, :, None, :, None]
    bd = jnp.where(eye, blocks.reshape(slabs, per, r, 1, c), jnp.zeros((), blocks.dtype))
    return bd.reshape(slabs, per * r, per * c)


def _s5_kernel(u_ref, are_ref, aim_ref, bb_re, bb_im, cc_re, cc_im, dsk_ref, gw_ref, gb_ref, o_ref,
               xr0, xr1, xi0, xi1, carry_ref, *, ts, slabs):
    @pl.when(pl.program_id(1) == 0)
    def _():
        carry_ref[...] = jnp.zeros_like(carry_ref)

    xr, xi = (xr0, xr1), (xi0, xi1)
    per_half = TILE_ROWS // 2
    u = u_ref[...]
    ub = u.astype(BF16)
    for j in range(slabs):
        uj = ub[:, LANES * j:LANES * (j + 1)]
        bur = jnp.dot(uj, bb_re[j], preferred_element_type=F32)
        bui = jnp.dot(uj, bb_im[j], preferred_element_type=F32)
        half, base = j // 2, (j % 2) * per_half
        for s in range(per_half):
            xr[half][pl.ds(base + s, ts, stride=TILE_ROWS), :] = bur[:, LANES * s:LANES * (s + 1)]
            xi[half][pl.ds(base + s, ts, stride=TILE_ROWS), :] = bui[:, LANES * s:LANES * (s + 1)]
    a_r = [are_ref[TILE_ROWS * h:TILE_ROWS * (h + 1), :] for h in range(2)]
    a_i = [aim_ref[TILE_ROWS * h:TILE_ROWS * (h + 1), :] for h in range(2)]

    def steps(tb, state):
        state = list(state)
        for tt in range(TILE_ROWS):
            row = pl.multiple_of((tb * TILE_ROWS + tt) * TILE_ROWS, TILE_ROWS)
            for h in range(2):
                sr, si = state[h], state[2 + h]
                nr = a_r[h] * sr - a_i[h] * si + xr[h][pl.ds(row, TILE_ROWS), :]
                ni = a_r[h] * si + a_i[h] * sr + xi[h][pl.ds(row, TILE_ROWS), :]
                xr[h][pl.ds(row, TILE_ROWS), :] = nr
                xi[h][pl.ds(row, TILE_ROWS), :] = ni
                state[h], state[2 + h] = nr, ni
        return tuple(state)

    final = lax.fori_loop(0, ts // TILE_ROWS, steps, tuple(carry_ref[i] for i in range(4)))
    for i in range(4):
        carry_ref[i] = final[i]
    ys = []
    for j in range(slabs):
        half, base = j // 2, (j % 2) * per_half
        gather = lambda ref: jnp.concatenate(
            [ref[pl.ds(base + s, ts, stride=TILE_ROWS), :] for s in range(per_half)], axis=1).astype(BF16)
        yj = jnp.dot(gather(xr[half]), cc_re[j], preferred_element_type=F32)
        ys.append(yj - jnp.dot(gather(xi[half]), cc_im[j], preferred_element_type=F32))
    y = jnp.concatenate(ys, axis=1) + dsk_ref[...] * u
    y = 0.5 * y * (1.0 + jnp.tanh(math.sqrt(2.0 / math.pi) * (y + 0.044715 * (y * y * y))))
    gate = jnp.dot(y.astype(BF16), gw_ref[...], preferred_element_type=F32) + gb_ref[...]
    o_ref[...] = (y * jax.nn.sigmoid(gate)).astype(o_ref.dtype)


def _s5_mixer(u, lam_re, lam_im, log_dt, b_re, b_im, c_re, c_im, d_skip, glu_w, glu_b, bsz, seq):
    t, width = u.shape
    groups, states = lam_re.shape
    n = groups * states
    slabs = width // LANES
    ts = S5_TILE
    assert n == 2 * TILE_ROWS * LANES and slabs == 4
    flat = lambda a: a.reshape(1, n)
    ldt = jnp.repeat(log_dt, states).reshape(1, n)
    bre_bd = _block_diag_slabs(jnp.swapaxes(b_re, 1, 2), slabs)
    bim_bd = _block_diag_slabs(jnp.swapaxes(b_im, 1, 2), slabs)
    cre_bd = _block_diag_slabs(jnp.swapaxes(c_re, 1, 2), slabs).astype(BF16)
    cim_bd = _block_diag_slabs(jnp.swapaxes(c_im, 1, 2), slabs).astype(BF16)
    full = lambda a: pl.BlockSpec(a.shape, lambda *_: (0,) * a.ndim)
    prep_in = (flat(lam_re), flat(lam_im), ldt, bre_bd, bim_bd)
    a_re, a_im, bb_re, bb_im = pl.pallas_call(
        _s5_prep_kernel,
        grid=(1,),
        in_specs=[full(a) for a in prep_in],
        out_specs=[pl.BlockSpec((1, n), lambda i: (0, 0))] * 2 + [pl.BlockSpec(bre_bd.shape, lambda i: (0, 0, 0))] * 2,
        out_shape=[jax.ShapeDtypeStruct((1, n), F32)] * 2 + [jax.ShapeDtypeStruct(bre_bd.shape, BF16)] * 2,
        compiler_params=_cparams(("arbitrary",)),
        name="s5_prep",
    )(*prep_in)
    nt = seq // ts
    tiles = lambda a: a.reshape(n // LANES, LANES)
    consts = (tiles(a_re), tiles(a_im), bb_re, bb_im, cre_bd, cim_bd, d_skip.reshape(1, width),
              glu_w.astype(BF16), glu_b.reshape(1, width))
    return pl.pallas_call(
        functools.partial(_s5_kernel, ts=ts, slabs=slabs),
        grid=(bsz, nt),
        in_specs=[pl.BlockSpec((ts, width), lambda b, j: (b * nt + j, 0))] + [full(a) for a in consts],
        out_specs=pl.BlockSpec((ts, width), lambda b, j: (b * nt + j, 0)),
        out_shape=jax.ShapeDtypeStruct((t, width), BF16),
        scratch_shapes=[pltpu.VMEM((ts * TILE_ROWS, LANES), F32)] * 4 + [pltpu.VMEM((4, TILE_ROWS, LANES), F32)],
        compiler_params=_cparams(("parallel", "arbitrary")),
        name="s5_mixer",
    )(u, *consts)


def kernel(x, c, mod_w, mod_b, ln_g, ln_b, even_w_in, pool_w, pool_scale, rel_bias, even_w_out, odd_w_in, forget_b, ssm_lam_re, ssm_lam_im, ssm_log_dt, ssm_b_re, ssm_b_im, ssm_c_re, ssm_c_im, ssm_d, ssm_glu_w, ssm_glu_b, odd_w_out, router_w, router_b, exp_w_gu, exp_b_gu, exp_w_down, exp_b_down):
    bsz, seq, d = x.shape
    depth = mod_w.shape[0]
    alpha = (2.0 * depth) ** 0.25
    mods = _ada_mods(c, mod_w.reshape(depth * 2, d, 3 * d), mod_b.reshape(depth * 2, 3 * d))
    xt = x.reshape(bsz * seq, d)
    for layer in range(depth):
        i = layer // 2
        if layer % 2 == 0:
            w_in = even_w_in[i]
            width = w_in.shape[1] // 4
            zp, q, k, vt = _inproj_even(xt, mods, 2 * layer, w_in[:, :3 * width].astype(BF16),
                                        w_in[:, 3 * width:].T.astype(BF16), seq)
            ya = _pool_mixer(zp, pool_w[i].astype(BF16), pool_scale[i], seq)
            yb = _chunk_attention(q, k, vt, _cattn_bias_table(rel_bias[i]), bsz, seq)
            w_out = even_w_out[i]
        else:
            width = ssm_d.shape[1] * ssm_d.shape[2]
            w_in = odd_w_in[i]
            n_heads = forget_b.shape[1]
            wf = jnp.zeros((d, LANES), BF16).at[:, :n_heads].set(w_in[:, 4 * width:].astype(BF16))
            fb = jnp.zeros((1, LANES), F32).at[0, :n_heads].set(forget_b[i])
            zs, q, k, vt, flog = _inproj_odd(xt, mods, 2 * layer, w_in[:, :3 * width].astype(BF16),
                                             w_in[:, 3 * width:4 * width].T.astype(BF16), wf, fb, seq)
            ya = _s5_mixer(zs, ssm_lam_re[i], ssm_lam_im[i], ssm_log_dt[i], ssm_b_re[i], ssm_b_im[i],
                           ssm_c_re[i], ssm_c_im[i], ssm_d[i], ssm_glu_w[i], ssm_glu_b[i], bsz, seq)
            yb = _forgetting_attention(q, _forget_keys(flog, k, bsz, seq), vt, bsz, seq)
            w_out = odd_w_out[i]
        xt = _outproj_norm(ya, yb, xt, mods, 2 * layer, w_out.astype(BF16), ln_g[layer, 0], ln_b[layer, 0],
                           seq, alpha)
        xt = _moe_block(xt, mods, 2 * layer + 1, router_w[layer], router_b[layer], layer, exp_w_gu, exp_b_gu,
                        exp_w_down, exp_b_down, ln_g[layer, 1], ln_b[layer, 1], seq, alpha)
    return xt.reshape(bsz, seq, d)
```

```python
import functools
import math

import jax
import jax.numpy as jnp
from jax import lax
from jax.experimental import pallas as pl
from jax.experimental.pallas import tpu as pltpu

F32 = jnp.float32
BF16 = jnp.bfloat16
I32 = jnp.int32
HIGHEST = lax.Precision.HIGHEST

LANES = 128
SUBLANES = 8
VMEM_LIMIT_BYTES = 60000 * 1024

HEAD_DIM = 64
CHUNK = 64
LEFT_CHUNKS = 8
REL_CLIP = 128
POOL_WINDOWS = (2, 4, 8, 16)
POOL_HALO = 16
SSM_GROUP_DIM = 16
SSM_STATE = 64
N_EXPERTS = 32
TOP_K = 4
SWIGLU_LIMIT = 7.0
SWIGLU_ALPHA = 1.702
LN_EPS = 1e-5
NEG_INF = -1e30
LOG2E = math.log2(math.e)
FOX_BIAS_PARTS = 3

ROW_TILE = 512
CATTN_Q = 2 * CHUNK
CATTN_BAND = (LEFT_CHUNKS + 2) * CHUNK
FOX_TQ = 512
FOX_TK = 512
S5_TILE = 256
FCUM_TILE = 512
TILE_ROWS = SUBLANES
MOE_ROW_BLOCK = 256
MOE_TOK_TILE = 512
ROUTE_TILE = 512


def _cparams(sem):
    return pltpu.CompilerParams(dimension_semantics=sem, vmem_limit_bytes=VMEM_LIMIT_BYTES)


def _layer_norm(r, g, b):
    mu = jnp.mean(r, axis=-1, keepdims=True)
    d = r - mu
    var = jnp.mean(d * d, axis=-1, keepdims=True)
    return d * lax.rsqrt(var + LN_EPS) * g + b


def _mod_kernel(c_ref, w_ref, b_ref, o_ref):
    c = c_ref[...]
    s = c * jax.nn.sigmoid(c)
    o_ref[0] = jnp.dot(s, w_ref[0], precision=HIGHEST, preferred_element_type=F32) + b_ref[0]


def _ada_mods(c, mod_w, mod_b):
    bsz, d = c.shape
    m = mod_w.shape[0]
    rows = 8
    c_pad = jnp.zeros((rows, d), F32).at[:bsz].set(c)
    out = pl.pallas_call(
        _mod_kernel,
        grid=(m, 3),
        in_specs=[
            pl.BlockSpec((rows, d), lambda i, j: (0, 0)),
            pl.BlockSpec((1, d, d), lambda i, j: (i, 0, j)),
            pl.BlockSpec((1, 1, d), lambda i, j: (i, 0, j)),
        ],
        out_specs=pl.BlockSpec((1, rows, d), lambda i, j: (i, 0, j)),
        out_shape=jax.ShapeDtypeStruct((m, rows, 3 * d), F32),
        compiler_params=_cparams(("parallel", "parallel")),
        name="ada_mod",
    )(c_pad, mod_w, mod_b.reshape(m, 1, 3 * d))
    return out[:, :bsz].reshape(m, bsz, 3, d)


def _mod_spec(midx, tiles_per_batch, d):
    return pl.BlockSpec((None, None, 3, d), lambda i: (midx, i // tiles_per_batch, 0, 0))


def _inproj_even_kernel(x_ref, mod_ref, w_ref, wvt_ref, zp_ref, q_ref, k_ref, vt_ref, *, width):
    h = (x_ref[...] * (1.0 + mod_ref[1:2, :]) + mod_ref[0:1, :]).astype(BF16)
    z = jnp.dot(h, w_ref[...], preferred_element_type=F32)
    zp_ref[...] = z[:, :width]
    q_ref[...] = (z[:, width:2 * width] * (HEAD_DIM ** -0.5 * LOG2E)).astype(BF16)
    k_ref[...] = z[:, 2 * width:3 * width].astype(BF16)
    vt_ref[...] = lax.dot_general(wvt_ref[...], h, (((1,), (1,)), ((), ())),
                                  preferred_element_type=F32).astype(BF16)


def _inproj_even(x, mods, midx, w_main_bf, wvt_bf, seq):
    t, d = x.shape
    width = wvt_bf.shape[0]
    tm = ROW_TILE
    row = lambda i: (i, 0)
    return pl.pallas_call(
        functools.partial(_inproj_even_kernel, width=width),
        grid=(t // tm,),
        in_specs=[
            pl.BlockSpec((tm, d), row),
            _mod_spec(midx, seq // tm, d),
            pl.BlockSpec((d, 3 * width), lambda i: (0, 0)),
            pl.BlockSpec((width, d), lambda i: (0, 0)),
        ],
        out_specs=[pl.BlockSpec((tm, width), row)] * 3
        + [pl.BlockSpec((None, width, tm), lambda i: (i // (seq // tm), 0, i % (seq // tm)))],
        out_shape=[jax.ShapeDtypeStruct((t, width), F32)] + [jax.ShapeDtypeStruct((t, width), BF16)] * 2
        + [jax.ShapeDtypeStruct((t // seq, width, seq), BF16)],
        compiler_params=_cparams(("parallel",)),
        name="inproj_even",
    )(x, mods, w_main_bf, wvt_bf)


def _pool_kernel(zp_ref, halo_ref, w_ref, sc_ref, o_ref, xs_ref, *, ts, seq):
    i = pl.program_id(0)
    t0 = lax.rem(i * ts, seq)
    xs_ref[0:POOL_HALO, :] = jnp.where(t0 == 0, 0.0, halo_ref[...])
    xs_ref[POOL_HALO:POOL_HALO + ts, :] = zp_ref[...]
    pos = lax.broadcasted_iota(I32, (ts, LANES), 0) + t0
    for g, win in enumerate(POOL_WINDOWS):
        cols = slice(LANES * g, LANES * (g + 1))
        u = xs_ref[POOL_HALO:POOL_HALO + ts, cols]
        acc = u
        for k in range(1, win):
            acc = acc + xs_ref[POOL_HALO - k:POOL_HALO - k + ts, cols]
        cnt = jnp.minimum(pos + 1, win).astype(F32)
        dlt = acc / cnt - u
        y = jnp.dot(dlt.astype(BF16), w_ref[g], preferred_element_type=F32)
        o_ref[:, cols] = (y * sc_ref[:, cols]).astype(o_ref.dtype)


def _pool_mixer(zp, w_pool_bf, pool_scale, seq):
    t, width = zp.shape
    ts = ROW_TILE
    hb = ts // POOL_HALO
    return pl.pallas_call(
        functools.partial(_pool_kernel, ts=ts, seq=seq),
        grid=(t // ts,),
        in_specs=[
            pl.BlockSpec((ts, width), lambda i: (i, 0)),
            pl.BlockSpec((POOL_HALO, width), lambda i: (jnp.maximum(i * hb - 1, 0), 0)),
            pl.BlockSpec(w_pool_bf.shape, lambda i: (0, 0, 0)),
            pl.BlockSpec((1, width), lambda i: (0, 0)),
        ],
        out_specs=pl.BlockSpec((ts, width), lambda i: (i, 0)),
        out_shape=jax.ShapeDtypeStruct((t, width), BF16),
        scratch_shapes=[pltpu.VMEM((ts + POOL_HALO, width), F32)],
        compiler_params=_cparams(("parallel",)),
        name="pool_mixer",
    )(zp, zp, w_pool_bf, pool_scale.reshape(1, width))


def _cbias_kernel(r_ref, o_ref):
    base = jnp.broadcast_to(r_ref[...], (CATTN_Q, r_ref.shape[-1]))
    rolled = pltpu.roll(base, 0, 1, stride=1, stride_axis=0)[:, :CATTN_BAND]
    qi = lax.broadcasted_iota(I32, (CATTN_Q, CATTN_BAND), 0)
    kp = lax.broadcasted_iota(I32, (CATTN_Q, CATTN_BAND), 1)
    lo = (qi // CHUNK) * CHUNK
    valid = (kp >= lo) & (kp < lo + (LEFT_CHUNKS + 1) * CHUNK)
    o_ref[...] = jnp.where(valid, rolled * LOG2E, NEG_INF).T


def _cattn_bias_table(rel_bias):
    h = rel_bias.shape[0]
    wide = 1024
    far = LEFT_CHUNKS * CHUNK - REL_CLIP + 1
    tail = wide - far - (2 * REL_CLIP - 1)
    last = rel_bias[:, 2 * REL_CLIP:]
    base = jnp.concatenate(
        [jnp.broadcast_to(last, (h, far)), rel_bias[:, 1:2 * REL_CLIP][:, ::-1], jnp.broadcast_to(last, (h, tail))],
        axis=1).reshape(h, 1, wide)
    return pl.pallas_call(
        _cbias_kernel,
        grid=(h,),
        in_specs=[pl.BlockSpec((None, 1, wide), lambda i: (i, 0, 0))],
        out_specs=pl.BlockSpec((None, CATTN_BAND, CATTN_Q), lambda i: (i, 0, 0)),
        out_shape=jax.ShapeDtypeStruct((h, CATTN_BAND, CATTN_Q), F32),
        compiler_params=_cparams(("parallel",)),
        name="cattn_bias",
    )(base)


def _cattn_kernel(q_ref, k_ref, vt_ref, bias_ref, o_ref, st_ref, e_ref, l_ref, *, n_pairs):
    p = pl.program_id(1)
    start = pl.multiple_of(p * CATTN_Q, CATTN_Q)
    kpos = lax.broadcasted_iota(I32, (CATTN_BAND, CATTN_Q), 0) + start
    real_key = kpos >= LEFT_CHUNKS * CHUNK
    lane = lax.broadcasted_iota(I32, (CATTN_Q, LANES), 1)
    n_heads = 2 * n_pairs
    for h in range(n_heads):
        cols = slice(LANES * (h // 2), LANES * (h // 2 + 1))
        qp = q_ref[:, cols]
        mine = (lane < HEAD_DIM) if h % 2 == 0 else (lane >= HEAD_DIM)
        qm = jnp.where(mine, qp, jnp.zeros_like(qp))
        st_ref[h] = lax.dot_general(k_ref[pl.ds(start, CATTN_BAND), cols], qm, (((1,), (1,)), ((), ())),
                                    preferred_element_type=F32)
    for h in range(n_heads):
        st = jnp.where(real_key, st_ref[h] + bias_ref[h], NEG_INF)
        m = jnp.max(st, axis=0, keepdims=True)
        e = jnp.exp2(st - m)
        l_ref[h:h + 1, :] = jnp.sum(e, axis=0, keepdims=True)
        e_ref[h] = e.astype(BF16)
    for hp in range(n_pairs):
        outs = []
        for hh in range(2):
            h = 2 * hp + hh
            vth = vt_ref[HEAD_DIM * h:HEAD_DIM * (h + 1), pl.ds(start, CATTN_BAND)]
            outs.append(jnp.dot(vth, e_ref[h], preferred_element_type=F32) / l_ref[h:h + 1, :])
        o_ref[:, LANES * hp:LANES * (hp + 1)] = jnp.concatenate(outs, axis=0).T.astype(o_ref.dtype)


def _chunk_attention(q, k, vt, bias_t, bsz, seq):
    t, width = q.shape
    n_pairs = width // LANES
    pad = LEFT_CHUNKS * CHUNK
    kp = jnp.pad(k.reshape(bsz, seq, width), ((0, 0), (pad, 0), (0, 0)))
    vtp = jnp.pad(vt, ((0, 0), (0, 0), (pad, 0)))
    nq = seq // CATTN_Q
    return pl.pallas_call(
        functools.partial(_cattn_kernel, n_pairs=n_pairs),
        grid=(bsz, nq),
        in_specs=[
            pl.BlockSpec((CATTN_Q, width), lambda b, p: (b * nq + p, 0)),
            pl.BlockSpec((None, seq + pad, width), lambda b, p: (b, 0, 0)),
            pl.BlockSpec((None, width, seq + pad), lambda b, p: (b, 0, 0)),
            pl.BlockSpec(bias_t.shape, lambda b, p: (0, 0, 0)),
        ],
        out_specs=pl.BlockSpec((CATTN_Q, width), lambda b, p: (b * nq + p, 0)),
        out_shape=jax.ShapeDtypeStruct((t, width), BF16),
        scratch_shapes=[pltpu.VMEM((2 * n_pairs, CATTN_BAND, CATTN_Q), F32),
                        pltpu.VMEM((2 * n_pairs, CATTN_BAND, CATTN_Q), BF16),
                        pltpu.VMEM((2 * n_pairs, CATTN_Q), F32)],
        compiler_params=_cparams(("parallel", "arbitrary")),
        name="chunk_attention",
    )(q, kp, vtp, bias_t)


def _outproj_kernel(ya_ref, yb_ref, x_ref, mod_ref, w_ref, g_ref, b_ref, o_ref, *, alpha, half):
    y = jnp.dot(ya_ref[...], w_ref[0:half, :], preferred_element_type=F32)
    y = y + jnp.dot(yb_ref[...], w_ref[half:2 * half, :], preferred_element_type=F32)
    r = alpha * x_ref[...] + (1.0 + mod_ref[2:3, :]) * y
    o_ref[...] = _layer_norm(r, g_ref[...], b_ref[...])


def _outproj_norm(ya, yb, x, mods, midx, w_out_bf, ln_g, ln_b, seq, alpha):
    t, d = x.shape
    half = ya.shape[1]
    tm = ROW_TILE
    row = lambda i: (i, 0)
    return pl.pallas_call(
        functools.partial(_outproj_kernel, alpha=alpha, half=half),
        grid=(t // tm,),
        in_specs=[
            pl.BlockSpec((tm, half), row),
            pl.BlockSpec((tm, half), row),
            pl.BlockSpec((tm, d), row),
            _mod_spec(midx, seq // tm, d),
            pl.BlockSpec((2 * half, d), lambda i: (0, 0)),
            pl.BlockSpec((1, d), lambda i: (0, 0)),
            pl.BlockSpec((1, d), lambda i: (0, 0)),
        ],
        out_specs=pl.BlockSpec((tm, d), row),
        out_shape=jax.ShapeDtypeStruct((t, d), F32),
        compiler_params=_cparams(("parallel",)),
        name="outproj_norm",
    )(ya, yb, x, mods, w_out_bf, ln_g.reshape(1, d), ln_b.reshape(1, d))


def _route_kernel(x_ref, mod_ref, rw_ref, rb_ref, h_ref, e_ref, r_ref, g_ref, c_ref, carry_ref, *, tm):
    @pl.when(pl.program_id(0) == 0)
    def _():
        carry_ref[...] = jnp.zeros_like(carry_ref)

    h = x_ref[...] * (1.0 + mod_ref[1:2, :]) + mod_ref[0:1, :]
    _store_token_tiles(h_ref, h)
    logits = lax.dot_general(rw_ref[...], h, (((1,), (1,)), ((), ())), precision=HIGHEST,
                             preferred_element_type=F32) + rb_ref[...]
    eio = lax.broadcasted_iota(I32, (N_EXPERTS, tm), 0)
    vals, hots = [], []
    for k in range(TOP_K):
        m = jnp.max(logits, axis=0, keepdims=True)
        idx = jnp.min(jnp.where(logits == m, eio, N_EXPERTS), axis=0, keepdims=True)
        hot = eio == idx
        e_ref[k:k + 1, :] = idx
        vals.append(m)
        hots.append(hot)
        logits = jnp.where(hot, -jnp.inf, logits)
    exps = [jnp.exp(v - vals[0]) for v in vals]
    denom = exps[0] + exps[1] + exps[2] + exps[3]
    for k in range(TOP_K):
        g_ref[k:k + 1, :] = exps[k] / denom
    cnt = jnp.zeros((N_EXPERTS, tm), F32)
    for hot in hots:
        cnt = cnt + hot.astype(F32)
    before = (lax.broadcasted_iota(I32, (tm, tm), 0) < lax.broadcasted_iota(I32, (tm, tm), 1))
    prefix = jnp.dot(cnt.astype(BF16), before.astype(BF16), preferred_element_type=F32)
    tot = prefix + carry_ref[...]
    for k in range(TOP_K):
        r_ref[k:k + 1, :] = jnp.sum(jnp.where(hots[k], tot, 0.0), axis=0, keepdims=True).astype(I32)
    new_carry = carry_ref[...] + jnp.sum(cnt, axis=1, keepdims=True)
    carry_ref[...] = new_carry
    c_ref[...] = jnp.broadcast_to(new_carry, c_ref.shape)


def _route(x, mods, midx, router_w, router_b, seq):
    t, d = x.shape
    tm = ROUTE_TILE
    tok = lambda i: (0, i)
    return pl.pallas_call(
        functools.partial(_route_kernel, tm=tm),
        grid=(t // tm,),
        in_specs=[
            pl.BlockSpec((tm, d), lambda i: (i, 0)),
            _mod_spec(midx, seq // tm, d),
            pl.BlockSpec((N_EXPERTS, d), lambda i: (0, 0)),
            pl.BlockSpec((N_EXPERTS, 1), lambda i: (0, 0)),
        ],
        out_specs=[
            pl.BlockSpec((tm * TILE_ROWS, LANES), lambda i: (i, 0)),
            pl.BlockSpec((TOP_K, tm), tok),
            pl.BlockSpec((TOP_K, tm), tok),
            pl.BlockSpec((TOP_K, tm), tok),
            pl.BlockSpec((N_EXPERTS, LANES), lambda i: (0, 0)),
        ],
        out_shape=[
            jax.ShapeDtypeStruct((t * TILE_ROWS, LANES), F32),
            jax.ShapeDtypeStruct((TOP_K, t), I32),
            jax.ShapeDtypeStruct((TOP_K, t), I32),
            jax.ShapeDtypeStruct((TOP_K, t), F32),
            jax.ShapeDtypeStruct((N_EXPERTS, LANES), F32),
        ],
        scratch_shapes=[pltpu.VMEM((N_EXPERTS, 1), F32)],
        compiler_params=_cparams(("arbitrary",)),
        name="moe_route",
    )(x, mods, router_w.T, router_b.reshape(N_EXPERTS, 1))


def _plan_kernel(e_ref, r_ref, ccol_ref, crow_ref, d_ref, blk_ref, exp_ref, nu_ref, *, tm, rb, nbp):
    inv = 1.0 / rb
    pad_col = jnp.floor((ccol_ref[...] + (rb - 1)) * inv) * rb
    pad_row = jnp.floor((crow_ref[0:1, :] + (rb - 1)) * inv) * rb
    ei = lax.broadcasted_iota(I32, (N_EXPERTS, LANES), 0)
    li = lax.broadcasted_iota(I32, (N_EXPERTS, LANES), 1)
    pstart = jnp.sum(jnp.where(li < ei, jnp.broadcast_to(pad_row, (N_EXPERTS, LANES)), 0.0),
                     axis=1, keepdims=True)
    padded = pad_col[:, 0:1]
    pend = pstart + padded
    eio = lax.broadcasted_iota(I32, (N_EXPERTS, tm), 0)
    for k in range(TOP_K):
        base = jnp.sum(jnp.where(eio == e_ref[k:k + 1, :], pstart, 0.0), axis=0, keepdims=True)
        d_ref[k:k + 1, :] = base.astype(I32) + r_ref[k:k + 1, :]
    row0 = (lax.broadcasted_iota(I32, (N_EXPERTS, nbp), 1) * rb).astype(F32)
    be = jnp.minimum(jnp.sum((pend <= row0).astype(F32), axis=0, keepdims=True), N_EXPERTS - 1.0)
    ebi = lax.broadcasted_iota(I32, (N_EXPERTS, nbp), 0).astype(F32)
    mine = ebi == be
    my_end = jnp.sum(jnp.where(mine, pend, 0.0), axis=0, keepdims=True)
    nxt = jnp.sum((pend <= my_end).astype(F32), axis=0, keepdims=True)
    order = jnp.sum(jnp.where(jnp.logical_and(padded > 0.0, ebi < be), 1.0, 0.0), axis=0, keepdims=True)
    blk_ref[0:1, :] = be.astype(I32)
    blk_ref[1:2, :] = jnp.where(nxt < N_EXPERTS, nxt, -1.0).astype(I32)
    blk_ref[2:3, :] = (order - 2.0 * jnp.floor(order * 0.5)).astype(I32)
    blk_ref[3:8, :] = jnp.zeros((5, nbp), I32)
    pstart_row = jnp.sum(jnp.where(ei < li, pad_col, 0.0), axis=0, keepdims=True)
    exp_ref[0:1, :] = crow_ref[0:1, :].astype(I32)
    exp_ref[1:2, :] = pstart_row.astype(I32)
    exp_ref[2:3, :] = pad_row.astype(I32)
    exp_ref[3:8, :] = jnp.zeros((5, LANES), I32)
    used = jnp.sum(padded, axis=0, keepdims=True) * inv
    nu_ref[...] = jnp.broadcast_to(used, nu_ref.shape).astype(I32)


def _plan(eidx, rank, counts_col, rb, nb):
    t = eidx.shape[1]
    tm = ROUTE_TILE
    nbp = -(-nb // LANES) * LANES
    counts_row = jnp.zeros((8, LANES), F32).at[:, :N_EXPERTS].set(counts_col[:, 0][None, :])
    tok = lambda i: (0, i)
    dest, blk, exp, nu = pl.pallas_call(
        functools.partial(_plan_kernel, tm=tm, rb=rb, nbp=nbp),
        grid=(t // tm,),
        in_specs=[
            pl.BlockSpec((TOP_K, tm), tok),
            pl.BlockSpec((TOP_K, tm), tok),
            pl.BlockSpec((N_EXPERTS, LANES), lambda i: (0, 0)),
            pl.BlockSpec((8, LANES), lambda i: (0, 0)),
        ],
        out_specs=[
            pl.BlockSpec((TOP_K, tm), tok),
            pl.BlockSpec((8, nbp), lambda i: (0, 0)),
            pl.BlockSpec((8, LANES), lambda i: (0, 0)),
            pl.BlockSpec((1, LANES), lambda i: (0, 0)),
        ],
        out_shape=[
            jax.ShapeDtypeStruct((TOP_K, t), I32),
            jax.ShapeDtypeStruct((8, nbp), I32),
            jax.ShapeDtypeStruct((8, LANES), I32),
            jax.ShapeDtypeStruct((1, LANES), I32),
        ],
        compiler_params=_cparams(("arbitrary",)),
        name="moe_plan",
    )(eidx, rank, counts_col, counts_row)
    return dest, (blk[0], blk[1], blk[2]), (exp[0], exp[1], exp[2]), nu[0, :1]


def _store_token_tiles(ref, x):
    n = x.shape[0]
    for s in range(TILE_ROWS):
        ref[pl.ds(s, n, stride=TILE_ROWS), :] = x[:, LANES * s:LANES * (s + 1)]


def _load_token_tiles(ref, n):
    return jnp.concatenate([ref[pl.ds(s, n, stride=TILE_ROWS), :] for s in range(TILE_ROWS)], axis=1)


def _tile_copy(src_ref, src_tok, dst_ref, dst_tok, sem):
    src = src_ref.at[pl.ds(pl.multiple_of(src_tok * TILE_ROWS, TILE_ROWS), TILE_ROWS)]
    dst = dst_ref.at[pl.ds(pl.multiple_of(dst_tok * TILE_ROWS, TILE_ROWS), TILE_ROWS)]
    return pltpu.make_async_copy(src, dst, sem)


def _dispatch_kernel(cnt_ref, first_ref, padded_ref, h_ref, dest_ref, xout_ref, zero_ref, sem, zsem, *, tm, n_rows):
    i = pl.program_id(0)
    steps = pl.num_programs(0)

    def fill_range(e):
        lo = first_ref[e] + cnt_ref[e]
        hi = jnp.where(e == N_EXPERTS - 1, n_rows, first_ref[e] + padded_ref[e])
        return lo, hi

    @pl.when(i == 0)
    def _():
        zero_ref[...] = jnp.zeros_like(zero_ref)

    def start_fill(j, c):
        lo, hi = fill_range(i + j * steps)

        def fill(r, c2):
            _tile_copy(zero_ref, 0, xout_ref, r, zsem).start()
            return c2

        lax.fori_loop(lo, hi, fill, 0)
        return c

    lax.fori_loop(0, (N_EXPERTS - i + steps - 1) // steps, start_fill, 0)

    def issue(r, c):
        for k in range(TOP_K):
            _tile_copy(h_ref, r, xout_ref, dest_ref[k, r], sem).start(priority=k % 2)
        return c

    lax.fori_loop(0, tm, issue, 0)

    def drain(r, c):
        for k in range(TOP_K):
            _tile_copy(h_ref, 0, xout_ref, 0, sem).wait()
        return c

    lax.fori_loop(0, tm, drain, 0)

    @pl.when(i == steps - 1)
    def _():
        def wait_fill(e, c):
            lo, hi = fill_range(e)

            def fill_done(r, c2):
                _tile_copy(zero_ref, 0, xout_ref, 0, zsem).wait()
                return c2

            lax.fori_loop(lo, hi, fill_done, 0)
            return c

        lax.fori_loop(0, N_EXPERTS, wait_fill, 0)


def _dispatch(h_tiles, dest, expert_table, n_rows):
    t = h_tiles.shape[0] // TILE_ROWS
    tm = MOE_TOK_TILE
    return pl.pallas_call(
        functools.partial(_dispatch_kernel, tm=tm, n_rows=n_rows),
        grid_spec=pltpu.PrefetchScalarGridSpec(
            num_scalar_prefetch=3,
            grid=(t // tm,),
            in_specs=[
                pl.BlockSpec((tm * TILE_ROWS, LANES), lambda i, *_: (i, 0)),
                pl.BlockSpec((TOP_K, tm), lambda i, *_: (0, i), memory_space=pltpu.SMEM),
            ],
            out_specs=pl.BlockSpec(memory_space=pl.ANY),
            scratch_shapes=[pltpu.VMEM((TILE_ROWS, LANES), h_tiles.dtype), pltpu.SemaphoreType.DMA,
                            pltpu.SemaphoreType.DMA],
        ),
        out_shape=jax.ShapeDtypeStruct((n_rows * TILE_ROWS, LANES), h_tiles.dtype),
        compiler_params=_cparams(("arbitrary",)),
        name="moe_dispatch",
    )(*expert_table, h_tiles, dest)


def _expert_kernel(be_ref, nx_ref, slot_ref, nu_ref, x_ref, bgu_ref, bd_ref, wgu_hbm, wd_hbm, o_ref,
                   wgu_f, wd_f, wgu_s, wd_s, sems, *, dff, rb, layer):
    b = pl.program_id(0)
    e = be_ref[b]
    slot = slot_ref[b]
    used = b < nu_ref[0]
    first = jnp.logical_and(used, jnp.logical_or(b == 0, e != be_ref[jnp.maximum(b - 1, 0)]))

    def fetch(expert, s):
        return (pltpu.make_async_copy(wgu_hbm.at[layer, expert], wgu_f.at[s], sems.at[0, s]),
                pltpu.make_async_copy(wd_hbm.at[layer, expert], wd_f.at[s], sems.at[1, s]))

    @pl.when(b == 0)
    def _():
        for copy in fetch(e, slot):
            copy.start()

    @pl.when(first)
    def _():
        for copy in fetch(e, slot):
            copy.wait()
        wgu_s[...] = wgu_f[slot].astype(BF16)
        wd_s[...] = wd_f[slot].astype(BF16)

        @pl.when(nx_ref[b] >= 0)
        def _():
            for copy in fetch(nx_ref[b], 1 - slot):
                copy.start()

    @pl.when(used)
    def _():
        x = _load_token_tiles(x_ref, rb)
        gu = jnp.dot(x.astype(BF16), wgu_s[...], preferred_element_type=F32) + bgu_ref[0]
        gate = jnp.minimum(gu[:, :dff], SWIGLU_LIMIT)
        up = jnp.clip(gu[:, dff:], -SWIGLU_LIMIT, SWIGLU_LIMIT)
        act = (up + 1.0) * (gate * jax.nn.sigmoid(SWIGLU_ALPHA * gate))
        _store_token_tiles(o_ref, jnp.dot(act.astype(BF16), wd_s[...], preferred_element_type=F32) + bd_ref[0])

    @pl.when(jnp.logical_not(used))
    def _():
        o_ref[...] = jnp.zeros_like(o_ref)


def _expert_ffn(x_disp, block_table, n_used, layer, w_gu, b_gu, w_down, b_down, rb):
    nl, ne, d, dff2 = w_gu.shape
    dff = dff2 // 2
    nb = x_disp.shape[0] // (rb * TILE_ROWS)
    pick = lambda b, be, *_: (layer, be[b], 0, 0)
    blocks = lambda b, *_: (b, 0)
    return pl.pallas_call(
        functools.partial(_expert_kernel, dff=dff, rb=rb, layer=layer),
        grid_spec=pltpu.PrefetchScalarGridSpec(
            num_scalar_prefetch=4,
            grid=(nb,),
            in_specs=[
                pl.BlockSpec((rb * TILE_ROWS, LANES), blocks),
                pl.BlockSpec((None, 1, 1, dff2), pick),
                pl.BlockSpec((None, 1, 1, d), pick),
                pl.BlockSpec(memory_space=pl.ANY),
                pl.BlockSpec(memory_space=pl.ANY),
            ],
            out_specs=pl.BlockSpec((rb * TILE_ROWS, LANES), blocks),
            scratch_shapes=[pltpu.VMEM((2, d, dff2), F32), pltpu.VMEM((2, dff, d), F32),
                            pltpu.VMEM((d, dff2), BF16), pltpu.VMEM((dff, d), BF16),
                            pltpu.SemaphoreType.DMA((2, 2))],
        ),
        out_shape=jax.ShapeDtypeStruct(x_disp.shape, F32),
        compiler_params=_cparams(("arbitrary",)),
        name="moe_experts",
    )(*block_table, n_used, x_disp, b_gu.reshape(nl, ne, 1, dff2), b_down.reshape(nl, ne, 1, d), w_gu, w_down)


def _combine_kernel(dest_ref, gates_ref, x_ref, mod_ref, g_ref, b_ref, y_hbm, o_ref, buf, sem, *, tm, alpha):
    def issue(r, c):
        for k in range(TOP_K):
            _tile_copy(y_hbm, dest_ref[k, r], buf.at[k], r, sem).start(priority=k % 2)
        return c

    lax.fori_loop(0, tm, issue, 0)

    def drain(r, c):
        for k in range(TOP_K):
            _tile_copy(y_hbm, 0, buf.at[k], 0, sem).wait()
        return c

    lax.fori_loop(0, tm, drain, 0)
    y = gates_ref[:, 0:1] * _load_token_tiles(buf.at[0], tm)
    for k in range(1, TOP_K):
        y = y + gates_ref[:, k:k + 1] * _load_token_tiles(buf.at[k], tm)
    r = alpha * x_ref[...] + (1.0 + mod_ref[2:3, :]) * y
    o_ref[...] = _layer_norm(r, g_ref[...], b_ref[...])


def _combine_norm(y_disp, dest, gates_col, x, mods, midx, ln_g, ln_b, seq, alpha):
    t, d = x.shape
    tm = MOE_TOK_TILE
    row = lambda i: (i, 0)
    return pl.pallas_call(
        functools.partial(_combine_kernel, tm=tm, alpha=alpha),
        grid=(t // tm,),
        in_specs=[
            pl.BlockSpec((TOP_K, tm), lambda i: (0, i), memory_space=pltpu.SMEM),
            pl.BlockSpec((tm, TOP_K), row),
            pl.BlockSpec((tm, d), row),
            _mod_spec(midx, seq // tm, d),
            pl.BlockSpec((1, d), lambda i: (0, 0)),
            pl.BlockSpec((1, d), lambda i: (0, 0)),
            pl.BlockSpec(memory_space=pl.ANY),
        ],
        out_specs=pl.BlockSpec((tm, d), row),
        out_shape=jax.ShapeDtypeStruct((t, d), F32),
        scratch_shapes=[pltpu.VMEM((TOP_K, tm * TILE_ROWS, LANES), F32), pltpu.SemaphoreType.DMA],
        compiler_params=_cparams(("arbitrary",)),
        name="moe_combine",
    )(dest, gates_col, x, mods, ln_g.reshape(1, d), ln_b.reshape(1, d), y_disp)


def _moe_block(x, mods, midx, router_w, router_b, layer, w_gu, b_gu, w_down, b_down, ln_g, ln_b, seq, alpha):
    t, d = x.shape
    assert d == TILE_ROWS * LANES
    rb = MOE_ROW_BLOCK
    nb = (t * TOP_K + N_EXPERTS * (rb - 1) + rb - 1) // rb
    h, eidx, rank, gates, counts = _route(x, mods, midx, router_w, router_b, seq)
    dest, block_table, expert_table, n_used = _plan(eidx, rank, counts, rb, nb)
    x_disp = _dispatch(h, dest, expert_table, nb * rb)
    y_disp = _expert_ffn(x_disp, block_table, n_used, layer, w_gu, b_gu, w_down, b_down, rb)
    return _combine_norm(y_disp, dest, gates.T, x, mods, midx, ln_g, ln_b, seq, alpha)


def _inproj_odd_kernel(x_ref, mod_ref, w_ref, wvt_ref, wf_ref, fb_ref, zs_ref, q_ref, k_ref, vt_ref, f_ref,
                       *, width):
    h = (x_ref[...] * (1.0 + mod_ref[1:2, :]) + mod_ref[0:1, :]).astype(BF16)
    z = jnp.dot(h, w_ref[...], preferred_element_type=F32)
    zs_ref[...] = z[:, :width]
    q_ref[...] = (z[:, width:2 * width] * (HEAD_DIM ** -0.5 * LOG2E)).astype(BF16)
    k_ref[...] = z[:, 2 * width:3 * width].astype(BF16)
    vt_ref[...] = lax.dot_general(wvt_ref[...], h, (((1,), (1,)), ((), ())),
                                  preferred_element_type=F32).astype(BF16)
    f_ref[...] = jnp.dot(h, wf_ref[...], preferred_element_type=F32) + fb_ref[...]


def _inproj_odd(x, mods, midx, w_main_bf, wvt_bf, wf_bf, fb, seq):
    t, d = x.shape
    width = wvt_bf.shape[0]
    tm = ROW_TILE
    row = lambda i: (i, 0)
    const = lambda i: (0, 0)
    return pl.pallas_call(
        functools.partial(_inproj_odd_kernel, width=width),
        grid=(t // tm,),
        in_specs=[
            pl.BlockSpec((tm, d), row),
            _mod_spec(midx, seq // tm, d),
            pl.BlockSpec((d, 3 * width), const),
            pl.BlockSpec((width, d), const),
            pl.BlockSpec((d, LANES), const),
            pl.BlockSpec((1, LANES), const),
        ],
        out_specs=[pl.BlockSpec((tm, width), row)] * 3 + [pl.BlockSpec((width, tm), lambda i: (0, i)),
                                                          pl.BlockSpec((tm, LANES), row)],
        out_shape=[jax.ShapeDtypeStruct((t, width), F32)] + [jax.ShapeDtypeStruct((t, width), BF16)] * 2
        + [jax.ShapeDtypeStruct((width, t), BF16), jax.ShapeDtypeStruct((t, LANES), F32)],
        compiler_params=_cparams(("parallel",)),
        name="inproj_odd",
    )(x, mods, w_main_bf, wvt_bf, wf_bf, fb)


def _fkeys_kernel(f_ref, k_ref, sel_ref, o_ref, carry_ref, *, ts, n_pairs):
    @pl.when(pl.program_id(1) == 0)
    def _():
        carry_ref[...] = jnp.zeros_like(carry_ref)

    x = f_ref[...]
    ls = jnp.minimum(x, 0.0) - jnp.log(1.0 + jnp.exp(-jnp.abs(x)))
    upto = (lax.broadcasted_iota(I32, (ts, ts), 1) <= lax.broadcasted_iota(I32, (ts, ts), 0)).astype(F32)
    cs = jnp.dot(upto, ls, precision=HIGHEST, preferred_element_type=F32) + carry_ref[...]
    carry_ref[...] = cs[ts - 1:ts, :]
    rest = -LOG2E * cs
    pieces = []
    for _ in range(FOX_BIAS_PARTS):
        piece = rest.astype(BF16)
        pieces.append(piece)
        rest = rest - piece.astype(F32)
    ext = jnp.dot(jnp.concatenate(pieces, axis=1), sel_ref[...], preferred_element_type=F32).astype(BF16)
    for p in range(n_pairs):
        o_ref[:, 2 * LANES * p:2 * LANES * p + LANES] = k_ref[:, LANES * p:LANES * (p + 1)]
        o_ref[:, 2 * LANES * p + LANES:2 * LANES * (p + 1)] = ext[:, LANES * p:LANES * (p + 1)]


def _forget_keys(flog, k, bsz, seq):
    t, width = k.shape
    n_pairs = width // LANES
    ts = FCUM_TILE
    nt = seq // ts
    sel = [[0.0] * width for _ in range(FOX_BIAS_PARTS * LANES)]
    for h in range(2 * n_pairs):
        for part in range(FOX_BIAS_PARTS):
            sel[part * LANES + h][(h // 2) * LANES + FOX_BIAS_PARTS * (h % 2) + part] = 1.0
    sel = jnp.asarray(sel, BF16)
    tile = lambda b, j: (b * nt + j, 0)
    return pl.pallas_call(
        functools.partial(_fkeys_kernel, ts=ts, n_pairs=n_pairs),
        grid=(bsz, nt),
        in_specs=[pl.BlockSpec((ts, LANES), tile), pl.BlockSpec((ts, width), tile),
                  pl.BlockSpec(sel.shape, lambda b, j: (0, 0))],
        out_specs=pl.BlockSpec((ts, 2 * width), tile),
        out_shape=jax.ShapeDtypeStruct((t, 2 * width), BF16),
        scratch_shapes=[pltpu.VMEM((1, LANES), F32)],
        compiler_params=_cparams(("parallel", "arbitrary")),
        name="forget_keys",
    )(flog, k, sel)


def _fox_kernel(q_ref, ka_ref, vt_ref, o_ref, acc_ref, m_ref, l_ref, qa_ref, st0, st1, p0, p1, al0, al1,
                *, tq, tk):
    st_refs, p_refs, al_refs = (st0, st1), (p0, p1), (al0, al1)
    n_full = pl.program_id(2)
    lane = lax.broadcasted_iota(I32, (tq, LANES), 1)
    qp = q_ref[...]
    for hh in range(2):
        mine = (lane < HEAD_DIM) if hh == 0 else (lane >= HEAD_DIM)
        ones = (lane >= FOX_BIAS_PARTS * hh) & (lane < FOX_BIAS_PARTS * (hh + 1))
        qa_ref[hh] = jnp.concatenate([jnp.where(mine, qp, jnp.zeros_like(qp)),
                                      jnp.where(ones, 1.0, 0.0).astype(BF16)], axis=1)
    acc_ref[...] = jnp.zeros_like(acc_ref)
    m_ref[...] = jnp.full_like(m_ref, NEG_INF)
    l_ref[...] = jnp.zeros_like(l_ref)

    def scores(kb, hh):
        off = pl.multiple_of(kb * tk, tk)
        return lax.dot_general(ka_ref[pl.ds(off, tk), :], qa_ref[hh], (((1,), (1,)), ((), ())),
                               preferred_element_type=F32)

    def softmax_step(st, hh):
        m_old = m_ref[hh:hh + 1, :]
        m_new = jnp.maximum(m_old, jnp.max(st, axis=0, keepdims=True))
        alpha = jnp.exp2(m_old - m_new)
        p = jnp.exp2(st - m_new)
        l_ref[hh:hh + 1, :] = alpha * l_ref[hh:hh + 1, :] + jnp.sum(p, axis=0, keepdims=True)
        m_ref[hh:hh + 1, :] = m_new
        return alpha, p.astype(BF16)

    def accumulate(kb, hh, alpha, p):
        off = pl.multiple_of(kb * tk, tk)
        rows = slice(HEAD_DIM * hh, HEAD_DIM * (hh + 1))
        acc_ref[rows, :] = alpha * acc_ref[rows, :] + jnp.dot(vt_ref[rows, pl.ds(off, tk)], p,
                                                               preferred_element_type=F32)

    keep = lax.broadcasted_iota(I32, (tk, tq), 0) <= lax.broadcasted_iota(I32, (tk, tq), 1)
    for hh in range(2):
        alpha, p = softmax_step(jnp.where(keep, scores(n_full, hh), NEG_INF), hh)
        accumulate(n_full, hh, alpha, p)

    def stage_a(kb, slot):
        for hh in range(2):
            st_refs[slot][hh] = scores(kb, hh)

    def stage_b(slot):
        for hh in range(2):
            alpha, p = softmax_step(st_refs[slot][hh], hh)
            al_refs[slot][hh:hh + 1, :] = alpha
            p_refs[slot][hh] = p

    def stage_c(kb, slot):
        for hh in range(2):
            accumulate(kb, hh, al_refs[slot][hh:hh + 1, :], p_refs[slot][hh])

    def step(i, slot):
        stage_a(i, slot)
        stage_b(1 - slot)
        stage_c(i - 2, slot)

    odd = lax.rem(n_full, 2) == 1

    @pl.when(n_full >= 1)
    def _():
        stage_a(0, 0)

    @pl.when(n_full >= 2)
    def _():
        stage_a(1, 1)
        stage_b(0)

    @pl.when(n_full == 1)
    def _():
        stage_b(0)
        stage_c(0, 0)

    def body(j, c):
        i = 2 + 2 * j
        step(i, 0)
        step(i + 1, 1)
        return c

    lax.fori_loop(0, (n_full - 2) // 2, body, 0)

    @pl.when(jnp.logical_and(n_full >= 2, jnp.logical_not(odd)))
    def _():
        stage_b(1)
        stage_c(n_full - 2, 0)
        stage_c(n_full - 1, 1)

    @pl.when(jnp.logical_and(n_full >= 3, odd))
    def _():
        step(n_full - 1, 0)
        stage_b(0)
        stage_c(n_full - 2, 1)
        stage_c(n_full - 1, 0)

    head0 = lax.broadcasted_iota(I32, (LANES, tq), 0) < HEAD_DIM
    out_t = acc_ref[...] / jnp.where(head0, l_ref[0:1, :], l_ref[1:2, :])
    o_ref[...] = out_t.T.astype(o_ref.dtype)


def _forgetting_attention(q, kaug, vt, bsz, seq):
    t, width = q.shape
    n_pairs = width // LANES
    tq, tk = FOX_TQ, FOX_TK
    nq = seq // tq
    return pl.pallas_call(
        functools.partial(_fox_kernel, tq=tq, tk=tk),
        grid=(bsz, n_pairs, nq),
        in_specs=[
            pl.BlockSpec((tq, LANES), lambda b, h, i: (b * nq + i, h)),
            pl.BlockSpec((seq, 2 * LANES), lambda b, h, i: (b, h)),
            pl.BlockSpec((LANES, seq), lambda b, h, i: (h, b)),
        ],
        out_specs=pl.BlockSpec((tq, LANES), lambda b, h, i: (b * nq + i, h)),
        out_shape=jax.ShapeDtypeStruct((t, width), BF16),
        scratch_shapes=[pltpu.VMEM((LANES, tq), F32), pltpu.VMEM((8, tq), F32), pltpu.VMEM((8, tq), F32),
                        pltpu.VMEM((2, tq, 2 * LANES), BF16)]
        + [pltpu.VMEM((2, tk, tq), F32)] * 2 + [pltpu.VMEM((2, tk, tq), BF16)] * 2 + [pltpu.VMEM((8, tq), F32)] * 2,
        compiler_params=_cparams(("parallel", "parallel", "arbitrary")),
        name="forgetting_attention",
    )(q, kaug, vt)


def _s5_prep_kernel(lre_ref, lim_ref, ldt_ref, bre_ref, bim_ref, are_ref, aim_ref, bb_re, bb_im):
    dt = jnp.exp(ldt_ref[...])
    lre = jnp.minimum(lre_ref[...], -1e-4)
    lim = lim_ref[...]
    mag = jnp.exp(lre * dt)
    a_re = mag * jnp.cos(lim * dt)
    a_im = mag * jnp.sin(lim * dt)
    den = lre * lre + lim * lim
    nre, nim = a_re - 1.0, a_im
    g_re = (nre * lre + nim * lim) / den
    g_im = (nim * lre - nre * lim) / den
    n = lre.shape[-1]
    per = n // bre_ref.shape[0]
    for j in range(bre_ref.shape[0]):
        gr = g_re[:, per * j:per * (j + 1)]
        gi = g_im[:, per * j:per * (j + 1)]
        bb_re[j] = (gr * bre_ref[j] - gi * bim_ref[j]).astype(bb_re.dtype)
        bb_im[j] = (gr * bim_ref[j] + gi * bre_ref[j]).astype(bb_im.dtype)
    are_ref[...] = a_re
    aim_ref[...] = a_im


def _block_diag_slabs(blocks, slabs):
    g, r, c = blocks.shape
    per = g // slabs
    eye = jnp.eye(per, dtype=bool)[None, :, None, :, None]
    bd = jnp.where(eye, blocks.reshape(slabs, per, r, 1, c), jnp.zeros((), blocks.dtype))
    return bd.reshape(slabs, per * r, per * c)


def _s5_kernel(u_ref, are_ref, aim_ref, bb_re, bb_im, cc_re, cc_im, dsk_ref, gw_ref, gb_ref, o_ref,
               xr0, xr1, xi0, xi1, carry_ref, *, ts, slabs):
    @pl.when(pl.program_id(1) == 0)
    def _():
        carry_ref[...] = jnp.zeros_like(carry_ref)

    xr, xi = (xr0, xr1), (xi0, xi1)
    per_half = TILE_ROWS // 2
    u = u_ref[...]
    ub = u.astype(BF16)
    for j in range(slabs):
        uj = ub[:, LANES * j:LANES * (j + 1)]
        bur = jnp.dot(uj, bb_re[j], preferred_element_type=F32)
        bui = jnp.dot(uj, bb_im[j], preferred_element_type=F32)
        half, base = j // 2, (j % 2) * per_half
        for s in range(per_half):
            xr[half][pl.ds(base + s, ts, stride=TILE_ROWS), :] = bur[:, LANES * s:LANES * (s + 1)]
            xi[half][pl.ds(base + s, ts, stride=TILE_ROWS), :] = bui[:, LANES * s:LANES * (s + 1)]
    a_r = [are_ref[TILE_ROWS * h:TILE_ROWS * (h + 1), :] for h in range(2)]
    a_i = [aim_ref[TILE_ROWS * h:TILE_ROWS * (h + 1), :] for h in range(2)]

    def steps(tb, state):
        state = list(state)
        for tt in range(TILE_ROWS):
            row = pl.multiple_of((tb * TILE_ROWS + tt) * TILE_ROWS, TILE_ROWS)
            for h in range(2):
                sr, si = state[h], state[2 + h]
                nr = a_r[h] * sr - a_i[h] * si + xr[h][pl.ds(row, TILE_ROWS), :]
                ni = a_r[h] * si + a_i[h] * sr + xi[h][pl.ds(row, TILE_ROWS), :]
                xr[h][pl.ds(row, TILE_ROWS), :] = nr
                xi[h][pl.ds(row, TILE_ROWS), :] = ni
                state[h], state[2 + h] = nr, ni
        return tuple(state)

    final = lax.fori_loop(0, ts // TILE_ROWS, steps, tuple(carry_ref[i] for i in range(4)))
    for i in range(4):
        carry_ref[i] = final[i]
    ys = []
    for j in range(slabs):
        half, base = j // 2, (j % 2) * per_half
        gather = lambda ref: jnp.concatenate(
            [ref[pl.ds(base + s, ts, stride=TILE_ROWS), :] for s in range(per_half)], axis=1).astype(BF16)
        yj = jnp.dot(gather(xr[half]), cc_re[j], preferred_element_type=F32)
        ys.append(yj - jnp.dot(gather(xi[half]), cc_im[j], preferred_element_type=F32))
    y = jnp.concatenate(ys, axis=1) + dsk_ref[...] * u
    y = 0.5 * y * (1.0 + jnp.tanh(math.sqrt(2.0 / math.pi) * (y + 0.044715 * (y * y * y))))
    gate = jnp.dot(y.astype(BF16), gw_ref[...], preferred_element_type=F32) + gb_ref[...]
    o_ref[...] = (y * jax.nn.sigmoid(gate)).astype(o_ref.dtype)


def _s5_mixer(u, lam_re, lam_im, log_dt, b_re, b_im, c_re, c_im, d_skip, glu_w, glu_b, bsz, seq):
    t, width = u.shape
    groups, states = lam_re.shape
    n = groups * states
    slabs = width // LANES
    ts = S5_TILE
    assert n == 2 * TILE_ROWS * LANES and slabs == 4
    flat = lambda a: a.reshape(1, n)
    ldt = jnp.repeat(log_dt, states).reshape(1, n)
    bre_bd = _block_diag_slabs(jnp.swapaxes(b_re, 1, 2), slabs)
    bim_bd = _block_diag_slabs(jnp.swapaxes(b_im, 1, 2), slabs)
    cre_bd = _block_diag_slabs(jnp.swapaxes(c_re, 1, 2), slabs).astype(BF16)
    cim_bd = _block_diag_slabs(jnp.swapaxes(c_im, 1, 2), slabs).astype(BF16)
    full = lambda a: pl.BlockSpec(a.shape, lambda *_: (0,) * a.ndim)
    prep_in = (flat(lam_re), flat(lam_im), ldt, bre_bd, bim_bd)
    a_re, a_im, bb_re, bb_im = pl.pallas_call(
        _s5_prep_kernel,
        grid=(1,),
        in_specs=[full(a) for a in prep_in],
        out_specs=[pl.BlockSpec((1, n), lambda i: (0, 0))] * 2 + [pl.BlockSpec(bre_bd.shape, lambda i: (0, 0, 0))] * 2,
        out_shape=[jax.ShapeDtypeStruct((1, n), F32)] * 2 + [jax.ShapeDtypeStruct(bre_bd.shape, BF16)] * 2,
        compiler_params=_cparams(("arbitrary",)),
        name="s5_prep",
    )(*prep_in)
    nt = seq // ts
    tiles = lambda a: a.reshape(n // LANES, LANES)
    consts = (tiles(a_re), tiles(a_im), bb_re, bb_im, cre_bd, cim_bd, d_skip.reshape(1, width),
              glu_w.astype(BF16), glu_b.reshape(1, width))
    return pl.pallas_call(
        functools.partial(_s5_kernel, ts=ts, slabs=slabs),
        grid=(bsz, nt),
        in_specs=[pl.BlockSpec((ts, width), lambda b, j: (b * nt + j, 0))] + [full(a) for a in consts],
        out_specs=pl.BlockSpec((ts, width), lambda b, j: (b * nt + j, 0)),
        out_shape=jax.ShapeDtypeStruct((t, width), BF16),
        scratch_shapes=[pltpu.VMEM((ts * TILE_ROWS, LANES), F32)] * 4 + [pltpu.VMEM((4, TILE_ROWS, LANES), F32)],
        compiler_params=_cparams(("parallel", "arbitrary")),
        name="s5_mixer",
    )(u, *consts)


def kernel(x, c, mod_w, mod_b, ln_g, ln_b, even_w_in, pool_w, pool_scale, rel_bias, even_w_out, odd_w_in, forget_b, ssm_lam_re, ssm_lam_im, ssm_log_dt, ssm_b_re, ssm_b_im, ssm_c_re, ssm_c_im, ssm_d, ssm_glu_w, ssm_glu_b, odd_w_out, router_w, router_b, exp_w_gu, exp_b_gu, exp_w_down, exp_b_down):
    bsz, seq, d = x.shape
    depth = mod_w.shape[0]
    alpha = (2.0 * depth) ** 0.25
    mods = _ada_mods(c, mod_w.reshape(depth * 2, d, 3 * d), mod_b.reshape(depth * 2, 3 * d))
    xt = x.reshape(bsz * seq, d)
    for layer in range(depth):
        i = layer // 2
        if layer % 2 == 0:
            w_in = even_w_in[i]
            width = w_in.shape[1] // 4
            zp, q, k, vt = _inproj_even(xt, mods, 2 * layer, w_in[:, :3 * width].astype(BF16),
                                        w_in[:, 3 * width:].T.astype(BF16), seq)
            ya = _pool_mixer(zp, pool_w[i].astype(BF16), pool_scale[i], seq)
            yb = _chunk_attention(q, k, vt, _cattn_bias_table(rel_bias[i]), bsz, seq)
            w_out = even_w_out[i]
        else:
            width = ssm_d.shape[1] * ssm_d.shape[2]
            w_in = odd_w_in[i]
            n_heads = forget_b.shape[1]
            wf = jnp.zeros((d, LANES), BF16).at[:, :n_heads].set(w_in[:, 4 * width:].astype(BF16))
            fb = jnp.zeros((1, LANES), F32).at[0, :n_heads].set(forget_b[i])
            zs, q, k, vt, flog = _inproj_odd(xt, mods, 2 * layer, w_in[:, :3 * width].astype(BF16),
                                             w_in[:, 3 * width:4 * width].T.astype(BF16), wf, fb, seq)
            ya = _s5_mixer(zs, ssm_lam_re[i], ssm_lam_im[i], ssm_log_dt[i], ssm_b_re[i], ssm_b_im[i],
                           ssm_c_re[i], ssm_c_im[i], ssm_d[i], ssm_glu_w[i], ssm_glu_b[i], bsz, seq)
            yb = _forgetting_attention(q, _forget_keys(flog, k, bsz, seq), vt, bsz, seq)
            w_out = odd_w_out[i]
        xt = _outproj_norm(ya, yb, xt, mods, 2 * layer, w_out.astype(BF16), ln_g[layer, 0], ln_b[layer, 0],
                           seq, alpha)
        xt = _moe_block(xt, mods, 2 * layer + 1, router_w[layer], router_b[layer], layer, exp_w_gu, exp_b_gu,
                        exp_w_down, exp_b_down, ln_g[layer, 1], ln_b[layer, 1], seq, alpha)
    return xt.reshape(bsz, seq, d)
```

```python
import functools
import math

import jax
import jax.numpy as jnp
from jax import lax
from jax.experimental import pallas as pl
from jax.experimental.pallas import tpu as pltpu

F32 = jnp.float32
BF16 = jnp.bfloat16
I32 = jnp.int32
HIGHEST = lax.Precision.HIGHEST

LANES = 128
SUBLANES = 8
VMEM_LIMIT_BYTES = 60000 * 1024

HEAD_DIM = 64
CHUNK = 64
LEFT_CHUNKS = 8
REL_CLIP = 128
POOL_WINDOWS = (2, 4, 8, 16)
POOL_HALO = 16
SSM_GROUP_DIM = 16
SSM_STATE = 64
N_EXPERTS = 32
TOP_K = 4
SWIGLU_LIMIT = 7.0
SWIGLU_ALPHA = 1.702
LN_EPS = 1e-5
NEG_INF = -1e30
LOG2E = math.log2(math.e)
FOX_BIAS_PARTS = 3

ROW_TILE = 512
CATTN_Q = 2 * CHUNK
CATTN_BAND = (LEFT_CHUNKS + 2) * CHUNK
FOX_TQ = 512
FOX_TK = 512
S5_TILE = 256
FCUM_TILE = 256
TILE_ROWS = SUBLANES
MOE_ROW_BLOCK = 256
MOE_TOK_TILE = 256
FILL_TILES = 128
ROUTE_TILE = 512


def _cparams(sem):
    return pltpu.CompilerParams(dimension_semantics=sem, vmem_limit_bytes=VMEM_LIMIT_BYTES)


def _layer_norm(r, g, b):
    mu = jnp.mean(r, axis=-1, keepdims=True)
    d = r - mu
    var = jnp.mean(d * d, axis=-1, keepdims=True)
    return d * lax.rsqrt(var + LN_EPS) * g + b


def _mod_kernel(c_ref, w_ref, b_ref, o_ref):
    c = c_ref[...]
    s = c * jax.nn.sigmoid(c)
    o_ref[0] = jnp.dot(s, w_ref[0], precision=HIGHEST, preferred_element_type=F32) + b_ref[0]


def _ada_mods(c, mod_w, mod_b):
    bsz, d = c.shape
    m = mod_w.shape[0]
    rows = 8
    c_pad = jnp.zeros((rows, d), F32).at[:bsz].set(c)
    out = pl.pallas_call(
        _mod_kernel,
        grid=(m, 3),
        in_specs=[
            pl.BlockSpec((rows, d), lambda i, j: (0, 0)),
            pl.BlockSpec((1, d, d), lambda i, j: (i, 0, j)),
            pl.BlockSpec((1, 1, d), lambda i, j: (i, 0, j)),
        ],
        out_specs=pl.BlockSpec((1, rows, d), lambda i, j: (i, 0, j)),
        out_shape=jax.ShapeDtypeStruct((m, rows, 3 * d), F32),
        compiler_params=_cparams(("parallel", "parallel")),
        name="ada_mod",
    )(c_pad, mod_w, mod_b.reshape(m, 1, 3 * d))
    return out[:, :bsz].reshape(m, bsz, 3, d)


def _mod_spec(midx, tiles_per_batch, d):
    return pl.BlockSpec((None, None, 3, d), lambda i: (midx, i // tiles_per_batch, 0, 0))


def _inproj_even_kernel(x_ref, mod_ref, w_ref, wvt_ref, zp_ref, q_ref, k_ref, vt_ref, *, width):
    h = (x_ref[...] * (1.0 + mod_ref[1:2, :]) + mod_ref[0:1, :]).astype(BF16)
    z = jnp.dot(h, w_ref[...], preferred_element_type=F32)
    zp_ref[...] = z[:, :width]
    q_ref[...] = (z[:, width:2 * width] * (HEAD_DIM ** -0.5 * LOG2E)).astype(BF16)
    k_ref[...] = z[:, 2 * width:3 * width].astype(BF16)
    vt_ref[...] = lax.dot_general(wvt_ref[...], h, (((1,), (1,)), ((), ())),
                                  preferred_element_type=F32).astype(BF16)


def _inproj_even(x, mods, midx, w_main_bf, wvt_bf, seq):
    t, d = x.shape
    width = wvt_bf.shape[0]
    tm = ROW_TILE
    row = lambda i: (i, 0)
    return pl.pallas_call(
        functools.partial(_inproj_even_kernel, width=width),
        grid=(t // tm,),
        in_specs=[
            pl.BlockSpec((tm, d), row),
            _mod_spec(midx, seq // tm, d),
            pl.BlockSpec((d, 3 * width), lambda i: (0, 0)),
            pl.BlockSpec((width, d), lambda i: (0, 0)),
        ],
        out_specs=[pl.BlockSpec((tm, width), row)] * 3
        + [pl.BlockSpec((None, width, tm), lambda i: (i // (seq // tm), 0, i % (seq // tm)))],
        out_shape=[jax.ShapeDtypeStruct((t, width), F32)] + [jax.ShapeDtypeStruct((t, width), BF16)] * 2
        + [jax.ShapeDtypeStruct((t // seq, width, seq), BF16)],
        compiler_params=_cparams(("parallel",)),
        name="inproj_even",
    )(x, mods, w_main_bf, wvt_bf)


def _pool_kernel(zp_ref, halo_ref, w_ref, sc_ref, o_ref, xs_ref, *, ts, seq):
    i = pl.program_id(0)
    t0 = lax.rem(i * ts, seq)
    xs_ref[0:POOL_HALO, :] = jnp.where(t0 == 0, 0.0, halo_ref[...])
    xs_ref[POOL_HALO:POOL_HALO + ts, :] = zp_ref[...]
    pos = lax.broadcasted_iota(I32, (ts, LANES), 0) + t0
    for g, win in enumerate(POOL_WINDOWS):
        cols = slice(LANES * g, LANES * (g + 1))
        u = xs_ref[POOL_HALO:POOL_HALO + ts, cols]
        acc = u
        for k in range(1, win):
            acc = acc + xs_ref[POOL_HALO - k:POOL_HALO - k + ts, cols]
        cnt = jnp.minimum(pos + 1, win).astype(F32)
        dlt = acc / cnt - u
        y = jnp.dot(dlt.astype(BF16), w_ref[g], preferred_element_type=F32)
        o_ref[:, cols] = (y * sc_ref[:, cols]).astype(o_ref.dtype)


def _pool_mixer(zp, w_pool_bf, pool_scale, seq):
    t, width = zp.shape
    ts = ROW_TILE
    hb = ts // POOL_HALO
    return pl.pallas_call(
        functools.partial(_pool_kernel, ts=ts, seq=seq),
        grid=(t // ts,),
        in_specs=[
            pl.BlockSpec((ts, width), lambda i: (i, 0)),
            pl.BlockSpec((POOL_HALO, width), lambda i: (jnp.maximum(i * hb - 1, 0), 0)),
            pl.BlockSpec(w_pool_bf.shape, lambda i: (0, 0, 0)),
            pl.BlockSpec((1, width), lambda i: (0, 0)),
        ],
        out_specs=pl.BlockSpec((ts, width), lambda i: (i, 0)),
        out_shape=jax.ShapeDtypeStruct((t, width), BF16),
        scratch_shapes=[pltpu.VMEM((ts + POOL_HALO, width), F32)],
        compiler_params=_cparams(("parallel",)),
        name="pool_mixer",
    )(zp, zp, w_pool_bf, pool_scale.reshape(1, width))


def _cbias_kernel(r_ref, o_ref):
    base = jnp.broadcast_to(r_ref[...], (CATTN_Q, r_ref.shape[-1]))
    rolled = pltpu.roll(base, 0, 1, stride=1, stride_axis=0)[:, :CATTN_BAND]
    qi = lax.broadcasted_iota(I32, (CATTN_Q, CATTN_BAND), 0)
    kp = lax.broadcasted_iota(I32, (CATTN_Q, CATTN_BAND), 1)
    lo = (qi // CHUNK) * CHUNK
    valid = (kp >= lo) & (kp < lo + (LEFT_CHUNKS + 1) * CHUNK)
    o_ref[...] = jnp.where(valid, rolled * LOG2E, NEG_INF).T


def _cattn_bias_table(rel_bias):
    h = rel_bias.shape[0]
    wide = 1024
    far = LEFT_CHUNKS * CHUNK - REL_CLIP + 1
    tail = wide - far - (2 * REL_CLIP - 1)
    last = rel_bias[:, 2 * REL_CLIP:]
    base = jnp.concatenate(
        [jnp.broadcast_to(last, (h, far)), rel_bias[:, 1:2 * REL_CLIP][:, ::-1], jnp.broadcast_to(last, (h, tail))],
        axis=1).reshape(h, 1, wide)
    return pl.pallas_call(
        _cbias_kernel,
        grid=(h,),
        in_specs=[pl.BlockSpec((None, 1, wide), lambda i: (i, 0, 0))],
        out_specs=pl.BlockSpec((None, CATTN_BAND, CATTN_Q), lambda i: (i, 0, 0)),
        out_shape=jax.ShapeDtypeStruct((h, CATTN_BAND, CATTN_Q), F32),
        compiler_params=_cparams(("parallel",)),
        name="cattn_bias",
    )(base)


def _cattn_kernel(q_ref, k_ref, vt_ref, bias_ref, o_ref, st_ref, e_ref, l_ref, *, n_pairs):
    p = pl.program_id(1)
    start = pl.multiple_of(p * CATTN_Q, CATTN_Q)
    kpos = lax.broadcasted_iota(I32, (CATTN_BAND, CATTN_Q), 0) + start
    real_key = kpos >= LEFT_CHUNKS * CHUNK
    lane = lax.broadcasted_iota(I32, (CATTN_Q, LANES), 1)
    n_heads = 2 * n_pairs
    for h in range(n_heads):
        cols = slice(LANES * (h // 2), LANES * (h // 2 + 1))
        qp = q_ref[:, cols]
        mine = (lane < HEAD_DIM) if h % 2 == 0 else (lane >= HEAD_DIM)
        qm = jnp.where(mine, qp, jnp.zeros_like(qp))
        st_ref[h] = lax.dot_general(k_ref[pl.ds(start, CATTN_BAND), cols], qm, (((1,), (1,)), ((), ())),
                                    preferred_element_type=F32)
    for h in range(n_heads):
        st = jnp.where(real_key, st_ref[h] + bias_ref[h], NEG_INF)
        m = jnp.max(st, axis=0, keepdims=True)
        e = jnp.exp2(st - m)
        l_ref[h:h + 1, :] = jnp.sum(e, axis=0, keepdims=True)
        e_ref[h] = e.astype(BF16)
    for hp in range(n_pairs):
        outs = []
        for hh in range(2):
            h = 2 * hp + hh
            vth = vt_ref[HEAD_DIM * h:HEAD_DIM * (h + 1), pl.ds(start, CATTN_BAND)]
            outs.append(jnp.dot(vth, e_ref[h], preferred_element_type=F32) / l_ref[h:h + 1, :])
        o_ref[:, LANES * hp:LANES * (hp + 1)] = jnp.concatenate(outs, axis=0).T.astype(o_ref.dtype)


def _chunk_attention(q, k, vt, bias_t, bsz, seq):
    t, width = q.shape
    n_pairs = width // LANES
    pad = LEFT_CHUNKS * CHUNK
    kp = jnp.pad(k.reshape(bsz, seq, width), ((0, 0), (pad, 0), (0, 0)))
    vtp = jnp.pad(vt, ((0, 0), (0, 0), (pad, 0)))
    nq = seq // CATTN_Q
    return pl.pallas_call(
        functools.partial(_cattn_kernel, n_pairs=n_pairs),
        grid=(bsz, nq),
        in_specs=[
            pl.BlockSpec((CATTN_Q, width), lambda b, p: (b * nq + p, 0)),
            pl.BlockSpec((None, seq + pad, width), lambda b, p: (b, 0, 0)),
            pl.BlockSpec((None, width, seq + pad), lambda b, p: (b, 0, 0)),
            pl.BlockSpec(bias_t.shape, lambda b, p: (0, 0, 0)),
        ],
        out_specs=pl.BlockSpec((CATTN_Q, width), lambda b, p: (b * nq + p, 0)),
        out_shape=jax.ShapeDtypeStruct((t, width), BF16),
        scratch_shapes=[pltpu.VMEM((2 * n_pairs, CATTN_BAND, CATTN_Q), F32),
                        pltpu.VMEM((2 * n_pairs, CATTN_BAND, CATTN_Q), BF16),
                        pltpu.VMEM((2 * n_pairs, CATTN_Q), F32)],
        compiler_params=_cparams(("parallel", "arbitrary")),
        name="chunk_attention",
    )(q, kp, vtp, bias_t)


def _outproj_kernel(ya_ref, yb_ref, x_ref, mod_ref, w_ref, g_ref, b_ref, o_ref, *, alpha, half):
    y = jnp.dot(ya_ref[...], w_ref[0:half, :], preferred_element_type=F32)
    y = y + jnp.dot(yb_ref[...], w_ref[half:2 * half, :], preferred_element_type=F32)
    r = alpha * x_ref[...] + (1.0 + mod_ref[2:3, :]) * y
    o_ref[...] = _layer_norm(r, g_ref[...], b_ref[...])


def _outproj_norm(ya, yb, x, mods, midx, w_out_bf, ln_g, ln_b, seq, alpha):
    t, d = x.shape
    half = ya.shape[1]
    tm = ROW_TILE
    row = lambda i: (i, 0)
    return pl.pallas_call(
        functools.partial(_outproj_kernel, alpha=alpha, half=half),
        grid=(t // tm,),
        in_specs=[
            pl.BlockSpec((tm, half), row),
            pl.BlockSpec((tm, half), row),
            pl.BlockSpec((tm, d), row),
            _mod_spec(midx, seq // tm, d),
            pl.BlockSpec((2 * half, d), lambda i: (0, 0)),
            pl.BlockSpec((1, d), lambda i: (0, 0)),
            pl.BlockSpec((1, d), lambda i: (0, 0)),
        ],
        out_specs=pl.BlockSpec((tm, d), row),
        out_shape=jax.ShapeDtypeStruct((t, d), F32),
        compiler_params=_cparams(("parallel",)),
        name="outproj_norm",
    )(ya, yb, x, mods, w_out_bf, ln_g.reshape(1, d), ln_b.reshape(1, d))


def _route_kernel(x_ref, mod_ref, rw_ref, rb_ref, h_ref, e_ref, r_ref, g_ref, c_ref, carry_ref, *, tm):
    @pl.when(pl.program_id(0) == 0)
    def _():
        carry_ref[...] = jnp.zeros_like(carry_ref)

    h = x_ref[...] * (1.0 + mod_ref[1:2, :]) + mod_ref[0:1, :]
    _store_token_tiles(h_ref, h)
    logits = lax.dot_general(rw_ref[...], h, (((1,), (1,)), ((), ())), precision=HIGHEST,
                             preferred_element_type=F32) + rb_ref[...]
    eio = lax.broadcasted_iota(I32, (N_EXPERTS, tm), 0)
    vals, hots = [], []
    for k in range(TOP_K):
        m = jnp.max(logits, axis=0, keepdims=True)
        idx = jnp.min(jnp.where(logits == m, eio, N_EXPERTS), axis=0, keepdims=True)
        hot = eio == idx
        e_ref[k:k + 1, :] = idx
        vals.append(m)
        hots.append(hot)
        logits = jnp.where(hot, -jnp.inf, logits)
    exps = [jnp.exp(v - vals[0]) for v in vals]
    denom = exps[0] + exps[1] + exps[2] + exps[3]
    for k in range(TOP_K):
        g_ref[k:k + 1, :] = exps[k] / denom
    cnt = jnp.zeros((N_EXPERTS, tm), F32)
    for hot in hots:
        cnt = cnt + hot.astype(F32)
    before = (lax.broadcasted_iota(I32, (tm, tm), 0) < lax.broadcasted_iota(I32, (tm, tm), 1))
    prefix = jnp.dot(cnt.astype(BF16), before.astype(BF16), preferred_element_type=F32)
    tot = prefix + carry_ref[...]
    for k in range(TOP_K):
        r_ref[k:k + 1, :] = jnp.sum(jnp.where(hots[k], tot, 0.0), axis=0, keepdims=True).astype(I32)
    new_carry = carry_ref[...] + jnp.sum(cnt, axis=1, keepdims=True)
    carry_ref[...] = new_carry
    c_ref[...] = jnp.broadcast_to(new_carry, c_ref.shape)


def _route(x, mods, midx, router_w, router_b, seq):
    t, d = x.shape
    tm = ROUTE_TILE
    tok = lambda i: (0, i)
    return pl.pallas_call(
        functools.partial(_route_kernel, tm=tm),
        grid=(t // tm,),
        in_specs=[
            pl.BlockSpec((tm, d), lambda i: (i, 0)),
            _mod_spec(midx, seq // tm, d),
            pl.BlockSpec((N_EXPERTS, d), lambda i: (0, 0)),
            pl.BlockSpec((N_EXPERTS, 1), lambda i: (0, 0)),
        ],
        out_specs=[
            pl.BlockSpec((tm * TILE_ROWS, LANES), lambda i: (i, 0)),
            pl.BlockSpec((TOP_K, tm), tok),
            pl.BlockSpec((TOP_K, tm), tok),
            pl.BlockSpec((TOP_K, tm), tok),
            pl.BlockSpec((N_EXPERTS, LANES), lambda i: (0, 0)),
        ],
        out_shape=[
            jax.ShapeDtypeStruct((t * TILE_ROWS, LANES), F32),
            jax.ShapeDtypeStruct((TOP_K, t), I32),
            jax.ShapeDtypeStruct((TOP_K, t), I32),
            jax.ShapeDtypeStruct((TOP_K, t), F32),
            jax.ShapeDtypeStruct((N_EXPERTS, LANES), F32),
        ],
        scratch_shapes=[pltpu.VMEM((N_EXPERTS, 1), F32)],
        compiler_params=_cparams(("arbitrary",)),
        name="moe_route",
    )(x, mods, router_w.T, router_b.reshape(N_EXPERTS, 1))


def _plan_kernel(e_ref, r_ref, ccol_ref, crow_ref, d_ref, blk_ref, exp_ref, nu_ref, *, tm, rb, nbp):
    inv = 1.0 / rb
    pad_col = jnp.floor((ccol_ref[...] + (rb - 1)) * inv) * rb
    pad_row = jnp.floor((crow_ref[0:1, :] + (rb - 1)) * inv) * rb
    ei = lax.broadcasted_iota(I32, (N_EXPERTS, LANES), 0)
    li = lax.broadcasted_iota(I32, (N_EXPERTS, LANES), 1)
    pstart = jnp.sum(jnp.where(li < ei, jnp.broadcast_to(pad_row, (N_EXPERTS, LANES)), 0.0),
                     axis=1, keepdims=True)
    padded = pad_col[:, 0:1]
    pend = pstart + padded
    eio = lax.broadcasted_iota(I32, (N_EXPERTS, tm), 0)
    for k in range(TOP_K):
        base = jnp.sum(jnp.where(eio == e_ref[k:k + 1, :], pstart, 0.0), axis=0, keepdims=True)
        d_ref[k:k + 1, :] = base.astype(I32) + r_ref[k:k + 1, :]
    row0 = (lax.broadcasted_iota(I32, (N_EXPERTS, nbp), 1) * rb).astype(F32)
    be = jnp.minimum(jnp.sum((pend <= row0).astype(F32), axis=0, keepdims=True), N_EXPERTS - 1.0)
    ebi = lax.broadcasted_iota(I32, (N_EXPERTS, nbp), 0).astype(F32)
    mine = ebi == be
    my_end = jnp.sum(jnp.where(mine, pend, 0.0), axis=0, keepdims=True)
    nxt = jnp.sum((pend <= my_end).astype(F32), axis=0, keepdims=True)
    order = jnp.sum(jnp.where(jnp.logical_and(padded > 0.0, ebi < be), 1.0, 0.0), axis=0, keepdims=True)
    blk_ref[0:1, :] = be.astype(I32)
    blk_ref[1:2, :] = jnp.where(nxt < N_EXPERTS, nxt, -1.0).astype(I32)
    blk_ref[2:3, :] = (order - 2.0 * jnp.floor(order * 0.5)).astype(I32)
    blk_ref[3:8, :] = jnp.zeros((5, nbp), I32)
    pstart_row = jnp.sum(jnp.where(ei < li, pad_col, 0.0), axis=0, keepdims=True)
    exp_ref[0:1, :] = crow_ref[0:1, :].astype(I32)
    exp_ref[1:2, :] = pstart_row.astype(I32)
    exp_ref[2:3, :] = pad_row.astype(I32)
    exp_ref[3:8, :] = jnp.zeros((5, LANES), I32)
    used = jnp.sum(padded, axis=0, keepdims=True) * inv
    nu_ref[...] = jnp.broadcast_to(used, nu_ref.shape).astype(I32)


def _plan(eidx, rank, counts_col, rb, nb):
    t = eidx.shape[1]
    tm = ROUTE_TILE
    nbp = -(-nb // LANES) * LANES
    counts_row = jnp.zeros((8, LANES), F32).at[:, :N_EXPERTS].set(counts_col[:, 0][None, :])
    tok = lambda i: (0, i)
    dest, blk, exp, nu = pl.pallas_call(
        functools.partial(_plan_kernel, tm=tm, rb=rb, nbp=nbp),
        grid=(t // tm,),
        in_specs=[
            pl.BlockSpec((TOP_K, tm), tok),
            pl.BlockSpec((TOP_K, tm), tok),
            pl.BlockSpec((N_EXPERTS, LANES), lambda i: (0, 0)),
            pl.BlockSpec((8, LANES), lambda i: (0, 0)),
        ],
        out_specs=[
            pl.BlockSpec((TOP_K, tm), tok),
            pl.BlockSpec((8, nbp), lambda i: (0, 0)),
            pl.BlockSpec((8, LANES), lambda i: (0, 0)),
            pl.BlockSpec((1, LANES), lambda i: (0, 0)),
        ],
        out_shape=[
            jax.ShapeDtypeStruct((TOP_K, t), I32),
            jax.ShapeDtypeStruct((8, nbp), I32),
            jax.ShapeDtypeStruct((8, LANES), I32),
            jax.ShapeDtypeStruct((1, LANES), I32),
        ],
        compiler_params=_cparams(("arbitrary",)),
        name="moe_plan",
    )(eidx, rank, counts_col, counts_row)
    return dest, (blk[0], blk[1], blk[2]), (exp[0], exp[1], exp[2]), nu[0, :1]


def _store_token_tiles(ref, x):
    n = x.shape[0]
    for s in range(TILE_ROWS):
        ref[pl.ds(s, n, stride=TILE_ROWS), :] = x[:, LANES * s:LANES * (s + 1)]


def _load_token_tiles(ref, n):
    return jnp.concatenate([ref[pl.ds(s, n, stride=TILE_ROWS), :] for s in range(TILE_ROWS)], axis=1)


def _tile_copy(src_ref, src_tok, dst_ref, dst_tok, sem):
    src = src_ref.at[pl.ds(pl.multiple_of(src_tok * TILE_ROWS, TILE_ROWS), TILE_ROWS)]
    dst = dst_ref.at[pl.ds(pl.multiple_of(dst_tok * TILE_ROWS, TILE_ROWS), TILE_ROWS)]
    return pltpu.make_async_copy(src, dst, sem)


def _dispatch_kernel(cnt_ref, first_ref, padded_ref, h_ref, dest_ref, xout_ref, zero_ref, sem, zsem, *, tm, n_rows):
    i = pl.program_id(0)
    steps = pl.num_programs(0)

    def fill_copy(row, size):
        return pltpu.make_async_copy(zero_ref.at[pl.ds(0, size * TILE_ROWS)],
                                     xout_ref.at[pl.ds(pl.multiple_of(row * TILE_ROWS, TILE_ROWS), size * TILE_ROWS)],
                                     zsem)

    def each_fill(act):
        def per_expert(e, c):
            lo = first_ref[e] + cnt_ref[e]
            n = jnp.where(e == N_EXPERTS - 1, n_rows, first_ref[e] + padded_ref[e]) - lo
            whole, rem = n // FILL_TILES, lax.rem(n, FILL_TILES)

            def big(j, c2):
                act(fill_copy(lo + j * FILL_TILES, FILL_TILES))
                return c2

            lax.fori_loop(0, whole, big, 0)
            size = FILL_TILES // 2
            while size >= 1:
                @pl.when((rem & size) != 0)
                def _(size=size):
                    act(fill_copy(lo + whole * FILL_TILES + (rem & ~(2 * size - 1)), size))
                size //= 2
            return c

        lax.fori_loop(0, N_EXPERTS, per_expert, 0)

    @pl.when(i == 0)
    def _():
        zero_ref[...] = jnp.zeros_like(zero_ref)
        each_fill(lambda copy: copy.start())

    def issue(r, c):
        for k in range(TOP_K):
            _tile_copy(h_ref, r, xout_ref, dest_ref[k, r], sem).start(priority=k % 2)
        return c

    lax.fori_loop(0, tm, issue, 0)

    def drain(r, c):
        for k in range(TOP_K):
            _tile_copy(h_ref, 0, xout_ref, 0, sem).wait()
        return c

    lax.fori_loop(0, tm, drain, 0)

    @pl.when(i == steps - 1)
    def _():
        each_fill(lambda copy: copy.wait())


def _dispatch(h_tiles, dest, expert_table, n_rows):
    t = h_tiles.shape[0] // TILE_ROWS
    tm = MOE_TOK_TILE
    return pl.pallas_call(
        functools.partial(_dispatch_kernel, tm=tm, n_rows=n_rows),
        grid_spec=pltpu.PrefetchScalarGridSpec(
            num_scalar_prefetch=3,
            grid=(t // tm,),
            in_specs=[
                pl.BlockSpec((tm * TILE_ROWS, LANES), lambda i, *_: (i, 0)),
                pl.BlockSpec((TOP_K, tm), lambda i, *_: (0, i), memory_space=pltpu.SMEM),
            ],
            out_specs=pl.BlockSpec(memory_space=pl.ANY),
            scratch_shapes=[pltpu.VMEM((FILL_TILES * TILE_ROWS, LANES), h_tiles.dtype), pltpu.SemaphoreType.DMA,
                            pltpu.SemaphoreType.DMA],
        ),
        out_shape=jax.ShapeDtypeStruct((n_rows * TILE_ROWS, LANES), h_tiles.dtype),
        compiler_params=_cparams(("arbitrary",)),
        name="moe_dispatch",
    )(*expert_table, h_tiles, dest)


def _expert_kernel(be_ref, nx_ref, slot_ref, nu_ref, x_ref, bgu_ref, bd_ref, wgu_hbm, wd_hbm, o_ref,
                   wgu_f, wd_f, wgu_s, wd_s, sems, *, dff, rb, layer):
    b = pl.program_id(0)
    e = be_ref[b]
    slot = slot_ref[b]
    used = b < nu_ref[0]
    first = jnp.logical_and(used, jnp.logical_or(b == 0, e != be_ref[jnp.maximum(b - 1, 0)]))

    def fetch(expert, s):
        return (pltpu.make_async_copy(wgu_hbm.at[layer, expert], wgu_f.at[s], sems.at[0, s]),
                pltpu.make_async_copy(wd_hbm.at[layer, expert], wd_f.at[s], sems.at[1, s]))

    @pl.when(b == 0)
    def _():
        for copy in fetch(e, slot):
            copy.start()

    @pl.when(first)
    def _():
        for copy in fetch(e, slot):
            copy.wait()
        wgu_s[...] = wgu_f[slot].astype(BF16)
        wd_s[...] = wd_f[slot].astype(BF16)

        @pl.when(nx_ref[b] >= 0)
        def _():
            for copy in fetch(nx_ref[b], 1 - slot):
                copy.start()

    @pl.when(used)
    def _():
        x = _load_token_tiles(x_ref, rb)
        gu = jnp.dot(x.astype(BF16), wgu_s[...], preferred_element_type=F32) + bgu_ref[0]
        gate = jnp.minimum(gu[:, :dff], SWIGLU_LIMIT)
        up = jnp.clip(gu[:, dff:], -SWIGLU_LIMIT, SWIGLU_LIMIT)
        act = (up + 1.0) * (gate * jax.nn.sigmoid(SWIGLU_ALPHA * gate))
        _store_token_tiles(o_ref, jnp.dot(act.astype(BF16), wd_s[...], preferred_element_type=F32) + bd_ref[0])

    @pl.when(jnp.logical_not(used))
    def _():
        o_ref[...] = jnp.zeros_like(o_ref)


def _expert_ffn(x_disp, block_table, n_used, layer, w_gu, b_gu, w_down, b_down, rb):
    nl, ne, d, dff2 = w_gu.shape
    dff = dff2 // 2
    nb = x_disp.shape[0] // (rb * TILE_ROWS)
    pick = lambda b, be, *_: (layer, be[b], 0, 0)
    blocks = lambda b, *_: (b, 0)
    return pl.pallas_call(
        functools.partial(_expert_kernel, dff=dff, rb=rb, layer=layer),
        grid_spec=pltpu.PrefetchScalarGridSpec(
            num_scalar_prefetch=4,
            grid=(nb,),
            in_specs=[
                pl.BlockSpec((rb * TILE_ROWS, LANES), blocks),
                pl.BlockSpec((None, 1, 1, dff2), pick),
                pl.BlockSpec((None, 1, 1, d), pick),
                pl.BlockSpec(memory_space=pl.ANY),
                pl.BlockSpec(memory_space=pl.ANY),
            ],
            out_specs=pl.BlockSpec((rb * TILE_ROWS, LANES), blocks),
            scratch_shapes=[pltpu.VMEM((2, d, dff2), F32), pltpu.VMEM((2, dff, d), F32),
                            pltpu.VMEM((d, dff2), BF16), pltpu.VMEM((dff, d), BF16),
                            pltpu.SemaphoreType.DMA((2, 2))],
        ),
        out_shape=jax.ShapeDtypeStruct(x_disp.shape, F32),
        compiler_params=_cparams(("arbitrary",)),
        name="moe_experts",
    )(*block_table, n_used, x_disp, b_gu.reshape(nl, ne, 1, dff2), b_down.reshape(nl, ne, 1, d), w_gu, w_down)


def _combine_kernel(dest_ref, gates_ref, x_ref, mod_ref, g_ref, b_ref, y_hbm, o_ref, buf, sem, *, tm, alpha):
    def issue(r, c):
        for k in range(TOP_K):
            _tile_copy(y_hbm, dest_ref[k, r], buf.at[k], r, sem).start(priority=k % 2)
        return c

    lax.fori_loop(0, tm, issue, 0)

    def drain(r, c):
        for k in range(TOP_K):
            _tile_copy(y_hbm, 0, buf.at[k], 0, sem).wait()
        return c

    lax.fori_loop(0, tm, drain, 0)
    y = gates_ref[:, 0:1] * _load_token_tiles(buf.at[0], tm)
    for k in range(1, TOP_K):
        y = y + gates_ref[:, k:k + 1] * _load_token_tiles(buf.at[k], tm)
    r = alpha * x_ref[...] + (1.0 + mod_ref[2:3, :]) * y
    o_ref[...] = _layer_norm(r, g_ref[...], b_ref[...])


def _combine_norm(y_disp, dest, gates_col, x, mods, midx, ln_g, ln_b, seq, alpha):
    t, d = x.shape
    tm = MOE_TOK_TILE
    row = lambda i: (i, 0)
    return pl.pallas_call(
        functools.partial(_combine_kernel, tm=tm, alpha=alpha),
        grid=(t // tm,),
        in_specs=[
            pl.BlockSpec((TOP_K, tm), lambda i: (0, i), memory_space=pltpu.SMEM),
            pl.BlockSpec((tm, TOP_K), row),
            pl.BlockSpec((tm, d), row),
            _mod_spec(midx, seq // tm, d),
            pl.BlockSpec((1, d), lambda i: (0, 0)),
            pl.BlockSpec((1, d), lambda i: (0, 0)),
            pl.BlockSpec(memory_space=pl.ANY),
        ],
        out_specs=pl.BlockSpec((tm, d), row),
        out_shape=jax.ShapeDtypeStruct((t, d), F32),
        scratch_shapes=[pltpu.VMEM((TOP_K, tm * TILE_ROWS, LANES), F32), pltpu.SemaphoreType.DMA],
        compiler_params=_cparams(("arbitrary",)),
        name="moe_combine",
    )(dest, gates_col, x, mods, ln_g.reshape(1, d), ln_b.reshape(1, d), y_disp)


def _moe_block(x, mods, midx, router_w, router_b, layer, w_gu, b_gu, w_down, b_down, ln_g, ln_b, seq, alpha):
    t, d = x.shape
    assert d == TILE_ROWS * LANES
    rb = MOE_ROW_BLOCK
    nb = (t * TOP_K + N_EXPERTS * (rb - 1) + rb - 1) // rb
    h, eidx, rank, gates, counts = _route(x, mods, midx, router_w, router_b, seq)
    dest, block_table, expert_table, n_used = _plan(eidx, rank, counts, rb, nb)
    x_disp = _dispatch(h, dest, expert_table, nb * rb)
    y_disp = _expert_ffn(x_disp, block_table, n_used, layer, w_gu, b_gu, w_down, b_down, rb)
    return _combine_norm(y_disp, dest, gates.T, x, mods, midx, ln_g, ln_b, seq, alpha)


def _inproj_odd_kernel(x_ref, mod_ref, w_ref, wvt_ref, wf_ref, fb_ref, zs_ref, q_ref, k_ref, vt_ref, f_ref,
                       *, width):
    h = (x_ref[...] * (1.0 + mod_ref[1:2, :]) + mod_ref[0:1, :]).astype(BF16)
    z = jnp.dot(h, w_ref[...], preferred_element_type=F32)
    zs_ref[...] = z[:, :width]
    q_ref[...] = (z[:, width:2 * width] * (HEAD_DIM ** -0.5 * LOG2E)).astype(BF16)
    k_ref[...] = z[:, 2 * width:3 * width].astype(BF16)
    vt_ref[...] = lax.dot_general(wvt_ref[...], h, (((1,), (1,)), ((), ())),
                                  preferred_element_type=F32).astype(BF16)
    f_ref[...] = jnp.dot(h, wf_ref[...], preferred_element_type=F32) + fb_ref[...]


def _inproj_odd(x, mods, midx, w_main_bf, wvt_bf, wf_bf, fb, seq):
    t, d = x.shape
    width = wvt_bf.shape[0]
    tm = ROW_TILE
    row = lambda i: (i, 0)
    const = lambda i: (0, 0)
    return pl.pallas_call(
        functools.partial(_inproj_odd_kernel, width=width),
        grid=(t // tm,),
        in_specs=[
            pl.BlockSpec((tm, d), row),
            _mod_spec(midx, seq // tm, d),
            pl.BlockSpec((d, 3 * width), const),
            pl.BlockSpec((width, d), const),
            pl.BlockSpec((d, LANES), const),
            pl.BlockSpec((1, LANES), const),
        ],
        out_specs=[pl.BlockSpec((tm, width), row)] * 3 + [pl.BlockSpec((width, tm), lambda i: (0, i)),
                                                          pl.BlockSpec((tm, LANES), row)],
        out_shape=[jax.ShapeDtypeStruct((t, width), F32)] + [jax.ShapeDtypeStruct((t, width), BF16)] * 2
        + [jax.ShapeDtypeStruct((width, t), BF16), jax.ShapeDtypeStruct((t, LANES), F32)],
        compiler_params=_cparams(("parallel",)),
        name="inproj_odd",
    )(x, mods, w_main_bf, wvt_bf, wf_bf, fb)


def _fkeys_kernel(f_ref, k_ref, sel_ref, o_ref, carry_ref, *, ts, n_pairs):
    @pl.when(pl.program_id(1) == 0)
    def _():
        carry_ref[...] = jnp.zeros_like(carry_ref)

    x = f_ref[...]
    ls = jnp.minimum(x, 0.0) - jnp.log(1.0 + jnp.exp(-jnp.abs(x)))
    upto = (lax.broadcasted_iota(I32, (ts, ts), 1) <= lax.broadcasted_iota(I32, (ts, ts), 0)).astype(F32)
    cs = jnp.dot(upto, ls, precision=HIGHEST, preferred_element_type=F32) + carry_ref[...]
    carry_ref[...] = cs[ts - 1:ts, :]
    rest = -LOG2E * cs
    pieces = []
    for _ in range(FOX_BIAS_PARTS):
        piece = rest.astype(BF16)
        pieces.append(piece)
        rest = rest - piece.astype(F32)
    ext = jnp.dot(jnp.concatenate(pieces, axis=1), sel_ref[...], preferred_element_type=F32).astype(BF16)
    for p in range(n_pairs):
        o_ref[:, 2 * LANES * p:2 * LANES * p + LANES] = k_ref[:, LANES * p:LANES * (p + 1)]
        o_ref[:, 2 * LANES * p + LANES:2 * LANES * (p + 1)] = ext[:, LANES * p:LANES * (p + 1)]


def _forget_keys(flog, k, bsz, seq):
    t, width = k.shape
    n_pairs = width // LANES
    ts = FCUM_TILE
    nt = seq // ts
    sel = [[0.0] * width for _ in range(FOX_BIAS_PARTS * LANES)]
    for h in range(2 * n_pairs):
        for part in range(FOX_BIAS_PARTS):
            sel[part * LANES + h][(h // 2) * LANES + FOX_BIAS_PARTS * (h % 2) + part] = 1.0
    sel = jnp.asarray(sel, BF16)
    tile = lambda b, j: (b * nt + j, 0)
    return pl.pallas_call(
        functools.partial(_fkeys_kernel, ts=ts, n_pairs=n_pairs),
        grid=(bsz, nt),
        in_specs=[pl.BlockSpec((ts, LANES), tile), pl.BlockSpec((ts, width), tile),
                  pl.BlockSpec(sel.shape, lambda b, j: (0, 0))],
        out_specs=pl.BlockSpec((ts, 2 * width), tile),
        out_shape=jax.ShapeDtypeStruct((t, 2 * width), BF16),
        scratch_shapes=[pltpu.VMEM((1, LANES), F32)],
        compiler_params=_cparams(("parallel", "arbitrary")),
        name="forget_keys",
    )(flog, k, sel)


def _fox_kernel(q_ref, ka_ref, vt_ref, o_ref, acc_ref, m_ref, l_ref, qa_ref, st0, st1, p0, p1, al0, al1,
                *, tq, tk):
    st_refs, p_refs, al_refs = (st0, st1), (p0, p1), (al0, al1)
    n_full = pl.program_id(2)
    lane = lax.broadcasted_iota(I32, (tq, LANES), 1)
    qp = q_ref[...]
    for hh in range(2):
        mine = (lane < HEAD_DIM) if hh == 0 else (lane >= HEAD_DIM)
        ones = (lane >= FOX_BIAS_PARTS * hh) & (lane < FOX_BIAS_PARTS * (hh + 1))
        qa_ref[hh] = jnp.concatenate([jnp.where(mine, qp, jnp.zeros_like(qp)),
                                      jnp.where(ones, 1.0, 0.0).astype(BF16)], axis=1)
    acc_ref[...] = jnp.zeros_like(acc_ref)
    m_ref[...] = jnp.full_like(m_ref, NEG_INF)
    l_ref[...] = jnp.zeros_like(l_ref)

    def scores(kb, hh):
        off = pl.multiple_of(kb * tk, tk)
        return lax.dot_general(ka_ref[pl.ds(off, tk), :], qa_ref[hh], (((1,), (1,)), ((), ())),
                               preferred_element_type=F32)

    def softmax_step(st, hh):
        m_old = m_ref[hh:hh + 1, :]
        m_new = jnp.maximum(m_old, jnp.max(st, axis=0, keepdims=True))
        alpha = jnp.exp2(m_old - m_new)
        p = jnp.exp2(st - m_new)
        l_ref[hh:hh + 1, :] = alpha * l_ref[hh:hh + 1, :] + jnp.sum(p, axis=0, keepdims=True)
        m_ref[hh:hh + 1, :] = m_new
        return alpha, p.astype(BF16)

    def accumulate(kb, hh, alpha, p):
        off = pl.multiple_of(kb * tk, tk)
        rows = slice(HEAD_DIM * hh, HEAD_DIM * (hh + 1))
        acc_ref[rows, :] = alpha * acc_ref[rows, :] + jnp.dot(vt_ref[rows, pl.ds(off, tk)], p,
                                                               preferred_element_type=F32)

    keep = lax.broadcasted_iota(I32, (tk, tq), 0) <= lax.broadcasted_iota(I32, (tk, tq), 1)
    for hh in range(2):
        alpha, p = softmax_step(jnp.where(keep, scores(n_full, hh), NEG_INF), hh)
        accumulate(n_full, hh, alpha, p)

    def stage_a(kb, slot):
        for hh in range(2):
            st_refs[slot][hh] = scores(kb, hh)

    def stage_b(slot):
        for hh in range(2):
            alpha, p = softmax_step(st_refs[slot][hh], hh)
            al_refs[slot][hh:hh + 1, :] = alpha
            p_refs[slot][hh] = p

    def stage_c(kb, slot):
        for hh in range(2):
            accumulate(kb, hh, al_refs[slot][hh:hh + 1, :], p_refs[slot][hh])

    def step(i, slot):
        stage_a(i, slot)
        stage_b(1 - slot)
        stage_c(i - 2, slot)

    odd = lax.rem(n_full, 2) == 1

    @pl.when(n_full >= 1)
    def _():
        stage_a(0, 0)

    @pl.when(n_full >= 2)
    def _():
        stage_a(1, 1)
        stage_b(0)

    @pl.when(n_full == 1)
    def _():
        stage_b(0)
        stage_c(0, 0)

    def body(j, c):
        i = 2 + 2 * j
        step(i, 0)
        step(i + 1, 1)
        return c

    lax.fori_loop(0, (n_full - 2) // 2, body, 0)

    @pl.when(jnp.logical_and(n_full >= 2, jnp.logical_not(odd)))
    def _():
        stage_b(1)
        stage_c(n_full - 2, 0)
        stage_c(n_full - 1, 1)

    @pl.when(jnp.logical_and(n_full >= 3, odd))
    def _():
        step(n_full - 1, 0)
        stage_b(0)
        stage_c(n_full - 2, 1)
        stage_c(n_full - 1, 0)

    head0 = lax.broadcasted_iota(I32, (LANES, tq), 0) < HEAD_DIM
    out_t = acc_ref[...] / jnp.where(head0, l_ref[0:1, :], l_ref[1:2, :])
    o_ref[...] = out_t.T.astype(o_ref.dtype)


def _forgetting_attention(q, kaug, vt, bsz, seq):
    t, width = q.shape
    n_pairs = width // LANES
    tq, tk = FOX_TQ, FOX_TK
    nq = seq // tq
    return pl.pallas_call(
        functools.partial(_fox_kernel, tq=tq, tk=tk),
        grid=(bsz, n_pairs, nq),
        in_specs=[
            pl.BlockSpec((tq, LANES), lambda b, h, i: (b * nq + i, h)),
            pl.BlockSpec((seq, 2 * LANES), lambda b, h, i: (b, h)),
            pl.BlockSpec((LANES, seq), lambda b, h, i: (h, b)),
        ],
        out_specs=pl.BlockSpec((tq, LANES), lambda b, h, i: (b * nq + i, h)),
        out_shape=jax.ShapeDtypeStruct((t, width), BF16),
        scratch_shapes=[pltpu.VMEM((LANES, tq), F32), pltpu.VMEM((8, tq), F32), pltpu.VMEM((8, tq), F32),
                        pltpu.VMEM((2, tq, 2 * LANES), BF16)]
        + [pltpu.VMEM((2, tk, tq), F32)] * 2 + [pltpu.VMEM((2, tk, tq), BF16)] * 2 + [pltpu.VMEM((8, tq), F32)] * 2,
        compiler_params=_cparams(("parallel", "parallel", "arbitrary")),
        name="forgetting_attention",
    )(q, kaug, vt)


def _s5_prep_kernel(lre_ref, lim_ref, ldt_ref, bre_ref, bim_ref, are_ref, aim_ref, bb_re, bb_im):
    dt = jnp.exp(ldt_ref[...])
    lre = jnp.minimum(lre_ref[...], -1e-4)
    lim = lim_ref[...]
    mag = jnp.exp(lre * dt)
    a_re = mag * jnp.cos(lim * dt)
    a_im = mag * jnp.sin(lim * dt)
    den = lre * lre + lim * lim
    nre, nim = a_re - 1.0, a_im
    g_re = (nre * lre + nim * lim) / den
    g_im = (nim * lre - nre * lim) / den
    n = lre.shape[-1]
    per = n // bre_ref.shape[0]
    for j in range(bre_ref.shape[0]):
        gr = g_re[:, per * j:per * (j + 1)]
        gi = g_im[:, per * j:per * (j + 1)]
        bb_re[j] = (gr * bre_ref[j] - gi * bim_ref[j]).astype(bb_re.dtype)
        bb_im[j] = (gr * bim_ref[j] + gi * bre_ref[j]).astype(bb_im.dtype)
    are_ref[...] = a_re
    aim_ref[...] = a_im


def _block_diag_slabs(blocks, slabs):
    g, r, c = blocks.shape
    per = g // slabs
    eye = jnp.eye(per, dtype=bool)[None, :, None, :, None]
    bd = jnp.where(eye, blocks.reshape(slabs, per, r, 1, c), jnp.zeros((), blocks.dtype))
    return bd.reshape(slabs, per * r, per * c)


def _s5_kernel(u_ref, are_ref, aim_ref, bb_re, bb_im, cc_re, cc_im, dsk_ref, gw_ref, gb_ref, o_ref,
               xr0, xr1, xi0, xi1, carry_ref, *, ts, slabs):
    @pl.when(pl.program_id(1) == 0)
    def _():
        carry_ref[...] = jnp.zeros_like(carry_ref)

    xr, xi = (xr0, xr1), (xi0, xi1)
    per_half = TILE_ROWS // 2
    u = u_ref[...]
    ub = u.astype(BF16)
    for j in range(slabs):
        uj = ub[:, LANES * j:LANES * (j + 1)]
        bur = jnp.dot(uj, bb_re[j], preferred_element_type=F32)
        bui = jnp.dot(uj, bb_im[j], preferred_element_type=F32)
        half, base = j // 2, (j % 2) * per_half
        for s in range(per_half):
            xr[half][pl.ds(base + s, ts, stride=TILE_ROWS), :] = bur[:, LANES * s:LANES * (s + 1)]
            xi[half][pl.ds(base + s, ts, stride=TILE_ROWS), :] = bui[:, LANES * s:LANES * (s + 1)]
    a_r = [are_ref[TILE_ROWS * h:TILE_ROWS * (h + 1), :] for h in range(2)]
    a_i = [aim_ref[TILE_ROWS * h:TILE_ROWS * (h + 1), :] for h in range(2)]

    def steps(tb, state):
        state = list(state)
        for tt in range(TILE_ROWS):
            row = pl.multiple_of((tb * TILE_ROWS + tt) * TILE_ROWS, TILE_ROWS)
            for h in range(2):
                sr, si = state[h], state[2 + h]
                nr = a_r[h] * sr - a_i[h] * si + xr[h][pl.ds(row, TILE_ROWS), :]
                ni = a_r[h] * si + a_i[h] * sr + xi[h][pl.ds(row, TILE_ROWS), :]
                xr[h][pl.ds(row, TILE_ROWS), :] = nr
                xi[h][pl.ds(row, TILE_ROWS), :] = ni
                state[h], state[2 + h] = nr, ni
        return tuple(state)

    final = lax.fori_loop(0, ts // TILE_ROWS, steps, tuple(carry_ref[i] for i in range(4)))
    for i in range(4):
        carry_ref[i] = final[i]
    ys = []
    for j in range(slabs):
        half, base = j // 2, (j % 2) * per_half
        gather = lambda ref: jnp.concatenate(
            [ref[pl.ds(base + s, ts, stride=TILE_ROWS), :] for s in range(per_half)], axis=1).astype(BF16)
        yj = jnp.dot(gather(xr[half]), cc_re[j], preferred_element_type=F32)
        ys.append(yj - jnp.dot(gather(xi[half]), cc_im[j], preferred_element_type=F32))
    y = jnp.concatenate(ys, axis=1) + dsk_ref[...] * u
    y = 0.5 * y * (1.0 + jnp.tanh(math.sqrt(2.0 / math.pi) * (y + 0.044715 * (y * y * y))))
    gate = jnp.dot(y.astype(BF16), gw_ref[...], preferred_element_type=F32) + gb_ref[...]
    o_ref[...] = (y * jax.nn.sigmoid(gate)).astype(o_ref.dtype)


def _s5_mixer(u, lam_re, lam_im, log_dt, b_re, b_im, c_re, c_im, d_skip, glu_w, glu_b, bsz, seq):
    t, width = u.shape
    groups, states = lam_re.shape
    n = groups * states
    slabs = width // LANES
    ts = S5_TILE
    assert n == 2 * TILE_ROWS * LANES and slabs == 4
    flat = lambda a: a.reshape(1, n)
    ldt = jnp.repeat(log_dt, states).reshape(1, n)
    bre_bd = _block_diag_slabs(jnp.swapaxes(b_re, 1, 2), slabs)
    bim_bd = _block_diag_slabs(jnp.swapaxes(b_im, 1, 2), slabs)
    cre_bd = _block_diag_slabs(jnp.swapaxes(c_re, 1, 2), slabs).astype(BF16)
    cim_bd = _block_diag_slabs(jnp.swapaxes(c_im, 1, 2), slabs).astype(BF16)
    full = lambda a: pl.BlockSpec(a.shape, lambda *_: (0,) * a.ndim)
    prep_in = (flat(lam_re), flat(lam_im), ldt, bre_bd, bim_bd)
    a_re, a_im, bb_re, bb_im = pl.pallas_call(
        _s5_prep_kernel,
        grid=(1,),
        in_specs=[full(a) for a in prep_in],
        out_specs=[pl.BlockSpec((1, n), lambda i: (0, 0))] * 2 + [pl.BlockSpec(bre_bd.shape, lambda i: (0, 0, 0))] * 2,
        out_shape=[jax.ShapeDtypeStruct((1, n), F32)] * 2 + [jax.ShapeDtypeStruct(bre_bd.shape, BF16)] * 2,
        compiler_params=_cparams(("arbitrary",)),
        name="s5_prep",
    )(*prep_in)
    nt = seq // ts
    tiles = lambda a: a.reshape(n // LANES, LANES)
    consts = (tiles(a_re), tiles(a_im), bb_re, bb_im, cre_bd, cim_bd, d_skip.reshape(1, width),
              glu_w.astype(BF16), glu_b.reshape(1, width))
    return pl.pallas_call(
        functools.partial(_s5_kernel, ts=ts, slabs=slabs),
        grid=(bsz, nt),
        in_specs=[pl.BlockSpec((ts, width), lambda b, j: (b * nt + j, 0))] + [full(a) for a in consts],
        out_specs=pl.BlockSpec((ts, width), lambda b, j: (b * nt + j, 0)),
        out_shape=jax.ShapeDtypeStruct((t, width), BF16),
        scratch_shapes=[pltpu.VMEM((ts * TILE_ROWS, LANES), F32)] * 4 + [pltpu.VMEM((4, TILE_ROWS, LANES), F32)],
        compiler_params=_cparams(("parallel", "arbitrary")),
        name="s5_mixer",
    )(u, *consts)


def kernel(x, c, mod_w, mod_b, ln_g, ln_b, even_w_in, pool_w, pool_scale, rel_bias, even_w_out, odd_w_in, forget_b, ssm_lam_re, ssm_lam_im, ssm_log_dt, ssm_b_re, ssm_b_im, ssm_c_re, ssm_c_im, ssm_d, ssm_glu_w, ssm_glu_b, odd_w_out, router_w, router_b, exp_w_gu, exp_b_gu, exp_w_down, exp_b_down):
    bsz, seq, d = x.shape
    depth = mod_w.shape[0]
    alpha = (2.0 * depth) ** 0.25
    mods = _ada_mods(c, mod_w.reshape(depth * 2, d, 3 * d), mod_b.reshape(depth * 2, 3 * d))
    xt = x.reshape(bsz * seq, d)
    for layer in range(depth):
        i = layer // 2
        if layer % 2 == 0:
            w_in = even_w_in[i]
            width = w_in.shape[1] // 4
            zp, q, k, vt = _inproj_even(xt, mods, 2 * layer, w_in[:, :3 * width].astype(BF16),
                                        w_in[:, 3 * width:].T.astype(BF16), seq)
            ya = _pool_mixer(zp, pool_w[i].astype(BF16), pool_scale[i], seq)
            yb = _chunk_attention(q, k, vt, _cattn_bias_table(rel_bias[i]), bsz, seq)
            w_out = even_w_out[i]
        else:
            width = ssm_d.shape[1] * ssm_d.shape[2]
            w_in = odd_w_in[i]
            n_heads = forget_b.shape[1]
            wf = jnp.zeros((d, LANES), BF16).at[:, :n_heads].set(w_in[:, 4 * width:].astype(BF16))
            fb = jnp.zeros((1, LANES), F32).at[0, :n_heads].set(forget_b[i])
            zs, q, k, vt, flog = _inproj_odd(xt, mods, 2 * layer, w_in[:, :3 * width].astype(BF16),
                                             w_in[:, 3 * width:4 * width].T.astype(BF16), wf, fb, seq)
            ya = _s5_mixer(zs, ssm_lam_re[i], ssm_lam_im[i], ssm_log_dt[i], ssm_b_re[i], ssm_b_im[i],
                           ssm_c_re[i], ssm_c_im[i], ssm_d[i], ssm_glu_w[i], ssm_glu_b[i], bsz, seq)
            yb = _forgetting_attention(q, _forget_keys(flog, k, bsz, seq), vt, bsz, seq)
            w_out = odd_w_out[i]
        xt = _outproj_norm(ya, yb, xt, mods, 2 * layer, w_out.astype(BF16), ln_g[layer, 0], ln_b[layer, 0],
                           seq, alpha)
        xt = _moe_block(xt, mods, 2 * layer + 1, router_w[layer], router_b[layer], layer, exp_w_gu, exp_b_gu,
                        exp_w_down, exp_b_down, ln_g[layer, 1], ln_b[layer, 1], seq, alpha)
    return xt.reshape(bsz, seq, d)
```

```python
import functools
import math

import jax
import jax.numpy as jnp
from jax import lax
from jax.experimental import pallas as pl
from jax.experimental.pallas import tpu as pltpu

F32 = jnp.float32
BF16 = jnp.bfloat16
I32 = jnp.int32
HIGHEST = lax.Precision.HIGHEST

LANES = 128
SUBLANES = 8
VMEM_LIMIT_BYTES = 60000 * 1024

HEAD_DIM = 64
CHUNK = 64
LEFT_CHUNKS = 8
REL_CLIP = 128
POOL_WINDOWS = (2, 4, 8, 16)
POOL_HALO = 16
SSM_GROUP_DIM = 16
SSM_STATE = 64
N_EXPERTS = 32
TOP_K = 4
SWIGLU_LIMIT = 7.0
SWIGLU_ALPHA = 1.702
LN_EPS = 1e-5
NEG_INF = -1e30
LOG2E = math.log2(math.e)
FOX_BIAS_PARTS = 3

ROW_TILE = 512
CATTN_Q = 2 * CHUNK
CATTN_BAND = (LEFT_CHUNKS + 2) * CHUNK
FOX_TQ = 512
FOX_TK = 512
S5_TILE = 256
FCUM_TILE = 512
TILE_ROWS = SUBLANES
MOE_ROW_BLOCK = 256
MOE_TOK_TILE = 256
FILL_TILES = 128
ROUTE_TILE = 512


def _cparams(sem):
    return pltpu.CompilerParams(dimension_semantics=sem, vmem_limit_bytes=VMEM_LIMIT_BYTES)


def _layer_norm(r, g, b):
    mu = jnp.mean(r, axis=-1, keepdims=True)
    d = r - mu
    var = jnp.mean(d * d, axis=-1, keepdims=True)
    return d * lax.rsqrt(var + LN_EPS) * g + b


def _mod_kernel(c_ref, w_ref, b_ref, o_ref):
    c = c_ref[...]
    s = c * jax.nn.sigmoid(c)
    o_ref[0] = jnp.dot(s, w_ref[0], precision=HIGHEST, preferred_element_type=F32) + b_ref[0]


def _ada_mods(c, mod_w, mod_b):
    bsz, d = c.shape
    m = mod_w.shape[0]
    rows = 8
    c_pad = jnp.zeros((rows, d), F32).at[:bsz].set(c)
    out = pl.pallas_call(
        _mod_kernel,
        grid=(m, 3),
        in_specs=[
            pl.BlockSpec((rows, d), lambda i, j: (0, 0)),
            pl.BlockSpec((1, d, d), lambda i, j: (i, 0, j)),
            pl.BlockSpec((1, 1, d), lambda i, j: (i, 0, j)),
        ],
        out_specs=pl.BlockSpec((1, rows, d), lambda i, j: (i, 0, j)),
        out_shape=jax.ShapeDtypeStruct((m, rows, 3 * d), F32),
        compiler_params=_cparams(("parallel", "parallel")),
        name="ada_mod",
    )(c_pad, mod_w, mod_b.reshape(m, 1, 3 * d))
    return out[:, :bsz].reshape(m, bsz, 3, d)


def _mod_spec(midx, tiles_per_batch, d):
    return pl.BlockSpec((None, None, 3, d), lambda i: (midx, i // tiles_per_batch, 0, 0))


def _inproj_even_kernel(x_ref, mod_ref, w_ref, wvt_ref, zp_ref, q_ref, k_ref, vt_ref, *, width):
    h = (x_ref[...] * (1.0 + mod_ref[1:2, :]) + mod_ref[0:1, :]).astype(BF16)
    z = jnp.dot(h, w_ref[...], preferred_element_type=F32)
    zp_ref[...] = z[:, :width]
    q_ref[...] = (z[:, width:2 * width] * (HEAD_DIM ** -0.5 * LOG2E)).astype(BF16)
    k_ref[...] = z[:, 2 * width:3 * width].astype(BF16)
    vt_ref[...] = lax.dot_general(wvt_ref[...], h, (((1,), (1,)), ((), ())),
                                  preferred_element_type=F32).astype(BF16)


def _inproj_even(x, mods, midx, w_main_bf, wvt_bf, seq):
    t, d = x.shape
    width = wvt_bf.shape[0]
    tm = ROW_TILE
    row = lambda i: (i, 0)
    return pl.pallas_call(
        functools.partial(_inproj_even_kernel, width=width),
        grid=(t // tm,),
        in_specs=[
            pl.BlockSpec((tm, d), row),
            _mod_spec(midx, seq // tm, d),
            pl.BlockSpec((d, 3 * width), lambda i: (0, 0)),
            pl.BlockSpec((width, d), lambda i: (0, 0)),
        ],
        out_specs=[pl.BlockSpec((tm, width), row)] * 3
        + [pl.BlockSpec((None, width, tm), lambda i: (i // (seq // tm), 0, i % (seq // tm)))],
        out_shape=[jax.ShapeDtypeStruct((t, width), F32)] + [jax.ShapeDtypeStruct((t, width), BF16)] * 2
        + [jax.ShapeDtypeStruct((t // seq, width, seq), BF16)],
        compiler_params=_cparams(("parallel",)),
        name="inproj_even",
    )(x, mods, w_main_bf, wvt_bf)


def _pool_kernel(zp_ref, halo_ref, w_ref, sc_ref, o_ref, xs_ref, *, ts, seq):
    i = pl.program_id(0)
    t0 = lax.rem(i * ts, seq)
    xs_ref[0:POOL_HALO, :] = jnp.where(t0 == 0, 0.0, halo_ref[...])
    xs_ref[POOL_HALO:POOL_HALO + ts, :] = zp_ref[...]
    pos = lax.broadcasted_iota(I32, (ts, LANES), 0) + t0
    for g, win in enumerate(POOL_WINDOWS):
        cols = slice(LANES * g, LANES * (g + 1))
        u = xs_ref[POOL_HALO:POOL_HALO + ts, cols]
        acc = u
        for k in range(1, win):
            acc = acc + xs_ref[POOL_HALO - k:POOL_HALO - k + ts, cols]
        cnt = jnp.minimum(pos + 1, win).astype(F32)
        dlt = acc / cnt - u
        y = jnp.dot(dlt.astype(BF16), w_ref[g], preferred_element_type=F32)
        o_ref[:, cols] = (y * sc_ref[:, cols]).astype(o_ref.dtype)


def _pool_mixer(zp, w_pool_bf, pool_scale, seq):
    t, width = zp.shape
    ts = ROW_TILE
    hb = ts // POOL_HALO
    return pl.pallas_call(
        functools.partial(_pool_kernel, ts=ts, seq=seq),
        grid=(t // ts,),
        in_specs=[
            pl.BlockSpec((ts, width), lambda i: (i, 0)),
            pl.BlockSpec((POOL_HALO, width), lambda i: (jnp.maximum(i * hb - 1, 0), 0)),
            pl.BlockSpec(w_pool_bf.shape, lambda i: (0, 0, 0)),
            pl.BlockSpec((1, width), lambda i: (0, 0)),
        ],
        out_specs=pl.BlockSpec((ts, width), lambda i: (i, 0)),
        out_shape=jax.ShapeDtypeStruct((t, width), BF16),
        scratch_shapes=[pltpu.VMEM((ts + POOL_HALO, width), F32)],
        compiler_params=_cparams(("parallel",)),
        name="pool_mixer",
    )(zp, zp, w_pool_bf, pool_scale.reshape(1, width))


def _cbias_kernel(r_ref, o_ref):
    base = jnp.broadcast_to(r_ref[...], (CATTN_Q, r_ref.shape[-1]))
    rolled = pltpu.roll(base, 0, 1, stride=1, stride_axis=0)[:, :CATTN_BAND]
    qi = lax.broadcasted_iota(I32, (CATTN_Q, CATTN_BAND), 0)
    kp = lax.broadcasted_iota(I32, (CATTN_Q, CATTN_BAND), 1)
    lo = (qi // CHUNK) * CHUNK
    valid = (kp >= lo) & (kp < lo + (LEFT_CHUNKS + 1) * CHUNK)
    o_ref[...] = jnp.where(valid, rolled * LOG2E, NEG_INF).T


def _cattn_bias_table(rel_bias):
    h = rel_bias.shape[0]
    wide = 1024
    far = LEFT_CHUNKS * CHUNK - REL_CLIP + 1
    tail = wide - far - (2 * REL_CLIP - 1)
    last = rel_bias[:, 2 * REL_CLIP:]
    base = jnp.concatenate(
        [jnp.broadcast_to(last, (h, far)), rel_bias[:, 1:2 * REL_CLIP][:, ::-1], jnp.broadcast_to(last, (h, tail))],
        axis=1).reshape(h, 1, wide)
    return pl.pallas_call(
        _cbias_kernel,
        grid=(h,),
        in_specs=[pl.BlockSpec((None, 1, wide), lambda i: (i, 0, 0))],
        out_specs=pl.BlockSpec((None, CATTN_BAND, CATTN_Q), lambda i: (i, 0, 0)),
        out_shape=jax.ShapeDtypeStruct((h, CATTN_BAND, CATTN_Q), F32),
        compiler_params=_cparams(("parallel",)),
        name="cattn_bias",
    )(base)


def _cattn_kernel(q_ref, k_ref, vt_ref, bias_ref, o_ref, st_ref, e_ref, l_ref, m_ref, *, n_pairs):
    p = pl.program_id(1)
    start = pl.multiple_of(p * CATTN_Q, CATTN_Q)
    kpos = lax.broadcasted_iota(I32, (CATTN_BAND, CATTN_Q), 0) + start
    real_key = kpos >= LEFT_CHUNKS * CHUNK
    lane = lax.broadcasted_iota(I32, (CATTN_Q, LANES), 1)
    n_heads = 2 * n_pairs
    for h in range(n_heads):
        cols = slice(LANES * (h // 2), LANES * (h // 2 + 1))
        qp = q_ref[:, cols]
        mine = (lane < HEAD_DIM) if h % 2 == 0 else (lane >= HEAD_DIM)
        qm = jnp.where(mine, qp, jnp.zeros_like(qp))
        st = lax.dot_general(k_ref[pl.ds(start, CATTN_BAND), cols], qm, (((1,), (1,)), ((), ())),
                             preferred_element_type=F32)
        st = jnp.where(real_key, st + bias_ref[h], NEG_INF)
        st_ref[h] = st
        m_ref[h:h + 1, :] = jnp.max(st, axis=0, keepdims=True)
    for h in range(n_heads):
        e = jnp.exp2(st_ref[h] - m_ref[h:h + 1, :])
        l_ref[h:h + 1, :] = jnp.sum(e, axis=0, keepdims=True)
        e_ref[h] = e.astype(BF16)
    for hp in range(n_pairs):
        outs = []
        for hh in range(2):
            h = 2 * hp + hh
            vth = vt_ref[HEAD_DIM * h:HEAD_DIM * (h + 1), pl.ds(start, CATTN_BAND)]
            outs.append(jnp.dot(vth, e_ref[h], preferred_element_type=F32) / l_ref[h:h + 1, :])
        o_ref[:, LANES * hp:LANES * (hp + 1)] = jnp.concatenate(outs, axis=0).T.astype(o_ref.dtype)


def _chunk_attention(q, k, vt, bias_t, bsz, seq):
    t, width = q.shape
    n_pairs = width // LANES
    pad = LEFT_CHUNKS * CHUNK
    kp = jnp.pad(k.reshape(bsz, seq, width), ((0, 0), (pad, 0), (0, 0)))
    vtp = jnp.pad(vt, ((0, 0), (0, 0), (pad, 0)))
    nq = seq // CATTN_Q
    return pl.pallas_call(
        functools.partial(_cattn_kernel, n_pairs=n_pairs),
        grid=(bsz, nq),
        in_specs=[
            pl.BlockSpec((CATTN_Q, width), lambda b, p: (b * nq + p, 0)),
            pl.BlockSpec((None, seq + pad, width), lambda b, p: (b, 0, 0)),
            pl.BlockSpec((None, width, seq + pad), lambda b, p: (b, 0, 0)),
            pl.BlockSpec(bias_t.shape, lambda b, p: (0, 0, 0)),
        ],
        out_specs=pl.BlockSpec((CATTN_Q, width), lambda b, p: (b * nq + p, 0)),
        out_shape=jax.ShapeDtypeStruct((t, width), BF16),
        scratch_shapes=[pltpu.VMEM((2 * n_pairs, CATTN_BAND, CATTN_Q), F32),
                        pltpu.VMEM((2 * n_pairs, CATTN_BAND, CATTN_Q), BF16),
                        pltpu.VMEM((2 * n_pairs, CATTN_Q), F32), pltpu.VMEM((2 * n_pairs, CATTN_Q), F32)],
        compiler_params=_cparams(("parallel", "arbitrary")),
        name="chunk_attention",
    )(q, kp, vtp, bias_t)


def _outproj_kernel(ya_ref, yb_ref, x_ref, mod_ref, w_ref, g_ref, b_ref, o_ref, *, alpha, half):
    y = jnp.dot(ya_ref[...], w_ref[0:half, :], preferred_element_type=F32)
    y = y + jnp.dot(yb_ref[...], w_ref[half:2 * half, :], preferred_element_type=F32)
    r = alpha * x_ref[...] + (1.0 + mod_ref[2:3, :]) * y
    o_ref[...] = _layer_norm(r, g_ref[...], b_ref[...])


def _outproj_norm(ya, yb, x, mods, midx, w_out_bf, ln_g, ln_b, seq, alpha):
    t, d = x.shape
    half = ya.shape[1]
    tm = ROW_TILE
    row = lambda i: (i, 0)
    return pl.pallas_call(
        functools.partial(_outproj_kernel, alpha=alpha, half=half),
        grid=(t // tm,),
        in_specs=[
            pl.BlockSpec((tm, half), row),
            pl.BlockSpec((tm, half), row),
            pl.BlockSpec((tm, d), row),
            _mod_spec(midx, seq // tm, d),
            pl.BlockSpec((2 * half, d), lambda i: (0, 0)),
            pl.BlockSpec((1, d), lambda i: (0, 0)),
            pl.BlockSpec((1, d), lambda i: (0, 0)),
        ],
        out_specs=pl.BlockSpec((tm, d), row),
        out_shape=jax.ShapeDtypeStruct((t, d), F32),
        compiler_params=_cparams(("parallel",)),
        name="outproj_norm",
    )(ya, yb, x, mods, w_out_bf, ln_g.reshape(1, d), ln_b.reshape(1, d))


def _route_kernel(x_ref, mod_ref, rw_ref, rb_ref, h_ref, e_ref, r_ref, g_ref, c_ref, carry_ref, *, tm):
    @pl.when(pl.program_id(0) == 0)
    def _():
        carry_ref[...] = jnp.zeros_like(carry_ref)

    h = x_ref[...] * (1.0 + mod_ref[1:2, :]) + mod_ref[0:1, :]
    _store_token_tiles(h_ref, h)
    logits = lax.dot_general(rw_ref[...], h, (((1,), (1,)), ((), ())), precision=HIGHEST,
                             preferred_element_type=F32) + rb_ref[...]
    eio = lax.broadcasted_iota(I32, (N_EXPERTS, tm), 0)
    vals, hots = [], []
    for k in range(TOP_K):
        m = jnp.max(logits, axis=0, keepdims=True)
        idx = jnp.min(jnp.where(logits == m, eio, N_EXPERTS), axis=0, keepdims=True)
        hot = eio == idx
        e_ref[k:k + 1, :] = idx
        vals.append(m)
        hots.append(hot)
        logits = jnp.where(hot, -jnp.inf, logits)
    exps = [jnp.exp(v - vals[0]) for v in vals]
    denom = exps[0] + exps[1] + exps[2] + exps[3]
    for k in range(TOP_K):
        g_ref[k:k + 1, :] = exps[k] / denom
    cnt = jnp.zeros((N_EXPERTS, tm), F32)
    for hot in hots:
        cnt = cnt + hot.astype(F32)
    before = (lax.broadcasted_iota(I32, (tm, tm), 0) < lax.broadcasted_iota(I32, (tm, tm), 1))
    prefix = jnp.dot(cnt.astype(BF16), before.astype(BF16), preferred_element_type=F32)
    tot = prefix + carry_ref[...]
    for k in range(TOP_K):
        r_ref[k:k + 1, :] = jnp.sum(jnp.where(hots[k], tot, 0.0), axis=0, keepdims=True).astype(I32)
    new_carry = carry_ref[...] + jnp.sum(cnt, axis=1, keepdims=True)
    carry_ref[...] = new_carry
    c_ref[...] = jnp.broadcast_to(new_carry, c_ref.shape)


def _route(x, mods, midx, router_w, router_b, seq):
    t, d = x.shape
    tm = ROUTE_TILE
    tok = lambda i: (0, i)
    return pl.pallas_call(
        functools.partial(_route_kernel, tm=tm),
        grid=(t // tm,),
        in_specs=[
            pl.BlockSpec((tm, d), lambda i: (i, 0)),
            _mod_spec(midx, seq // tm, d),
            pl.BlockSpec((N_EXPERTS, d), lambda i: (0, 0)),
            pl.BlockSpec((N_EXPERTS, 1), lambda i: (0, 0)),
        ],
        out_specs=[
            pl.BlockSpec((tm * TILE_ROWS, LANES), lambda i: (i, 0)),
            pl.BlockSpec((TOP_K, tm), tok),
            pl.BlockSpec((TOP_K, tm), tok),
            pl.BlockSpec((TOP_K, tm), tok),
            pl.BlockSpec((N_EXPERTS, LANES), lambda i: (0, 0)),
        ],
        out_shape=[
            jax.ShapeDtypeStruct((t * TILE_ROWS, LANES), F32),
            jax.ShapeDtypeStruct((TOP_K, t), I32),
            jax.ShapeDtypeStruct((TOP_K, t), I32),
            jax.ShapeDtypeStruct((TOP_K, t), F32),
            jax.ShapeDtypeStruct((N_EXPERTS, LANES), F32),
        ],
        scratch_shapes=[pltpu.VMEM((N_EXPERTS, 1), F32)],
        compiler_params=_cparams(("arbitrary",)),
        name="moe_route",
    )(x, mods, router_w.T, router_b.reshape(N_EXPERTS, 1))


def _plan_kernel(e_ref, r_ref, ccol_ref, crow_ref, d_ref, blk_ref, exp_ref, nu_ref, *, tm, rb, nbp):
    inv = 1.0 / rb
    pad_col = jnp.floor((ccol_ref[...] + (rb - 1)) * inv) * rb
    pad_row = jnp.floor((crow_ref[0:1, :] + (rb - 1)) * inv) * rb
    ei = lax.broadcasted_iota(I32, (N_EXPERTS, LANES), 0)
    li = lax.broadcasted_iota(I32, (N_EXPERTS, LANES), 1)
    pstart = jnp.sum(jnp.where(li < ei, jnp.broadcast_to(pad_row, (N_EXPERTS, LANES)), 0.0),
                     axis=1, keepdims=True)
    padded = pad_col[:, 0:1]
    pend = pstart + padded
    eio = lax.broadcasted_iota(I32, (N_EXPERTS, tm), 0)
    for k in range(TOP_K):
        base = jnp.sum(jnp.where(eio == e_ref[k:k + 1, :], pstart, 0.0), axis=0, keepdims=True)
        d_ref[k:k + 1, :] = base.astype(I32) + r_ref[k:k + 1, :]
    row0 = (lax.broadcasted_iota(I32, (N_EXPERTS, nbp), 1) * rb).astype(F32)
    be = jnp.minimum(jnp.sum((pend <= row0).astype(F32), axis=0, keepdims=True), N_EXPERTS - 1.0)
    ebi = lax.broadcasted_iota(I32, (N_EXPERTS, nbp), 0).astype(F32)
    mine = ebi == be
    my_end = jnp.sum(jnp.where(mine, pend, 0.0), axis=0, keepdims=True)
    nxt = jnp.sum((pend <= my_end).astype(F32), axis=0, keepdims=True)
    order = jnp.sum(jnp.where(jnp.logical_and(padded > 0.0, ebi < be), 1.0, 0.0), axis=0, keepdims=True)
    blk_ref[0:1, :] = be.astype(I32)
    blk_ref[1:2, :] = jnp.where(nxt < N_EXPERTS, nxt, -1.0).astype(I32)
    blk_ref[2:3, :] = (order - 2.0 * jnp.floor(order * 0.5)).astype(I32)
    blk_ref[3:8, :] = jnp.zeros((5, nbp), I32)
    pstart_row = jnp.sum(jnp.where(ei < li, pad_col, 0.0), axis=0, keepdims=True)
    exp_ref[0:1, :] = crow_ref[0:1, :].astype(I32)
    exp_ref[1:2, :] = pstart_row.astype(I32)
    exp_ref[2:3, :] = pad_row.astype(I32)
    exp_ref[3:8, :] = jnp.zeros((5, LANES), I32)
    used = jnp.sum(padded, axis=0, keepdims=True) * inv
    nu_ref[...] = jnp.broadcast_to(used, nu_ref.shape).astype(I32)


def _plan(eidx, rank, counts_col, rb, nb):
    t = eidx.shape[1]
    tm = ROUTE_TILE
    nbp = -(-nb // LANES) * LANES
    counts_row = jnp.zeros((8, LANES), F32).at[:, :N_EXPERTS].set(counts_col[:, 0][None, :])
    tok = lambda i: (0, i)
    dest, blk, exp, nu = pl.pallas_call(
        functools.partial(_plan_kernel, tm=tm, rb=rb, nbp=nbp),
        grid=(t // tm,),
        in_specs=[
            pl.BlockSpec((TOP_K, tm), tok),
            pl.BlockSpec((TOP_K, tm), tok),
            pl.BlockSpec((N_EXPERTS, LANES), lambda i: (0, 0)),
            pl.BlockSpec((8, LANES), lambda i: (0, 0)),
        ],
        out_specs=[
            pl.BlockSpec((TOP_K, tm), tok),
            pl.BlockSpec((8, nbp), lambda i: (0, 0)),
            pl.BlockSpec((8, LANES), lambda i: (0, 0)),
            pl.BlockSpec((1, LANES), lambda i: (0, 0)),
        ],
        out_shape=[
            jax.ShapeDtypeStruct((TOP_K, t), I32),
            jax.ShapeDtypeStruct((8, nbp), I32),
            jax.ShapeDtypeStruct((8, LANES), I32),
            jax.ShapeDtypeStruct((1, LANES), I32),
        ],
        compiler_params=_cparams(("arbitrary",)),
        name="moe_plan",
    )(eidx, rank, counts_col, counts_row)
    return dest, (blk[0], blk[1], blk[2]), (exp[0], exp[1], exp[2]), nu[0, :1]


def _store_token_tiles(ref, x):
    n = x.shape[0]
    for s in range(TILE_ROWS):
        ref[pl.ds(s, n, stride=TILE_ROWS), :] = x[:, LANES * s:LANES * (s + 1)]


def _load_token_tiles(ref, n):
    return jnp.concatenate([ref[pl.ds(s, n, stride=TILE_ROWS), :] for s in range(TILE_ROWS)], axis=1)


def _tile_copy(src_ref, src_tok, dst_ref, dst_tok, sem):
    src = src_ref.at[pl.ds(pl.multiple_of(src_tok * TILE_ROWS, TILE_ROWS), TILE_ROWS)]
    dst = dst_ref.at[pl.ds(pl.multiple_of(dst_tok * TILE_ROWS, TILE_ROWS), TILE_ROWS)]
    return pltpu.make_async_copy(src, dst, sem)


def _dispatch_kernel(cnt_ref, first_ref, padded_ref, h_ref, dest_ref, xout_ref, zero_ref, sem, zsem, *, tm, n_rows):
    i = pl.program_id(0)
    steps = pl.num_programs(0)

    def fill_copy(row, size):
        return pltpu.make_async_copy(zero_ref.at[pl.ds(0, size * TILE_ROWS)],
                                     xout_ref.at[pl.ds(pl.multiple_of(row * TILE_ROWS, TILE_ROWS), size * TILE_ROWS)],
                                     zsem)

    def each_fill(act):
        def per_expert(e, c):
            lo = first_ref[e] + cnt_ref[e]
            n = jnp.where(e == N_EXPERTS - 1, n_rows, first_ref[e] + padded_ref[e]) - lo
            whole, rem = n // FILL_TILES, lax.rem(n, FILL_TILES)

            def big(j, c2):
                act(fill_copy(lo + j * FILL_TILES, FILL_TILES))
                return c2

            lax.fori_loop(0, whole, big, 0)
            size = FILL_TILES // 2
            while size >= 1:
                @pl.when((rem & size) != 0)
                def _(size=size):
                    act(fill_copy(lo + whole * FILL_TILES + (rem & ~(2 * size - 1)), size))
                size //= 2
            return c

        lax.fori_loop(0, N_EXPERTS, per_expert, 0)

    @pl.when(i == 0)
    def _():
        zero_ref[...] = jnp.zeros_like(zero_ref)
        each_fill(lambda copy: copy.start())

    def issue(r, c):
        for k in range(TOP_K):
            _tile_copy(h_ref, r, xout_ref, dest_ref[k, r], sem).start(priority=k % 2)
        return c

    lax.fori_loop(0, tm, issue, 0)

    def drain(r, c):
        for k in range(TOP_K):
            _tile_copy(h_ref, 0, xout_ref, 0, sem).wait()
        return c

    lax.fori_loop(0, tm, drain, 0)

    @pl.when(i == steps - 1)
    def _():
        each_fill(lambda copy: copy.wait())


def _dispatch(h_tiles, dest, expert_table, n_rows):
    t = h_tiles.shape[0] // TILE_ROWS
    tm = MOE_TOK_TILE
    return pl.pallas_call(
        functools.partial(_dispatch_kernel, tm=tm, n_rows=n_rows),
        grid_spec=pltpu.PrefetchScalarGridSpec(
            num_scalar_prefetch=3,
            grid=(t // tm,),
            in_specs=[
                pl.BlockSpec((tm * TILE_ROWS, LANES), lambda i, *_: (i, 0)),
                pl.BlockSpec((TOP_K, tm), lambda i, *_: (0, i), memory_space=pltpu.SMEM),
            ],
            out_specs=pl.BlockSpec(memory_space=pl.ANY),
            scratch_shapes=[pltpu.VMEM((FILL_TILES * TILE_ROWS, LANES), h_tiles.dtype), pltpu.SemaphoreType.DMA,
                            pltpu.SemaphoreType.DMA],
        ),
        out_shape=jax.ShapeDtypeStruct((n_rows * TILE_ROWS, LANES), h_tiles.dtype),
        compiler_params=_cparams(("arbitrary",)),
        name="moe_dispatch",
    )(*expert_table, h_tiles, dest)


def _expert_kernel(be_ref, nx_ref, slot_ref, nu_ref, x_ref, bgu_ref, bd_ref, wgu_hbm, wd_hbm, o_ref,
                   wgu_f, wd_f, wgu_s, wd_s, sems, *, dff, rb, layer):
    b = pl.program_id(0)
    e = be_ref[b]
    slot = slot_ref[b]
    used = b < nu_ref[0]
    first = jnp.logical_and(used, jnp.logical_or(b == 0, e != be_ref[jnp.maximum(b - 1, 0)]))

    def fetch(expert, s):
        return (pltpu.make_async_copy(wgu_hbm.at[layer, expert], wgu_f.at[s], sems.at[0, s]),
                pltpu.make_async_copy(wd_hbm.at[layer, expert], wd_f.at[s], sems.at[1, s]))

    @pl.when(b == 0)
    def _():
        for copy in fetch(e, slot):
            copy.start()

    @pl.when(first)
    def _():
        for copy in fetch(e, slot):
            copy.wait()
        wgu_s[...] = wgu_f[slot].astype(BF16)
        wd_s[...] = wd_f[slot].astype(BF16)

        @pl.when(nx_ref[b] >= 0)
        def _():
            for copy in fetch(nx_ref[b], 1 - slot):
                copy.start()

    @pl.when(used)
    def _():
        x = _load_token_tiles(x_ref, rb)
        gu = jnp.dot(x.astype(BF16), wgu_s[...], preferred_element_type=F32) + bgu_ref[0]
        gate = jnp.minimum(gu[:, :dff], SWIGLU_LIMIT)
        up = jnp.clip(gu[:, dff:], -SWIGLU_LIMIT, SWIGLU_LIMIT)
        act = (up + 1.0) * (gate * jax.nn.sigmoid(SWIGLU_ALPHA * gate))
        _store_token_tiles(o_ref, jnp.dot(act.astype(BF16), wd_s[...], preferred_element_type=F32) + bd_ref[0])

    @pl.when(jnp.logical_not(used))
    def _():
        o_ref[...] = jnp.zeros_like(o_ref)


def _expert_ffn(x_disp, block_table, n_used, layer, w_gu, b_gu, w_down, b_down, rb):
    nl, ne, d, dff2 = w_gu.shape
    dff = dff2 // 2
    nb = x_disp.shape[0] // (rb * TILE_ROWS)
    pick = lambda b, be, *_: (layer, be[b], 0, 0)
    blocks = lambda b, *_: (b, 0)
    return pl.pallas_call(
        functools.partial(_expert_kernel, dff=dff, rb=rb, layer=layer),
        grid_spec=pltpu.PrefetchScalarGridSpec(
            num_scalar_prefetch=4,
            grid=(nb,),
            in_specs=[
                pl.BlockSpec((rb * TILE_ROWS, LANES), blocks),
                pl.BlockSpec((None, 1, 1, dff2), pick),
                pl.BlockSpec((None, 1, 1, d), pick),
                pl.BlockSpec(memory_space=pl.ANY),
                pl.BlockSpec(memory_space=pl.ANY),
            ],
            out_specs=pl.BlockSpec((rb * TILE_ROWS, LANES), blocks),
            scratch_shapes=[pltpu.VMEM((2, d, dff2), F32), pltpu.VMEM((2, dff, d), F32),
                            pltpu.VMEM((d, dff2), BF16), pltpu.VMEM((dff, d), BF16),
                            pltpu.SemaphoreType.DMA((2, 2))],
        ),
        out_shape=jax.ShapeDtypeStruct(x_disp.shape, F32),
        compiler_params=_cparams(("arbitrary",)),
        name="moe_experts",
    )(*block_table, n_used, x_disp, b_gu.reshape(nl, ne, 1, dff2), b_down.reshape(nl, ne, 1, d), w_gu, w_down)


def _combine_kernel(dest_ref, gates_ref, x_ref, mod_ref, g_ref, b_ref, y_hbm, o_ref, buf, sem, *, tm, alpha):
    def issue(r, c):
        for k in range(TOP_K):
            _tile_copy(y_hbm, dest_ref[k, r], buf.at[k], r, sem).start(priority=k % 2)
        return c

    lax.fori_loop(0, tm, issue, 0)

    def drain(r, c):
        for k in range(TOP_K):
            _tile_copy(y_hbm, 0, buf.at[k], 0, sem).wait()
        return c

    lax.fori_loop(0, tm, drain, 0)
    y = gates_ref[:, 0:1] * _load_token_tiles(buf.at[0], tm)
    for k in range(1, TOP_K):
        y = y + gates_ref[:, k:k + 1] * _load_token_tiles(buf.at[k], tm)
    r = alpha * x_ref[...] + (1.0 + mod_ref[2:3, :]) * y
    o_ref[...] = _layer_norm(r, g_ref[...], b_ref[...])


def _combine_norm(y_disp, dest, gates_col, x, mods, midx, ln_g, ln_b, seq, alpha):
    t, d = x.shape
    tm = MOE_TOK_TILE
    row = lambda i: (i, 0)
    return pl.pallas_call(
        functools.partial(_combine_kernel, tm=tm, alpha=alpha),
        grid=(t // tm,),
        in_specs=[
            pl.BlockSpec((TOP_K, tm), lambda i: (0, i), memory_space=pltpu.SMEM),
            pl.BlockSpec((tm, TOP_K), row),
            pl.BlockSpec((tm, d), row),
            _mod_spec(midx, seq // tm, d),
            pl.BlockSpec((1, d), lambda i: (0, 0)),
            pl.BlockSpec((1, d), lambda i: (0, 0)),
            pl.BlockSpec(memory_space=pl.ANY),
        ],
        out_specs=pl.BlockSpec((tm, d), row),
        out_shape=jax.ShapeDtypeStruct((t, d), F32),
        scratch_shapes=[pltpu.VMEM((TOP_K, tm * TILE_ROWS, LANES), F32), pltpu.SemaphoreType.DMA],
        compiler_params=_cparams(("arbitrary",)),
        name="moe_combine",
    )(dest, gates_col, x, mods, ln_g.reshape(1, d), ln_b.reshape(1, d), y_disp)


def _moe_block(x, mods, midx, router_w, router_b, layer, w_gu, b_gu, w_down, b_down, ln_g, ln_b, seq, alpha):
    t, d = x.shape
    assert d == TILE_ROWS * LANES
    rb = MOE_ROW_BLOCK
    nb = (t * TOP_K + N_EXPERTS * (rb - 1) + rb - 1) // rb
    h, eidx, rank, gates, counts = _route(x, mods, midx, router_w, router_b, seq)
    dest, block_table, expert_table, n_used = _plan(eidx, rank, counts, rb, nb)
    x_disp = _dispatch(h, dest, expert_table, nb * rb)
    y_disp = _expert_ffn(x_disp, block_table, n_used, layer, w_gu, b_gu, w_down, b_down, rb)
    return _combine_norm(y_disp, dest, gates.T, x, mods, midx, ln_g, ln_b, seq, alpha)


def _inproj_odd_kernel(x_ref, mod_ref, w_ref, wvt_ref, wf_ref, fb_ref, zs_ref, q_ref, k_ref, vt_ref, f_ref,
                       *, width):
    h = (x_ref[...] * (1.0 + mod_ref[1:2, :]) + mod_ref[0:1, :]).astype(BF16)
    z = jnp.dot(h, w_ref[...], preferred_element_type=F32)
    zs_ref[...] = z[:, :width]
    q_ref[...] = (z[:, width:2 * width] * (HEAD_DIM ** -0.5 * LOG2E)).astype(BF16)
    k_ref[...] = z[:, 2 * width:3 * width].astype(BF16)
    vt_ref[...] = lax.dot_general(wvt_ref[...], h, (((1,), (1,)), ((), ())),
                                  preferred_element_type=F32).astype(BF16)
    f_ref[...] = jnp.dot(h, wf_ref[...], preferred_element_type=F32) + fb_ref[...]


def _inproj_odd(x, mods, midx, w_main_bf, wvt_bf, wf_bf, fb, seq):
    t, d = x.shape
    width = wvt_bf.shape[0]
    tm = ROW_TILE
    row = lambda i: (i, 0)
    const = lambda i: (0, 0)
    return pl.pallas_call(
        functools.partial(_inproj_odd_kernel, width=width),
        grid=(t // tm,),
        in_specs=[
            pl.BlockSpec((tm, d), row),
            _mod_spec(midx, seq // tm, d),
            pl.BlockSpec((d, 3 * width), const),
            pl.BlockSpec((width, d), const),
            pl.BlockSpec((d, LANES), const),
            pl.BlockSpec((1, LANES), const),
        ],
        out_specs=[pl.BlockSpec((tm, width), row)] * 3 + [pl.BlockSpec((width, tm), lambda i: (0, i)),
                                                          pl.BlockSpec((tm, LANES), row)],
        out_shape=[jax.ShapeDtypeStruct((t, width), F32)] + [jax.ShapeDtypeStruct((t, width), BF16)] * 2
        + [jax.ShapeDtypeStruct((width, t), BF16), jax.ShapeDtypeStruct((t, LANES), F32)],
        compiler_params=_cparams(("parallel",)),
        name="inproj_odd",
    )(x, mods, w_main_bf, wvt_bf, wf_bf, fb)


def _fkeys_kernel(f_ref, k_ref, sel_ref, o_ref, carry_ref, *, ts, n_pairs):
    @pl.when(pl.program_id(1) == 0)
    def _():
        carry_ref[...] = jnp.zeros_like(carry_ref)

    x = f_ref[...]
    ls = jnp.minimum(x, 0.0) - jnp.log(1.0 + jnp.exp(-jnp.abs(x)))
    upto = (lax.broadcasted_iota(I32, (ts, ts), 1) <= lax.broadcasted_iota(I32, (ts, ts), 0)).astype(F32)
    cs = jnp.dot(upto, ls, precision=HIGHEST, preferred_element_type=F32) + carry_ref[...]
    carry_ref[...] = cs[ts - 1:ts, :]
    rest = -LOG2E * cs
    pieces = []
    for _ in range(FOX_BIAS_PARTS):
        piece = rest.astype(BF16)
        pieces.append(piece)
        rest = rest - piece.astype(F32)
    ext = jnp.dot(jnp.concatenate(pieces, axis=1), sel_ref[...], preferred_element_type=F32).astype(BF16)
    for p in range(n_pairs):
        o_ref[:, 2 * LANES * p:2 * LANES * p + LANES] = k_ref[:, LANES * p:LANES * (p + 1)]
        o_ref[:, 2 * LANES * p + LANES:2 * LANES * (p + 1)] = ext[:, LANES * p:LANES * (p + 1)]


def _forget_keys(flog, k, bsz, seq):
    t, width = k.shape
    n_pairs = width // LANES
    ts = FCUM_TILE
    nt = seq // ts
    sel = [[0.0] * width for _ in range(FOX_BIAS_PARTS * LANES)]
    for h in range(2 * n_pairs):
        for part in range(FOX_BIAS_PARTS):
            sel[part * LANES + h][(h // 2) * LANES + FOX_BIAS_PARTS * (h % 2) + part] = 1.0
    sel = jnp.asarray(sel, BF16)
    tile = lambda b, j: (b * nt + j, 0)
    return pl.pallas_call(
        functools.partial(_fkeys_kernel, ts=ts, n_pairs=n_pairs),
        grid=(bsz, nt),
        in_specs=[pl.BlockSpec((ts, LANES), tile), pl.BlockSpec((ts, width), tile),
                  pl.BlockSpec(sel.shape, lambda b, j: (0, 0))],
        out_specs=pl.BlockSpec((ts, 2 * width), tile),
        out_shape=jax.ShapeDtypeStruct((t, 2 * width), BF16),
        scratch_shapes=[pltpu.VMEM((1, LANES), F32)],
        compiler_params=_cparams(("parallel", "arbitrary")),
        name="forget_keys",
    )(flog, k, sel)


def _fox_kernel(q_ref, ka_ref, vt_ref, o_ref, acc_ref, m_ref, l_ref, qa_ref, st0, st1, p0, p1, al0, al1,
                mb0, mb1, *, tq, tk):
    st_refs, p_refs, al_refs, mb_refs = (st0, st1), (p0, p1), (al0, al1), (mb0, mb1)
    n_full = pl.program_id(2)
    lane = lax.broadcasted_iota(I32, (tq, LANES), 1)
    qp = q_ref[...]
    for hh in range(2):
        mine = (lane < HEAD_DIM) if hh == 0 else (lane >= HEAD_DIM)
        ones = (lane >= FOX_BIAS_PARTS * hh) & (lane < FOX_BIAS_PARTS * (hh + 1))
        qa_ref[hh] = jnp.concatenate([jnp.where(mine, qp, jnp.zeros_like(qp)),
                                      jnp.where(ones, 1.0, 0.0).astype(BF16)], axis=1)
    acc_ref[...] = jnp.zeros_like(acc_ref)
    m_ref[...] = jnp.full_like(m_ref, NEG_INF)
    l_ref[...] = jnp.zeros_like(l_ref)

    def scores(kb, hh):
        off = pl.multiple_of(kb * tk, tk)
        return lax.dot_general(ka_ref[pl.ds(off, tk), :], qa_ref[hh], (((1,), (1,)), ((), ())),
                               preferred_element_type=F32)

    def softmax_step(st, block_max, hh):
        m_old = m_ref[hh:hh + 1, :]
        m_new = jnp.maximum(m_old, block_max)
        alpha = jnp.exp2(m_old - m_new)
        p = jnp.exp2(st - m_new)
        l_ref[hh:hh + 1, :] = alpha * l_ref[hh:hh + 1, :] + jnp.sum(p, axis=0, keepdims=True)
        m_ref[hh:hh + 1, :] = m_new
        return alpha, p.astype(BF16)

    def accumulate(kb, hh, alpha, p):
        off = pl.multiple_of(kb * tk, tk)
        rows = slice(HEAD_DIM * hh, HEAD_DIM * (hh + 1))
        acc_ref[rows, :] = alpha * acc_ref[rows, :] + jnp.dot(vt_ref[rows, pl.ds(off, tk)], p,
                                                               preferred_element_type=F32)

    keep = lax.broadcasted_iota(I32, (tk, tq), 0) <= lax.broadcasted_iota(I32, (tk, tq), 1)
    for hh in range(2):
        st = jnp.where(keep, scores(n_full, hh), NEG_INF)
        alpha, p = softmax_step(st, jnp.max(st, axis=0, keepdims=True), hh)
        accumulate(n_full, hh, alpha, p)

    def stage_a(kb, slot):
        for hh in range(2):
            st = scores(kb, hh)
            st_refs[slot][hh] = st
            mb_refs[slot][hh:hh + 1, :] = jnp.max(st, axis=0, keepdims=True)

    def stage_b(slot):
        for hh in range(2):
            alpha, p = softmax_step(st_refs[slot][hh], mb_refs[slot][hh:hh + 1, :], hh)
            al_refs[slot][hh:hh + 1, :] = alpha
            p_refs[slot][hh] = p

    def stage_c(kb, slot):
        for hh in range(2):
            accumulate(kb, hh, al_refs[slot][hh:hh + 1, :], p_refs[slot][hh])

    def step(i, slot):
        stage_a(i, slot)
        stage_b(1 - slot)
        stage_c(i - 2, slot)

    odd = lax.rem(n_full, 2) == 1

    @pl.when(n_full >= 1)
    def _():
        stage_a(0, 0)

    @pl.when(n_full >= 2)
    def _():
        stage_a(1, 1)
        stage_b(0)

    @pl.when(n_full == 1)
    def _():
        stage_b(0)
        stage_c(0, 0)

    def body(j, c):
        i = 2 + 2 * j
        step(i, 0)
        step(i + 1, 1)
        return c

    lax.fori_loop(0, (n_full - 2) // 2, body, 0)

    @pl.when(jnp.logical_and(n_full >= 2, jnp.logical_not(odd)))
    def _():
        stage_b(1)
        stage_c(n_full - 2, 0)
        stage_c(n_full - 1, 1)

    @pl.when(jnp.logical_and(n_full >= 3, odd))
    def _():
        step(n_full - 1, 0)
        stage_b(0)
        stage_c(n_full - 2, 1)
        stage_c(n_full - 1, 0)

    head0 = lax.broadcasted_iota(I32, (LANES, tq), 0) < HEAD_DIM
    out_t = acc_ref[...] / jnp.where(head0, l_ref[0:1, :], l_ref[1:2, :])
    o_ref[...] = out_t.T.astype(o_ref.dtype)


def _forgetting_attention(q, kaug, vt, bsz, seq):
    t, width = q.shape
    n_pairs = width // LANES
    tq, tk = FOX_TQ, FOX_TK
    nq = seq // tq
    return pl.pallas_call(
        functools.partial(_fox_kernel, tq=tq, tk=tk),
        grid=(bsz, n_pairs, nq),
        in_specs=[
            pl.BlockSpec((tq, LANES), lambda b, h, i: (b * nq + i, h)),
            pl.BlockSpec((seq, 2 * LANES), lambda b, h, i: (b, h)),
            pl.BlockSpec((LANES, seq), lambda b, h, i: (h, b)),
        ],
        out_specs=pl.BlockSpec((tq, LANES), lambda b, h, i: (b * nq + i, h)),
        out_shape=jax.ShapeDtypeStruct((t, width), BF16),
        scratch_shapes=[pltpu.VMEM((LANES, tq), F32), pltpu.VMEM((8, tq), F32), pltpu.VMEM((8, tq), F32),
                        pltpu.VMEM((2, tq, 2 * LANES), BF16)]
        + [pltpu.VMEM((2, tk, tq), F32)] * 2 + [pltpu.VMEM((2, tk, tq), BF16)] * 2 + [pltpu.VMEM((8, tq), F32)] * 4,
        compiler_params=_cparams(("parallel", "parallel", "arbitrary")),
        name="forgetting_attention",
    )(q, kaug, vt)


def _s5_prep_kernel(lre_ref, lim_ref, ldt_ref, bre_ref, bim_ref, are_ref, aim_ref, bb_re, bb_im):
    dt = jnp.exp(ldt_ref[...])
    lre = jnp.minimum(lre_ref[...], -1e-4)
    lim = lim_ref[...]
    mag = jnp.exp(lre * dt)
    a_re = mag * jnp.cos(lim * dt)
    a_im = mag * jnp.sin(lim * dt)
    den = lre * lre + lim * lim
    nre, nim = a_re - 1.0, a_im
    g_re = (nre * lre + nim * lim) / den
    g_im = (nim * lre - nre * lim) / den
    n = lre.shape[-1]
    per = n // bre_ref.shape[0]
    for j in range(bre_ref.shape[0]):
        gr = g_re[:, per * j:per * (j + 1)]
        gi = g_im[:, per * j:per * (j + 1)]
        bb_re[j] = (gr * bre_ref[j] - gi * bim_ref[j]).astype(bb_re.dtype)
        bb_im[j] = (gr * bim_ref[j] + gi * bre_ref[j]).astype(bb_im.dtype)
    are_ref[...] = a_re
    aim_ref[...] = a_im


def _block_diag_slabs(blocks, slabs):
    g, r, c = blocks.shape
    per = g // slabs
    eye = jnp.eye(per, dtype=bool)[None, :, None, :, None]
    bd = jnp.where(eye, blocks.reshape(slabs, per, r, 1, c), jnp.zeros((), blocks.dtype))
    return bd.reshape(slabs, per * r, per * c)


def _s5_kernel(u_ref, are_ref, aim_ref, bb_re, bb_im, cc_re, cc_im, dsk_ref, gw_ref, gb_ref, o_ref,
               xr0, xr1, xi0, xi1, carry_ref, *, ts, slabs):
    @pl.when(pl.program_id(1) == 0)
    def _():
        carry_ref[...] = jnp.zeros_like(carry_ref)

    xr, xi = (xr0, xr1), (xi0, xi1)
    per_half = TILE_ROWS // 2
    u = u_ref[...]
    ub = u.astype(BF16)
    for j in range(slabs):
        uj = ub[:, LANES * j:LANES * (j + 1)]
        bur = jnp.dot(uj, bb_re[j], preferred_element_type=F32)
        bui = jnp.dot(uj, bb_im[j], preferred_element_type=F32)
        half, base = j // 2, (j % 2) * per_half
        for s in range(per_half):
            xr[half][pl.ds(base + s, ts, stride=TILE_ROWS), :] = bur[:, LANES * s:LANES * (s + 1)]
            xi[half][pl.ds(base + s, ts, stride=TILE_ROWS), :] = bui[:, LANES * s:LANES * (s + 1)]
    a_r = [are_ref[TILE_ROWS * h:TILE_ROWS * (h + 1), :] for h in range(2)]
    a_i = [aim_ref[TILE_ROWS * h:TILE_ROWS * (h + 1), :] for h in range(2)]

    def steps(tb, state):
        state = list(state)
        for tt in range(TILE_ROWS):
            row = pl.multiple_of((tb * TILE_ROWS + tt) * TILE_ROWS, TILE_ROWS)
            for h in range(2):
                sr, si = state[h], state[2 + h]
                nr = a_r[h] * sr - a_i[h] * si + xr[h][pl.ds(row, TILE_ROWS), :]
                ni = a_r[h] * si + a_i[h] * sr + xi[h][pl.ds(row, TILE_ROWS), :]
                xr[h][pl.ds(row, TILE_ROWS), :] = nr
                xi[h][pl.ds(row, TILE_ROWS), :] = ni
                state[h], state[2 + h] = nr, ni
        return tuple(state)

    final = lax.fori_loop(0, ts // TILE_ROWS, steps, tuple(carry_ref[i] for i in range(4)))
    for i in range(4):
        carry_ref[i] = final[i]
    ys = []
    for j in range(slabs):
        half, base = j // 2, (j % 2) * per_half
        gather = lambda ref: jnp.concatenate(
            [ref[pl.ds(base + s, ts, stride=TILE_ROWS), :] for s in range(per_half)], axis=1).astype(BF16)
        yj = jnp.dot(gather(xr[half]), cc_re[j], preferred_element_type=F32)
        ys.append(yj - jnp.dot(gather(xi[half]), cc_im[j], preferred_element_type=F32))
    y = jnp.concatenate(ys, axis=1) + dsk_ref[...] * u
    y = 0.5 * y * (1.0 + jnp.tanh(math.sqrt(2.0 / math.pi) * (y + 0.044715 * (y * y * y))))
    gate = jnp.dot(y.astype(BF16), gw_ref[...], preferred_element_type=F32) + gb_ref[...]
    o_ref[...] = (y * jax.nn.sigmoid(gate)).astype(o_ref.dtype)


def _s5_mixer(u, lam_re, lam_im, log_dt, b_re, b_im, c_re, c_im, d_skip, glu_w, glu_b, bsz, seq):
    t, width = u.shape
    groups, states = lam_re.shape
    n = groups * states
    slabs = width // LANES
    ts = S5_TILE
    assert n == 2 * TILE_ROWS * LANES and slabs == 4
    flat = lambda a: a.reshape(1, n)
    ldt = jnp.repeat(log_dt, states).reshape(1, n)
    bre_bd = _block_diag_slabs(jnp.swapaxes(b_re, 1, 2), slabs)
    bim_bd = _block_diag_slabs(jnp.swapaxes(b_im, 1, 2), slabs)
    cre_bd = _block_diag_slabs(jnp.swapaxes(c_re, 1, 2), slabs).astype(BF16)
    cim_bd = _block_diag_slabs(jnp.swapaxes(c_im, 1, 2), slabs).astype(BF16)
    full = lambda a: pl.BlockSpec(a.shape, lambda *_: (0,) * a.ndim)
    prep_in = (flat(lam_re), flat(lam_im), ldt, bre_bd, bim_bd)
    a_re, a_im, bb_re, bb_im = pl.pallas_call(
        _s5_prep_kernel,
        grid=(1,),
        in_specs=[full(a) for a in prep_in],
        out_specs=[pl.BlockSpec((1, n), lambda i: (0, 0))] * 2 + [pl.BlockSpec(bre_bd.shape, lambda i: (0, 0, 0))] * 2,
        out_shape=[jax.ShapeDtypeStruct((1, n), F32)] * 2 + [jax.ShapeDtypeStruct(bre_bd.shape, BF16)] * 2,
        compiler_params=_cparams(("arbitrary",)),
        name="s5_prep",
    )(*prep_in)
    nt = seq // ts
    tiles = lambda a: a.reshape(n // LANES, LANES)
    consts = (tiles(a_re), tiles(a_im), bb_re, bb_im, cre_bd, cim_bd, d_skip.reshape(1, width),
              glu_w.astype(BF16), glu_b.reshape(1, width))
    return pl.pallas_call(
        functools.partial(_s5_kernel, ts=ts, slabs=slabs),
        grid=(bsz, nt),
        in_specs=[pl.BlockSpec((ts, width), lambda b, j: (b * nt + j, 0))] + [full(a) for a in consts],
        out_specs=pl.BlockSpec((ts, width), lambda b, j: (b * nt + j, 0)),
        out_shape=jax.ShapeDtypeStruct((t, width), BF16),
        scratch_shapes=[pltpu.VMEM((ts * TILE_ROWS, LANES), F32)] * 4 + [pltpu.VMEM((4, TILE_ROWS, LANES), F32)],
        compiler_params=_cparams(("parallel", "arbitrary")),
        name="s5_mixer",
    )(u, *consts)


def kernel(x, c, mod_w, mod_b, ln_g, ln_b, even_w_in, pool_w, pool_scale, rel_bias, even_w_out, odd_w_in, forget_b, ssm_lam_re, ssm_lam_im, ssm_log_dt, ssm_b_re, ssm_b_im, ssm_c_re, ssm_c_im, ssm_d, ssm_glu_w, ssm_glu_b, odd_w_out, router_w, router_b, exp_w_gu, exp_b_gu, exp_w_down, exp_b_down):
    bsz, seq, d = x.shape
    depth = mod_w.shape[0]
    alpha = (2.0 * depth) ** 0.25
    mods = _ada_mods(c, mod_w.reshape(depth * 2, d, 3 * d), mod_b.reshape(depth * 2, 3 * d))
    xt = x.reshape(bsz * seq, d)
    for layer in range(depth):
        i = layer // 2
        if layer % 2 == 0:
            w_in = even_w_in[i]
            width = w_in.shape[1] // 4
            zp, q, k, vt = _inproj_even(xt, mods, 2 * layer, w_in[:, :3 * width].astype(BF16),
                                        w_in[:, 3 * width:].T.astype(BF16), seq)
            ya = _pool_mixer(zp, pool_w[i].astype(BF16), pool_scale[i], seq)
            yb = _chunk_attention(q, k, vt, _cattn_bias_table(rel_bias[i]), bsz, seq)
            w_out = even_w_out[i]
        else:
            width = ssm_d.shape[1] * ssm_d.shape[2]
            w_in = odd_w_in[i]
            n_heads = forget_b.shape[1]
            wf = jnp.zeros((d, LANES), BF16).at[:, :n_heads].set(w_in[:, 4 * width:].astype(BF16))
            fb = jnp.zeros((1, LANES), F32).at[0, :n_heads].set(forget_b[i])
            zs, q, k, vt, flog = _inproj_odd(xt, mods, 2 * layer, w_in[:, :3 * width].astype(BF16),
                                             w_in[:, 3 * width:4 * width].T.astype(BF16), wf, fb, seq)
            ya = _s5_mixer(zs, ssm_lam_re[i], ssm_lam_im[i], ssm_log_dt[i], ssm_b_re[i], ssm_b_im[i],
                           ssm_c_re[i], ssm_c_im[i], ssm_d[i], ssm_glu_w[i], ssm_glu_b[i], bsz, seq)
            yb = _forgetting_attention(q, _forget_keys(flog, k, bsz, seq), vt, bsz, seq)
            w_out = odd_w_out[i]
        xt = _outproj_norm(ya, yb, xt, mods, 2 * layer, w_out.astype(BF16), ln_g[layer, 0], ln_b[layer, 0],
                           seq, alpha)
        xt = _moe_block(xt, mods, 2 * layer + 1, router_w[layer], router_b[layer], layer, exp_w_gu, exp_b_gu,
                        exp_w_down, exp_b_down, ln_g[layer, 1], ln_b[layer, 1], seq, alpha)
    return xt.reshape(bsz, seq, d)
```

```python
import functools
import math

import jax
import jax.numpy as jnp
from jax import lax
from jax.experimental import pallas as pl
from jax.experimental.pallas import tpu as pltpu

F32 = jnp.float32
BF16 = jnp.bfloat16
I32 = jnp.int32
HIGHEST = lax.Precision.HIGHEST

LANES = 128
SUBLANES = 8
VMEM_LIMIT_BYTES = 60000 * 1024

HEAD_DIM = 64
CHUNK = 64
LEFT_CHUNKS = 8
REL_CLIP = 128
POOL_WINDOWS = (2, 4, 8, 16)
POOL_HALO = 16
SSM_GROUP_DIM = 16
SSM_STATE = 64
N_EXPERTS = 32
TOP_K = 4
SWIGLU_LIMIT = 7.0
SWIGLU_ALPHA = 1.702
LN_EPS = 1e-5
NEG_INF = -1e30
LOG2E = math.log2(math.e)
FOX_BIAS_PARTS = 3

ROW_TILE = 512
CATTN_Q = 2 * CHUNK
CATTN_BAND = (LEFT_CHUNKS + 2) * CHUNK
FOX_TQ = 512
FOX_TK = 512
S5_TILE = 256
FCUM_TILE = 512
TILE_ROWS = SUBLANES
MOE_ROW_BLOCK = 256
MOE_TOK_TILE = 256
FILL_TILES = 128
ROUTE_TILE = 512


def _cparams(sem):
    return pltpu.CompilerParams(dimension_semantics=sem, vmem_limit_bytes=VMEM_LIMIT_BYTES)


def _layer_norm(r, g, b):
    mu = jnp.mean(r, axis=-1, keepdims=True)
    d = r - mu
    var = jnp.mean(d * d, axis=-1, keepdims=True)
    return d * lax.rsqrt(var + LN_EPS) * g + b


def _mod_kernel(c_ref, w_ref, b_ref, o_ref):
    c = c_ref[...]
    s = c * jax.nn.sigmoid(c)
    o_ref[0] = jnp.dot(s, w_ref[0], precision=HIGHEST, preferred_element_type=F32) + b_ref[0]


def _ada_mods(c, mod_w, mod_b):
    bsz, d = c.shape
    m = mod_w.shape[0]
    rows = 8
    c_pad = jnp.zeros((rows, d), F32).at[:bsz].set(c)
    out = pl.pallas_call(
        _mod_kernel,
        grid=(m, 3),
        in_specs=[
            pl.BlockSpec((rows, d), lambda i, j: (0, 0)),
            pl.BlockSpec((1, d, d), lambda i, j: (i, 0, j)),
            pl.BlockSpec((1, 1, d), lambda i, j: (i, 0, j)),
        ],
        out_specs=pl.BlockSpec((1, rows, d), lambda i, j: (i, 0, j)),
        out_shape=jax.ShapeDtypeStruct((m, rows, 3 * d), F32),
        compiler_params=_cparams(("parallel", "parallel")),
        name="ada_mod",
    )(c_pad, mod_w, mod_b.reshape(m, 1, 3 * d))
    return out[:, :bsz].reshape(m, bsz, 3, d)


def _mod_spec(midx, tiles_per_batch, d):
    return pl.BlockSpec((None, None, 3, d), lambda i: (midx, i // tiles_per_batch, 0, 0))


def _inproj_even_kernel(x_ref, mod_ref, w_ref, wvt_ref, zp_ref, q_ref, k_ref, vt_ref, *, width):
    h = (x_ref[...] * (1.0 + mod_ref[1:2, :]) + mod_ref[0:1, :]).astype(BF16)
    z = jnp.dot(h, w_ref[...], preferred_element_type=F32)
    zp_ref[...] = z[:, :width]
    q_ref[...] = (z[:, width:2 * width] * (HEAD_DIM ** -0.5 * LOG2E)).astype(BF16)
    k_ref[...] = z[:, 2 * width:3 * width].astype(BF16)
    vt_ref[...] = lax.dot_general(wvt_ref[...], h, (((1,), (1,)), ((), ())),
                                  preferred_element_type=F32).astype(BF16)


def _inproj_even(x, mods, midx, w_main_bf, wvt_bf, seq):
    t, d = x.shape
    width = wvt_bf.shape[0]
    tm = ROW_TILE
    row = lambda i: (i, 0)
    return pl.pallas_call(
        functools.partial(_inproj_even_kernel, width=width),
        grid=(t // tm,),
        in_specs=[
            pl.BlockSpec((tm, d), row),
            _mod_spec(midx, seq // tm, d),
            pl.BlockSpec((d, 3 * width), lambda i: (0, 0)),
            pl.BlockSpec((width, d), lambda i: (0, 0)),
        ],
        out_specs=[pl.BlockSpec((tm, width), row)] * 3
        + [pl.BlockSpec((None, width, tm), lambda i: (i // (seq // tm), 0, i % (seq // tm)))],
        out_shape=[jax.ShapeDtypeStruct((t, width), F32)] + [jax.ShapeDtypeStruct((t, width), BF16)] * 2
        + [jax.ShapeDtypeStruct((t // seq, width, seq), BF16)],
        compiler_params=_cparams(("parallel",)),
        name="inproj_even",
    )(x, mods, w_main_bf, wvt_bf)


def _pool_kernel(zp_ref, halo_ref, w_ref, sc_ref, o_ref, xs_ref, *, ts, seq):
    i = pl.program_id(0)
    t0 = lax.rem(i * ts, seq)
    xs_ref[0:POOL_HALO, :] = jnp.where(t0 == 0, 0.0, halo_ref[...])
    xs_ref[POOL_HALO:POOL_HALO + ts, :] = zp_ref[...]
    pos = lax.broadcasted_iota(I32, (ts, LANES), 0) + t0
    for g, win in enumerate(POOL_WINDOWS):
        cols = slice(LANES * g, LANES * (g + 1))
        u = xs_ref[POOL_HALO:POOL_HALO + ts, cols]
        acc = u
        for k in range(1, win):
            acc = acc + xs_ref[POOL_HALO - k:POOL_HALO - k + ts, cols]
        cnt = jnp.minimum(pos + 1, win).astype(F32)
        dlt = acc / cnt - u
        y = jnp.dot(dlt.astype(BF16), w_ref[g], preferred_element_type=F32)
        o_ref[:, cols] = (y * sc_ref[:, cols]).astype(o_ref.dtype)


def _pool_mixer(zp, w_pool_bf, pool_scale, seq):
    t, width = zp.shape
    ts = ROW_TILE
    hb = ts // POOL_HALO
    return pl.pallas_call(
        functools.partial(_pool_kernel, ts=ts, seq=seq),
        grid=(t // ts,),
        in_specs=[
            pl.BlockSpec((ts, width), lambda i: (i, 0)),
            pl.BlockSpec((POOL_HALO, width), lambda i: (jnp.maximum(i * hb - 1, 0), 0)),
            pl.BlockSpec(w_pool_bf.shape, lambda i: (0, 0, 0)),
            pl.BlockSpec((1, width), lambda i: (0, 0)),
        ],
        out_specs=pl.BlockSpec((ts, width), lambda i: (i, 0)),
        out_shape=jax.ShapeDtypeStruct((t, width), BF16),
        scratch_shapes=[pltpu.VMEM((ts + POOL_HALO, width), F32)],
        compiler_params=_cparams(("parallel",)),
        name="pool_mixer",
    )(zp, zp, w_pool_bf, pool_scale.reshape(1, width))


def _cbias_kernel(r_ref, o_ref):
    base = jnp.broadcast_to(r_ref[...], (CATTN_Q, r_ref.shape[-1]))
    rolled = pltpu.roll(base, 0, 1, stride=1, stride_axis=0)[:, :CATTN_BAND]
    qi = lax.broadcasted_iota(I32, (CATTN_Q, CATTN_BAND), 0)
    kp = lax.broadcasted_iota(I32, (CATTN_Q, CATTN_BAND), 1)
    lo = (qi // CHUNK) * CHUNK
    valid = (kp >= lo) & (kp < lo + (LEFT_CHUNKS + 1) * CHUNK)
    o_ref[...] = jnp.where(valid, rolled * LOG2E, NEG_INF).T


def _cattn_bias_table(rel_bias):
    h = rel_bias.shape[0]
    wide = 1024
    far = LEFT_CHUNKS * CHUNK - REL_CLIP + 1
    tail = wide - far - (2 * REL_CLIP - 1)
    last = rel_bias[:, 2 * REL_CLIP:]
    base = jnp.concatenate(
        [jnp.broadcast_to(last, (h, far)), rel_bias[:, 1:2 * REL_CLIP][:, ::-1], jnp.broadcast_to(last, (h, tail))],
        axis=1).reshape(h, 1, wide)
    return pl.pallas_call(
        _cbias_kernel,
        grid=(h,),
        in_specs=[pl.BlockSpec((None, 1, wide), lambda i: (i, 0, 0))],
        out_specs=pl.BlockSpec((None, CATTN_BAND, CATTN_Q), lambda i: (i, 0, 0)),
        out_shape=jax.ShapeDtypeStruct((h, CATTN_BAND, CATTN_Q), F32),
        compiler_params=_cparams(("parallel",)),
        name="cattn_bias",
    )(base)


def _cattn_kernel(q_ref, k_ref, vt_ref, bias_ref, o_ref, st_ref, e_ref, l_ref, m_ref, *, n_pairs):
    p = pl.program_id(1)
    start = pl.multiple_of(p * CATTN_Q, CATTN_Q)
    kpos = lax.broadcasted_iota(I32, (CATTN_BAND, CATTN_Q), 0) + start
    real_key = kpos >= LEFT_CHUNKS * CHUNK
    lane = lax.broadcasted_iota(I32, (CATTN_Q, LANES), 1)
    n_heads = 2 * n_pairs
    for h in range(n_heads):
        cols = slice(LANES * (h // 2), LANES * (h // 2 + 1))
        qp = q_ref[:, cols]
        mine = (lane < HEAD_DIM) if h % 2 == 0 else (lane >= HEAD_DIM)
        qm = jnp.where(mine, qp, jnp.zeros_like(qp))
        st = lax.dot_general(k_ref[pl.ds(start, CATTN_BAND), cols], qm, (((1,), (1,)), ((), ())),
                             preferred_element_type=F32)
        st = jnp.where(real_key, st + bias_ref[h], NEG_INF)
        st_ref[h] = st
        m_ref[h:h + 1, :] = jnp.max(st, axis=0, keepdims=True)
    for h in range(n_heads):
        e = jnp.exp2(st_ref[h] - m_ref[h:h + 1, :])
        l_ref[h:h + 1, :] = jnp.sum(e, axis=0, keepdims=True)
        e_ref[h] = e.astype(BF16)
    for hp in range(n_pairs):
        outs = []
        for hh in range(2):
            h = 2 * hp + hh
            vth = vt_ref[HEAD_DIM * h:HEAD_DIM * (h + 1), pl.ds(start, CATTN_BAND)]
            outs.append(jnp.dot(vth, e_ref[h], preferred_element_type=F32) / l_ref[h:h + 1, :])
        o_ref[:, LANES * hp:LANES * (hp + 1)] = jnp.concatenate(outs, axis=0).T.astype(o_ref.dtype)


def _chunk_attention(q, k, vt, bias_t, bsz, seq):
    t, width = q.shape
    n_pairs = width // LANES
    pad = LEFT_CHUNKS * CHUNK
    kp = jnp.pad(k.reshape(bsz, seq, width), ((0, 0), (pad, 0), (0, 0)))
    vtp = jnp.pad(vt, ((0, 0), (0, 0), (pad, 0)))
    nq = seq // CATTN_Q
    return pl.pallas_call(
        functools.partial(_cattn_kernel, n_pairs=n_pairs),
        grid=(bsz, nq),
        in_specs=[
            pl.BlockSpec((CATTN_Q, width), lambda b, p: (b * nq + p, 0)),
            pl.BlockSpec((None, seq + pad, width), lambda b, p: (b, 0, 0)),
            pl.BlockSpec((None, width, seq + pad), lambda b, p: (b, 0, 0)),
            pl.BlockSpec(bias_t.shape, lambda b, p: (0, 0, 0)),
        ],
        out_specs=pl.BlockSpec((CATTN_Q, width), lambda b, p: (b * nq + p, 0)),
        out_shape=jax.ShapeDtypeStruct((t, width), BF16),
        scratch_shapes=[pltpu.VMEM((2 * n_pairs, CATTN_BAND, CATTN_Q), F32),
                        pltpu.VMEM((2 * n_pairs, CATTN_BAND, CATTN_Q), BF16),
                        pltpu.VMEM((2 * n_pairs, CATTN_Q), F32), pltpu.VMEM((2 * n_pairs, CATTN_Q), F32)],
        compiler_params=_cparams(("parallel", "arbitrary")),
        name="chunk_attention",
    )(q, kp, vtp, bias_t)


def _outproj_kernel(ya_ref, yb_ref, x_ref, mod_ref, w_ref, g_ref, b_ref, o_ref, *, alpha, half):
    y = jnp.dot(ya_ref[...], w_ref[0:half, :], preferred_element_type=F32)
    y = y + jnp.dot(yb_ref[...], w_ref[half:2 * half, :], preferred_element_type=F32)
    r = alpha * x_ref[...] + (1.0 + mod_ref[2:3, :]) * y
    o_ref[...] = _layer_norm(r, g_ref[...], b_ref[...])


def _outproj_norm(ya, yb, x, mods, midx, w_out_bf, ln_g, ln_b, seq, alpha):
    t, d = x.shape
    half = ya.shape[1]
    tm = ROW_TILE
    row = lambda i: (i, 0)
    return pl.pallas_call(
        functools.partial(_outproj_kernel, alpha=alpha, half=half),
        grid=(t // tm,),
        in_specs=[
            pl.BlockSpec((tm, half), row),
            pl.BlockSpec((tm, half), row),
            pl.BlockSpec((tm, d), row),
            _mod_spec(midx, seq // tm, d),
            pl.BlockSpec((2 * half, d), lambda i: (0, 0)),
            pl.BlockSpec((1, d), lambda i: (0, 0)),
            pl.BlockSpec((1, d), lambda i: (0, 0)),
        ],
        out_specs=pl.BlockSpec((tm, d), row),
        out_shape=jax.ShapeDtypeStruct((t, d), F32),
        compiler_params=_cparams(("parallel",)),
        name="outproj_norm",
    )(ya, yb, x, mods, w_out_bf, ln_g.reshape(1, d), ln_b.reshape(1, d))


def _route_kernel(x_ref, mod_ref, rw_ref, rb_ref, h_ref, e_ref, r_ref, g_ref, c_ref, carry_ref, *, tm):
    @pl.when(pl.program_id(0) == 0)
    def _():
        carry_ref[...] = jnp.zeros_like(carry_ref)

    h = x_ref[...] * (1.0 + mod_ref[1:2, :]) + mod_ref[0:1, :]
    _store_token_tiles(h_ref, h)
    logits = lax.dot_general(rw_ref[...], h, (((1,), (1,)), ((), ())), precision=HIGHEST,
                             preferred_element_type=F32) + rb_ref[...]
    eio = lax.broadcasted_iota(I32, (N_EXPERTS, tm), 0)
    vals, hots = [], []
    for k in range(TOP_K):
        m = jnp.max(logits, axis=0, keepdims=True)
        idx = jnp.min(jnp.where(logits == m, eio, N_EXPERTS), axis=0, keepdims=True)
        hot = eio == idx
        e_ref[k:k + 1, :] = idx
        vals.append(m)
        hots.append(hot)
        logits = jnp.where(hot, -jnp.inf, logits)
    exps = [jnp.exp(v - vals[0]) for v in vals]
    denom = exps[0] + exps[1] + exps[2] + exps[3]
    for k in range(TOP_K):
        g_ref[k:k + 1, :] = exps[k] / denom
    cnt = jnp.zeros((N_EXPERTS, tm), F32)
    for hot in hots:
        cnt = cnt + hot.astype(F32)
    before = (lax.broadcasted_iota(I32, (tm, tm), 0) < lax.broadcasted_iota(I32, (tm, tm), 1))
    prefix = jnp.dot(cnt.astype(BF16), before.astype(BF16), preferred_element_type=F32)
    tot = prefix + carry_ref[...]
    for k in range(TOP_K):
        r_ref[k:k + 1, :] = jnp.sum(jnp.where(hots[k], tot, 0.0), axis=0, keepdims=True).astype(I32)
    new_carry = carry_ref[...] + jnp.sum(cnt, axis=1, keepdims=True)
    carry_ref[...] = new_carry
    c_ref[...] = jnp.broadcast_to(new_carry, c_ref.shape)


def _route(x, mods, midx, router_w, router_b, seq):
    t, d = x.shape
    tm = ROUTE_TILE
    tok = lambda i: (0, i)
    return pl.pallas_call(
        functools.partial(_route_kernel, tm=tm),
        grid=(t // tm,),
        in_specs=[
            pl.BlockSpec((tm, d), lambda i: (i, 0)),
            _mod_spec(midx, seq // tm, d),
            pl.BlockSpec((N_EXPERTS, d), lambda i: (0, 0)),
            pl.BlockSpec((N_EXPERTS, 1), lambda i: (0, 0)),
        ],
        out_specs=[
            pl.BlockSpec((tm * TILE_ROWS, LANES), lambda i: (i, 0)),
            pl.BlockSpec((TOP_K, tm), tok),
            pl.BlockSpec((TOP_K, tm), tok),
            pl.BlockSpec((TOP_K, tm), tok),
            pl.BlockSpec((N_EXPERTS, LANES), lambda i: (0, 0)),
        ],
        out_shape=[
            jax.ShapeDtypeStruct((t * TILE_ROWS, LANES), F32),
            jax.ShapeDtypeStruct((TOP_K, t), I32),
            jax.ShapeDtypeStruct((TOP_K, t), I32),
            jax.ShapeDtypeStruct((TOP_K, t), F32),
            jax.ShapeDtypeStruct((N_EXPERTS, LANES), F32),
        ],
        scratch_shapes=[pltpu.VMEM((N_EXPERTS, 1), F32)],
        compiler_params=_cparams(("arbitrary",)),
        name="moe_route",
    )(x, mods, router_w.T, router_b.reshape(N_EXPERTS, 1))


def _plan_kernel(e_ref, r_ref, ccol_ref, crow_ref, d_ref, blk_ref, exp_ref, nu_ref, *, tm, rb, nbp):
    inv = 1.0 / rb
    pad_col = jnp.floor((ccol_ref[...] + (rb - 1)) * inv) * rb
    pad_row = jnp.floor((crow_ref[0:1, :] + (rb - 1)) * inv) * rb
    ei = lax.broadcasted_iota(I32, (N_EXPERTS, LANES), 0)
    li = lax.broadcasted_iota(I32, (N_EXPERTS, LANES), 1)
    pstart = jnp.sum(jnp.where(li < ei, jnp.broadcast_to(pad_row, (N_EXPERTS, LANES)), 0.0),
                     axis=1, keepdims=True)
    padded = pad_col[:, 0:1]
    pend = pstart + padded
    eio = lax.broadcasted_iota(I32, (N_EXPERTS, tm), 0)
    for k in range(TOP_K):
        base = jnp.sum(jnp.where(eio == e_ref[k:k + 1, :], pstart, 0.0), axis=0, keepdims=True)
        d_ref[k:k + 1, :] = base.astype(I32) + r_ref[k:k + 1, :]
    row0 = (lax.broadcasted_iota(I32, (N_EXPERTS, nbp), 1) * rb).astype(F32)
    be = jnp.minimum(jnp.sum((pend <= row0).astype(F32), axis=0, keepdims=True), N_EXPERTS - 1.0)
    ebi = lax.broadcasted_iota(I32, (N_EXPERTS, nbp), 0).astype(F32)
    mine = ebi == be
    my_end = jnp.sum(jnp.where(mine, pend, 0.0), axis=0, keepdims=True)
    nxt = jnp.sum((pend <= my_end).astype(F32), axis=0, keepdims=True)
    order = jnp.sum(jnp.where(jnp.logical_and(padded > 0.0, ebi < be), 1.0, 0.0), axis=0, keepdims=True)
    blk_ref[0:1, :] = be.astype(I32)
    blk_ref[1:2, :] = jnp.where(nxt < N_EXPERTS, nxt, -1.0).astype(I32)
    blk_ref[2:3, :] = (order - 2.0 * jnp.floor(order * 0.5)).astype(I32)
    blk_ref[3:8, :] = jnp.zeros((5, nbp), I32)
    pstart_row = jnp.sum(jnp.where(ei < li, pad_col, 0.0), axis=0, keepdims=True)
    exp_ref[0:1, :] = crow_ref[0:1, :].astype(I32)
    exp_ref[1:2, :] = pstart_row.astype(I32)
    exp_ref[2:3, :] = pad_row.astype(I32)
    exp_ref[3:8, :] = jnp.zeros((5, LANES), I32)
    used = jnp.sum(padded, axis=0, keepdims=True) * inv
    nu_ref[...] = jnp.broadcast_to(used, nu_ref.shape).astype(I32)


def _plan(eidx, rank, counts_col, rb, nb):
    t = eidx.shape[1]
    tm = ROUTE_TILE
    nbp = -(-nb // LANES) * LANES
    counts_row = jnp.zeros((8, LANES), F32).at[:, :N_EXPERTS].set(counts_col[:, 0][None, :])
    tok = lambda i: (0, i)
    dest, blk, exp, nu = pl.pallas_call(
        functools.partial(_plan_kernel, tm=tm, rb=rb, nbp=nbp),
        grid=(t // tm,),
        in_specs=[
            pl.BlockSpec((TOP_K, tm), tok),
            pl.BlockSpec((TOP_K, tm), tok),
            pl.BlockSpec((N_EXPERTS, LANES), lambda i: (0, 0)),
            pl.BlockSpec((8, LANES), lambda i: (0, 0)),
        ],
        out_specs=[
            pl.BlockSpec((TOP_K, tm), tok),
            pl.BlockSpec((8, nbp), lambda i: (0, 0)),
            pl.BlockSpec((8, LANES), lambda i: (0, 0)),
            pl.BlockSpec((1, LANES), lambda i: (0, 0)),
        ],
        out_shape=[
            jax.ShapeDtypeStruct((TOP_K, t), I32),
            jax.ShapeDtypeStruct((8, nbp), I32),
            jax.ShapeDtypeStruct((8, LANES), I32),
            jax.ShapeDtypeStruct((1, LANES), I32),
        ],
        compiler_params=_cparams(("arbitrary",)),
        name="moe_plan",
    )(eidx, rank, counts_col, counts_row)
    return dest, (blk[0], blk[1], blk[2]), (exp[0], exp[1], exp[2]), nu[0, :1]


def _store_token_tiles(ref, x):
    n = x.shape[0]
    for s in range(TILE_ROWS):
        ref[pl.ds(s, n, stride=TILE_ROWS), :] = x[:, LANES * s:LANES * (s + 1)]


def _load_token_tiles(ref, n):
    return jnp.concatenate([ref[pl.ds(s, n, stride=TILE_ROWS), :] for s in range(TILE_ROWS)], axis=1)


def _tile_copy(src_ref, src_tok, dst_ref, dst_tok, sem):
    src = src_ref.at[pl.ds(pl.multiple_of(src_tok * TILE_ROWS, TILE_ROWS), TILE_ROWS)]
    dst = dst_ref.at[pl.ds(pl.multiple_of(dst_tok * TILE_ROWS, TILE_ROWS), TILE_ROWS)]
    return pltpu.make_async_copy(src, dst, sem)


def _dispatch_kernel(cnt_ref, first_ref, padded_ref, h_ref, dest_ref, xout_ref, zero_ref, sem, zsem, *, tm, n_rows):
    i = pl.program_id(0)
    steps = pl.num_programs(0)

    def fill_copy(row, size):
        return pltpu.make_async_copy(zero_ref.at[pl.ds(0, size * TILE_ROWS)],
                                     xout_ref.at[pl.ds(pl.multiple_of(row * TILE_ROWS, TILE_ROWS), size * TILE_ROWS)],
                                     zsem)

    def each_fill(act):
        def per_expert(e, c):
            lo = first_ref[e] + cnt_ref[e]
            n = jnp.where(e == N_EXPERTS - 1, n_rows, first_ref[e] + padded_ref[e]) - lo
            whole, rem = n // FILL_TILES, lax.rem(n, FILL_TILES)

            def big(j, c2):
                act(fill_copy(lo + j * FILL_TILES, FILL_TILES))
                return c2

            lax.fori_loop(0, whole, big, 0)
            size = FILL_TILES // 2
            while size >= 1:
                @pl.when((rem & size) != 0)
                def _(size=size):
                    act(fill_copy(lo + whole * FILL_TILES + (rem & ~(2 * size - 1)), size))
                size //= 2
            return c

        lax.fori_loop(0, N_EXPERTS, per_expert, 0)

    @pl.when(i == 0)
    def _():
        zero_ref[...] = jnp.zeros_like(zero_ref)
        each_fill(lambda copy: copy.start())

    def issue(r, c):
        for k in range(TOP_K):
            _tile_copy(h_ref, r, xout_ref, dest_ref[k, r], sem).start(priority=k % 2)
        return c

    lax.fori_loop(0, tm, issue, 0)

    def drain(r, c):
        for k in range(TOP_K):
            _tile_copy(h_ref, 0, xout_ref, 0, sem).wait()
        return c

    lax.fori_loop(0, tm, drain, 0)

    @pl.when(i == steps - 1)
    def _():
        each_fill(lambda copy: copy.wait())


def _dispatch(h_tiles, dest, expert_table, n_rows):
    t = h_tiles.shape[0] // TILE_ROWS
    tm = MOE_TOK_TILE
    return pl.pallas_call(
        functools.partial(_dispatch_kernel, tm=tm, n_rows=n_rows),
        grid_spec=pltpu.PrefetchScalarGridSpec(
            num_scalar_prefetch=3,
            grid=(t // tm,),
            in_specs=[
                pl.BlockSpec((tm * TILE_ROWS, LANES), lambda i, *_: (i, 0)),
                pl.BlockSpec((TOP_K, tm), lambda i, *_: (0, i), memory_space=pltpu.SMEM),
            ],
            out_specs=pl.BlockSpec(memory_space=pl.ANY),
            scratch_shapes=[pltpu.VMEM((FILL_TILES * TILE_ROWS, LANES), h_tiles.dtype), pltpu.SemaphoreType.DMA,
                            pltpu.SemaphoreType.DMA],
        ),
        out_shape=jax.ShapeDtypeStruct((n_rows * TILE_ROWS, LANES), h_tiles.dtype),
        compiler_params=_cparams(("arbitrary",)),
        name="moe_dispatch",
    )(*expert_table, h_tiles, dest)


def _expert_kernel(be_ref, nx_ref, slot_ref, nu_ref, x_ref, bgu_ref, bd_ref, wgu_hbm, wd_hbm, o_ref,
                   wgu_f, wd_f, wgu_s, wd_s, sems, *, dff, rb, layer):
    b = pl.program_id(0)
    e = be_ref[b]
    slot = slot_ref[b]
    used = b < nu_ref[0]
    first = jnp.logical_and(used, jnp.logical_or(b == 0, e != be_ref[jnp.maximum(b - 1, 0)]))

    def fetch(expert, s):
        return (pltpu.make_async_copy(wgu_hbm.at[layer, expert], wgu_f.at[s], sems.at[0, s]),
                pltpu.make_async_copy(wd_hbm.at[layer, expert], wd_f.at[s], sems.at[1, s]))

    @pl.when(b == 0)
    def _():
        for copy in fetch(e, slot):
            copy.start()

    @pl.when(first)
    def _():
        for copy in fetch(e, slot):
            copy.wait()
        wgu_s[...] = wgu_f[slot].astype(BF16)
        wd_s[...] = wd_f[slot].astype(BF16)

        @pl.when(nx_ref[b] >= 0)
        def _():
            for copy in fetch(nx_ref[b], 1 - slot):
                copy.start()

    @pl.when(used)
    def _():
        x = _load_token_tiles(x_ref, rb)
        gu = jnp.dot(x.astype(BF16), wgu_s[...], preferred_element_type=F32) + bgu_ref[0]
        gate = jnp.minimum(gu[:, :dff], SWIGLU_LIMIT)
        up = jnp.clip(gu[:, dff:], -SWIGLU_LIMIT, SWIGLU_LIMIT)
        act = (up + 1.0) * (gate * jax.nn.sigmoid(SWIGLU_ALPHA * gate))
        _store_token_tiles(o_ref, jnp.dot(act.astype(BF16), wd_s[...], preferred_element_type=F32) + bd_ref[0])

    @pl.when(jnp.logical_not(used))
    def _():
        o_ref[...] = jnp.zeros_like(o_ref)


def _expert_ffn(x_disp, block_table, n_used, layer, w_gu, b_gu, w_down, b_down, rb):
    nl, ne, d, dff2 = w_gu.shape
    dff = dff2 // 2
    nb = x_disp.shape[0] // (rb * TILE_ROWS)
    pick = lambda b, be, *_: (layer, be[b], 0, 0)
    blocks = lambda b, *_: (b, 0)
    return pl.pallas_call(
        functools.partial(_expert_kernel, dff=dff, rb=rb, layer=layer),
        grid_spec=pltpu.PrefetchScalarGridSpec(
            num_scalar_prefetch=4,
            grid=(nb,),
            in_specs=[
                pl.BlockSpec((rb * TILE_ROWS, LANES), blocks),
                pl.BlockSpec((None, 1, 1, dff2), pick),
                pl.BlockSpec((None, 1, 1, d), pick),
                pl.BlockSpec(memory_space=pl.ANY),
                pl.BlockSpec(memory_space=pl.ANY),
            ],
            out_specs=pl.BlockSpec((rb * TILE_ROWS, LANES), blocks),
            scratch_shapes=[pltpu.VMEM((2, d, dff2), F32), pltpu.VMEM((2, dff, d), F32),
                            pltpu.VMEM((d, dff2), BF16), pltpu.VMEM((dff, d), BF16),
                            pltpu.SemaphoreType.DMA((2, 2))],
        ),
        out_shape=jax.ShapeDtypeStruct(x_disp.shape, F32),
        compiler_params=_cparams(("arbitrary",)),
        name="moe_experts",
    )(*block_table, n_used, x_disp, b_gu.reshape(nl, ne, 1, dff2), b_down.reshape(nl, ne, 1, d), w_gu, w_down)


def _combine_kernel(dest_ref, gates_ref, x_ref, mod_ref, g_ref, b_ref, y_hbm, o_ref, buf, sem, *, tm, alpha):
    def issue(r, c):
        for k in range(TOP_K):
            _tile_copy(y_hbm, dest_ref[k, r], buf.at[k], r, sem).start(priority=k % 2)
        return c

    lax.fori_loop(0, tm, issue, 0)

    def drain(r, c):
        for k in range(TOP_K):
            _tile_copy(y_hbm, 0, buf.at[k], 0, sem).wait()
        return c

    lax.fori_loop(0, tm, drain, 0)
    y = gates_ref[:, 0:1] * _load_token_tiles(buf.at[0], tm)
    for k in range(1, TOP_K):
        y = y + gates_ref[:, k:k + 1] * _load_token_tiles(buf.at[k], tm)
    r = alpha * x_ref[...] + (1.0 + mod_ref[2:3, :]) * y
    o_ref[...] = _layer_norm(r, g_ref[...], b_ref[...])


def _combine_norm(y_disp, dest, gates_col, x, mods, midx, ln_g, ln_b, seq, alpha):
    t, d = x.shape
    tm = MOE_TOK_TILE
    row = lambda i: (i, 0)
    return pl.pallas_call(
        functools.partial(_combine_kernel, tm=tm, alpha=alpha),
        grid=(t // tm,),
        in_specs=[
            pl.BlockSpec((TOP_K, tm), lambda i: (0, i), memory_space=pltpu.SMEM),
            pl.BlockSpec((tm, TOP_K), row),
            pl.BlockSpec((tm, d), row),
            _mod_spec(midx, seq // tm, d),
            pl.BlockSpec((1, d), lambda i: (0, 0)),
            pl.BlockSpec((1, d), lambda i: (0, 0)),
            pl.BlockSpec(memory_space=pl.ANY),
        ],
        out_specs=pl.BlockSpec((tm, d), row),
        out_shape=jax.ShapeDtypeStruct((t, d), F32),
        scratch_shapes=[pltpu.VMEM((TOP_K, tm * TILE_ROWS, LANES), F32), pltpu.SemaphoreType.DMA],
        compiler_params=_cparams(("arbitrary",)),
        name="moe_combine",
    )(dest, gates_col, x, mods, ln_g.reshape(1, d), ln_b.reshape(1, d), y_disp)


def _moe_block(x, mods, midx, router_w, router_b, layer, w_gu, b_gu, w_down, b_down, ln_g, ln_b, seq, alpha):
    t, d = x.shape
    assert d == TILE_ROWS * LANES
    rb = MOE_ROW_BLOCK
    nb = (t * TOP_K + N_EXPERTS * (rb - 1) + rb - 1) // rb
    h, eidx, rank, gates, counts = _route(x, mods, midx, router_w, router_b, seq)
    dest, block_table, expert_table, n_used = _plan(eidx, rank, counts, rb, nb)
    x_disp = _dispatch(h, dest, expert_table, nb * rb)
    y_disp = _expert_ffn(x_disp, block_table, n_used, layer, w_gu, b_gu, w_down, b_down, rb)
    return _combine_norm(y_disp, dest, gates.T, x, mods, midx, ln_g, ln_b, seq, alpha)


def _inproj_odd_kernel(x_ref, mod_ref, w_ref, wvt_ref, wf_ref, fb_ref, zs_ref, q_ref, k_ref, vt_ref, f_ref,
                       *, width):
    h = (x_ref[...] * (1.0 + mod_ref[1:2, :]) + mod_ref[0:1, :]).astype(BF16)
    z = jnp.dot(h, w_ref[...], preferred_element_type=F32)
    zs_ref[...] = z[:, :width]
    q_ref[...] = (z[:, width:2 * width] * (HEAD_DIM ** -0.5 * LOG2E)).astype(BF16)
    k_ref[...] = z[:, 2 * width:3 * width].astype(BF16)
    vt_ref[...] = lax.dot_general(wvt_ref[...], h, (((1,), (1,)), ((), ())),
                                  preferred_element_type=F32).astype(BF16)
    f_ref[...] = jnp.dot(h, wf_ref[...], preferred_element_type=F32) + fb_ref[...]


def _inproj_odd(x, mods, midx, w_main_bf, wvt_bf, wf_bf, fb, seq):
    t, d = x.shape
    width = wvt_bf.shape[0]
    tm = ROW_TILE
    row = lambda i: (i, 0)
    const = lambda i: (0, 0)
    return pl.pallas_call(
        functools.partial(_inproj_odd_kernel, width=width),
        grid=(t // tm,),
        in_specs=[
            pl.BlockSpec((tm, d), row),
            _mod_spec(midx, seq // tm, d),
            pl.BlockSpec((d, 3 * width), const),
            pl.BlockSpec((width, d), const),
            pl.BlockSpec((d, LANES), const),
            pl.BlockSpec((1, LANES), const),
        ],
        out_specs=[pl.BlockSpec((tm, width), row)] * 3 + [pl.BlockSpec((width, tm), lambda i: (0, i)),
                                                          pl.BlockSpec((tm, LANES), row)],
        out_shape=[jax.ShapeDtypeStruct((t, width), F32)] + [jax.ShapeDtypeStruct((t, width), BF16)] * 2
        + [jax.ShapeDtypeStruct((width, t), BF16), jax.ShapeDtypeStruct((t, LANES), F32)],
        compiler_params=_cparams(("parallel",)),
        name="inproj_odd",
    )(x, mods, w_main_bf, wvt_bf, wf_bf, fb)


def _fkeys_kernel(f_ref, k_ref, sel_ref, o_ref, carry_ref, *, ts, n_pairs):
    @pl.when(pl.program_id(1) == 0)
    def _():
        carry_ref[...] = jnp.zeros_like(carry_ref)

    x = f_ref[...]
    ls = jnp.minimum(x, 0.0) - jnp.log(1.0 + jnp.exp(-jnp.abs(x)))
    upto = (lax.broadcasted_iota(I32, (ts, ts), 1) <= lax.broadcasted_iota(I32, (ts, ts), 0)).astype(F32)
    cs = jnp.dot(upto, ls, precision=HIGHEST, preferred_element_type=F32) + carry_ref[...]
    carry_ref[...] = cs[ts - 1:ts, :]
    rest = -LOG2E * cs
    pieces = []
    for _ in range(FOX_BIAS_PARTS):
        piece = rest.astype(BF16)
        pieces.append(piece)
        rest = rest - piece.astype(F32)
    ext = jnp.dot(jnp.concatenate(pieces, axis=1), sel_ref[...], preferred_element_type=F32).astype(BF16)
    for p in range(n_pairs):
        o_ref[:, 2 * LANES * p:2 * LANES * p + LANES] = k_ref[:, LANES * p:LANES * (p + 1)]
        o_ref[:, 2 * LANES * p + LANES:2 * LANES * (p + 1)] = ext[:, LANES * p:LANES * (p + 1)]


def _forget_keys(flog, k, bsz, seq):
    t, width = k.shape
    n_pairs = width // LANES
    ts = FCUM_TILE
    nt = seq // ts
    sel = [[0.0] * width for _ in range(FOX_BIAS_PARTS * LANES)]
    for h in range(2 * n_pairs):
        for part in range(FOX_BIAS_PARTS):
            sel[part * LANES + h][(h // 2) * LANES + FOX_BIAS_PARTS * (h % 2) + part] = 1.0
    sel = jnp.asarray(sel, BF16)
    tile = lambda b, j: (b * nt + j, 0)
    return pl.pallas_call(
        functools.partial(_fkeys_kernel, ts=ts, n_pairs=n_pairs),
        grid=(bsz, nt),
        in_specs=[pl.BlockSpec((ts, LANES), tile), pl.BlockSpec((ts, width), tile),
                  pl.BlockSpec(sel.shape, lambda b, j: (0, 0))],
        out_specs=pl.BlockSpec((ts, 2 * width), tile),
        out_shape=jax.ShapeDtypeStruct((t, 2 * width), BF16),
        scratch_shapes=[pltpu.VMEM((1, LANES), F32)],
        compiler_params=_cparams(("parallel", "arbitrary")),
        name="forget_keys",
    )(flog, k, sel)


def _fox_kernel(q_ref, ka_ref, vt_ref, o_ref, acc_ref, m_ref, l_ref, qa_ref, st0, st1, p0, p1, al0, al1,
                mb0, mb1, *, tq, tk):
    st_refs, p_refs, al_refs, mb_refs = (st0, st1), (p0, p1), (al0, al1), (mb0, mb1)
    n_full = pl.program_id(2)
    lane = lax.broadcasted_iota(I32, (tq, LANES), 1)
    qp = q_ref[...]
    for hh in range(2):
        mine = (lane < HEAD_DIM) if hh == 0 else (lane >= HEAD_DIM)
        ones = (lane >= FOX_BIAS_PARTS * hh) & (lane < FOX_BIAS_PARTS * (hh + 1))
        qa_ref[hh] = jnp.concatenate([jnp.where(mine, qp, jnp.zeros_like(qp)),
                                      jnp.where(ones, 1.0, 0.0).astype(BF16)], axis=1)
    acc_ref[...] = jnp.zeros_like(acc_ref)
    m_ref[...] = jnp.full_like(m_ref, NEG_INF)
    l_ref[...] = jnp.zeros_like(l_ref)

    def scores(kb, hh):
        off = pl.multiple_of(kb * tk, tk)
        return lax.dot_general(ka_ref[pl.ds(off, tk), :], qa_ref[hh], (((1,), (1,)), ((), ())),
                               preferred_element_type=F32)

    def softmax_step(st, block_max, hh):
        m_old = m_ref[hh:hh + 1, :]
        m_new = jnp.maximum(m_old, block_max)
        alpha = jnp.exp2(m_old - m_new)
        p = jnp.exp2(st - m_new)
        l_ref[hh:hh + 1, :] = alpha * l_ref[hh:hh + 1, :] + jnp.sum(p, axis=0, keepdims=True)
        m_ref[hh:hh + 1, :] = m_new
        return alpha, p.astype(BF16)

    def accumulate(kb, hh, alpha, p):
        off = pl.multiple_of(kb * tk, tk)
        rows = slice(HEAD_DIM * hh, HEAD_DIM * (hh + 1))
        acc_ref[rows, :] = alpha * acc_ref[rows, :] + jnp.dot(vt_ref[rows, pl.ds(off, tk)], p,
                                                               preferred_element_type=F32)

    n = n_full + 1

    def key_block(pos):
        return jnp.where(pos == 0, n_full, pos - 1)

    def stage_a(pos, slot, masked=False):
        for hh in range(2):
            st = scores(key_block(pos), hh)
            if masked:
                keep = lax.broadcasted_iota(I32, (tk, tq), 0) <= lax.broadcasted_iota(I32, (tk, tq), 1)
                st = jnp.where(keep, st, NEG_INF)
            st_refs[slot][hh] = st
            mb_refs[slot][hh:hh + 1, :] = jnp.max(st, axis=0, keepdims=True)

    def stage_b(slot):
        for hh in range(2):
            alpha, p = softmax_step(st_refs[slot][hh], mb_refs[slot][hh:hh + 1, :], hh)
            al_refs[slot][hh:hh + 1, :] = alpha
            p_refs[slot][hh] = p

    def stage_c(pos, slot):
        for hh in range(2):
            accumulate(key_block(pos), hh, al_refs[slot][hh:hh + 1, :], p_refs[slot][hh])

    def step(i, slot):
        stage_a(i, slot)
        stage_b(1 - slot)
        stage_c(i - 2, slot)

    odd = lax.rem(n, 2) == 1
    stage_a(0, 0, masked=True)

    @pl.when(n >= 2)
    def _():
        stage_a(1, 1)
        stage_b(0)

    @pl.when(n == 1)
    def _():
        stage_b(0)
        stage_c(0, 0)

    def body(j, c):
        i = 2 + 2 * j
        step(i, 0)
        step(i + 1, 1)
        return c

    lax.fori_loop(0, (n - 2) // 2, body, 0)

    @pl.when(jnp.logical_and(n >= 2, jnp.logical_not(odd)))
    def _():
        stage_b(1)
        stage_c(n - 2, 0)
        stage_c(n - 1, 1)

    @pl.when(jnp.logical_and(n >= 3, odd))
    def _():
        step(n - 1, 0)
        stage_b(0)
        stage_c(n - 2, 1)
        stage_c(n - 1, 0)

    head0 = lax.broadcasted_iota(I32, (LANES, tq), 0) < HEAD_DIM
    out_t = acc_ref[...] / jnp.where(head0, l_ref[0:1, :], l_ref[1:2, :])
    o_ref[...] = out_t.T.astype(o_ref.dtype)


def _forgetting_attention(q, kaug, vt, bsz, seq):
    t, width = q.shape
    n_pairs = width // LANES
    tq, tk = FOX_TQ, FOX_TK
    nq = seq // tq
    return pl.pallas_call(
        functools.partial(_fox_kernel, tq=tq, tk=tk),
        grid=(bsz, n_pairs, nq),
        in_specs=[
            pl.BlockSpec((tq, LANES), lambda b, h, i: (b * nq + i, h)),
            pl.BlockSpec((seq, 2 * LANES), lambda b, h, i: (b, h)),
            pl.BlockSpec((LANES, seq), lambda b, h, i: (h, b)),
        ],
        out_specs=pl.BlockSpec((tq, LANES), lambda b, h, i: (b * nq + i, h)),
        out_shape=jax.ShapeDtypeStruct((t, width), BF16),
        scratch_shapes=[pltpu.VMEM((LANES, tq), F32), pltpu.VMEM((8, tq), F32), pltpu.VMEM((8, tq), F32),
                        pltpu.VMEM((2, tq, 2 * LANES), BF16)]
        + [pltpu.VMEM((2, tk, tq), F32)] * 2 + [pltpu.VMEM((2, tk, tq), BF16)] * 2 + [pltpu.VMEM((8, tq), F32)] * 4,
        compiler_params=_cparams(("parallel", "parallel", "arbitrary")),
        name="forgetting_attention",
    )(q, kaug, vt)


def _s5_prep_kernel(lre_ref, lim_ref, ldt_ref, bre_ref, bim_ref, are_ref, aim_ref, bb_re, bb_im):
    dt = jnp.exp(ldt_ref[...])
    lre = jnp.minimum(lre_ref[...], -1e-4)
    lim = lim_ref[...]
    mag = jnp.exp(lre * dt)
    a_re = mag * jnp.cos(lim * dt)
    a_im = mag * jnp.sin(lim * dt)
    den = lre * lre + lim * lim
    nre, nim = a_re - 1.0, a_im
    g_re = (nre * lre + nim * lim) / den
    g_im = (nim * lre - nre * lim) / den
    n = lre.shape[-1]
    per = n // bre_ref.shape[0]
    for j in range(bre_ref.shape[0]):
        gr = g_re[:, per * j:per * (j + 1)]
        gi = g_im[:, per * j:per * (j + 1)]
        bb_re[j] = (gr * bre_ref[j] - gi * bim_ref[j]).astype(bb_re.dtype)
        bb_im[j] = (gr * bim_ref[j] + gi * bre_ref[j]).astype(bb_im.dtype)
    are_ref[...] = a_re
    aim_ref[...] = a_im


def _block_diag_slabs(blocks, slabs):
    g, r, c = blocks.shape
    per = g // slabs
    eye = jnp.eye(per, dtype=bool)[None, :, None, :, None]
    bd = jnp.where(eye, blocks.reshape(slabs, per, r, 1, c), jnp.zeros((), blocks.dtype))
    return bd.reshape(slabs, per * r, per * c)


def _s5_kernel(u_ref, are_ref, aim_ref, bb_re, bb_im, cc_re, cc_im, dsk_ref, gw_ref, gb_ref, o_ref,
               xr0, xr1, xi0, xi1, carry_ref, *, ts, slabs):
    @pl.when(pl.program_id(1) == 0)
    def _():
        carry_ref[...] = jnp.zeros_like(carry_ref)

    xr, xi = (xr0, xr1), (xi0, xi1)
    per_half = TILE_ROWS // 2
    u = u_ref[...]
    ub = u.astype(BF16)
    for j in range(slabs):
        uj = ub[:, LANES * j:LANES * (j + 1)]
        bur = jnp.dot(uj, bb_re[j], preferred_element_type=F32)
        bui = jnp.dot(uj, bb_im[j], preferred_element_type=F32)
        half, base = j // 2, (j % 2) * per_half
        for s in range(per_half):
            xr[half][pl.ds(base + s, ts, stride=TILE_ROWS), :] = bur[:, LANES * s:LANES * (s + 1)]
            xi[half][pl.ds(base + s, ts, stride=TILE_ROWS), :] = bui[:, LANES * s:LANES * (s + 1)]
    a_r = [are_ref[TILE_ROWS * h:TILE_ROWS * (h + 1), :] for h in range(2)]
    a_i = [aim_ref[TILE_ROWS * h:TILE_ROWS * (h + 1), :] for h in range(2)]

    def steps(tb, state):
        state = list(state)
        for tt in range(TILE_ROWS):
            row = pl.multiple_of((tb * TILE_ROWS + tt) * TILE_ROWS, TILE_ROWS)
            for h in range(2):
                sr, si = state[h], state[2 + h]
                nr = a_r[h] * sr - a_i[h] * si + xr[h][pl.ds(row, TILE_ROWS), :]
                ni = a_r[h] * si + a_i[h] * sr + xi[h][pl.ds(row, TILE_ROWS), :]
                xr[h][pl.ds(row, TILE_ROWS), :] = nr
                xi[h][pl.ds(row, TILE_ROWS), :] = ni
                state[h], state[2 + h] = nr, ni
        return tuple(state)

    final = lax.fori_loop(0, ts // TILE_ROWS, steps, tuple(carry_ref[i] for i in range(4)))
    for i in range(4):
        carry_ref[i] = final[i]
    ys = []
    for j in range(slabs):
        half, base = j // 2, (j % 2) * per_half
        gather = lambda ref: jnp.concatenate(
            [ref[pl.ds(base + s, ts, stride=TILE_ROWS), :] for s in range(per_half)], axis=1).astype(BF16)
        yj = jnp.dot(gather(xr[half]), cc_re[j], preferred_element_type=F32)
        ys.append(yj - jnp.dot(gather(xi[half]), cc_im[j], preferred_element_type=F32))
    y = jnp.concatenate(ys, axis=1) + dsk_ref[...] * u
    y = 0.5 * y * (1.0 + jnp.tanh(math.sqrt(2.0 / math.pi) * (y + 0.044715 * (y * y * y))))
    gate = jnp.dot(y.astype(BF16), gw_ref[...], preferred_element_type=F32) + gb_ref[...]
    o_ref[...] = (y * jax.nn.sigmoid(gate)).astype(o_ref.dtype)


def _s5_mixer(u, lam_re, lam_im, log_dt, b_re, b_im, c_re, c_im, d_skip, glu_w, glu_b, bsz, seq):
    t, width = u.shape
    groups, states = lam_re.shape
    n = groups * states
    slabs = width // LANES
    ts = S5_TILE
    assert n == 2 * TILE_ROWS * LANES and slabs == 4
    flat = lambda a: a.reshape(1, n)
    ldt = jnp.repeat(log_dt, states).reshape(1, n)
    bre_bd = _block_diag_slabs(jnp.swapaxes(b_re, 1, 2), slabs)
    bim_bd = _block_diag_slabs(jnp.swapaxes(b_im, 1, 2), slabs)
    cre_bd = _block_diag_slabs(jnp.swapaxes(c_re, 1, 2), slabs).astype(BF16)
    cim_bd = _block_diag_slabs(jnp.swapaxes(c_im, 1, 2), slabs).astype(BF16)
    full = lambda a: pl.BlockSpec(a.shape, lambda *_: (0,) * a.ndim)
    prep_in = (flat(lam_re), flat(lam_im), ldt, bre_bd, bim_bd)
    a_re, a_im, bb_re, bb_im = pl.pallas_call(
        _s5_prep_kernel,
        grid=(1,),
        in_specs=[full(a) for a in prep_in],
        out_specs=[pl.BlockSpec((1, n), lambda i: (0, 0))] * 2 + [pl.BlockSpec(bre_bd.shape, lambda i: (0, 0, 0))] * 2,
        out_shape=[jax.ShapeDtypeStruct((1, n), F32)] * 2 + [jax.ShapeDtypeStruct(bre_bd.shape, BF16)] * 2,
        compiler_params=_cparams(("arbitrary",)),
        name="s5_prep",
    )(*prep_in)
    nt = seq // ts
    tiles = lambda a: a.reshape(n // LANES, LANES)
    consts = (tiles(a_re), tiles(a_im), bb_re, bb_im, cre_bd, cim_bd, d_skip.reshape(1, width),
              glu_w.astype(BF16), glu_b.reshape(1, width))
    return pl.pallas_call(
        functools.partial(_s5_kernel, ts=ts, slabs=slabs),
        grid=(bsz, nt),
        in_specs=[pl.BlockSpec((ts, width), lambda b, j: (b * nt + j, 0))] + [full(a) for a in consts],
        out_specs=pl.BlockSpec((ts, width), lambda b, j: (b * nt + j, 0)),
        out_shape=jax.ShapeDtypeStruct((t, width), BF16),
        scratch_shapes=[pltpu.VMEM((ts * TILE_ROWS, LANES), F32)] * 4 + [pltpu.VMEM((4, TILE_ROWS, LANES), F32)],
        compiler_params=_cparams(("parallel", "arbitrary")),
        name="s5_mixer",
    )(u, *consts)


def kernel(x, c, mod_w, mod_b, ln_g, ln_b, even_w_in, pool_w, pool_scale, rel_bias, even_w_out, odd_w_in, forget_b, ssm_lam_re, ssm_lam_im, ssm_log_dt, ssm_b_re, ssm_b_im, ssm_c_re, ssm_c_im, ssm_d, ssm_glu_w, ssm_glu_b, odd_w_out, router_w, router_b, exp_w_gu, exp_b_gu, exp_w_down, exp_b_down):
    bsz, seq, d = x.shape
    depth = mod_w.shape[0]
    alpha = (2.0 * depth) ** 0.25
    mods = _ada_mods(c, mod_w.reshape(depth * 2, d, 3 * d), mod_b.reshape(depth * 2, 3 * d))
    xt = x.reshape(bsz * seq, d)
    for layer in range(depth):
        i = layer // 2
        if layer % 2 == 0:
            w_in = even_w_in[i]
            width = w_in.shape[1] // 4
            zp, q, k, vt = _inproj_even(xt, mods, 2 * layer, w_in[:, :3 * width].astype(BF16),
                                        w_in[:, 3 * width:].T.astype(BF16), seq)
            ya = _pool_mixer(zp, pool_w[i].astype(BF16), pool_scale[i], seq)
            yb = _chunk_attention(q, k, vt, _cattn_bias_table(rel_bias[i]), bsz, seq)
            w_out = even_w_out[i]
        else:
            width = ssm_d.shape[1] * ssm_d.shape[2]
            w_in = odd_w_in[i]
            n_heads = forget_b.shape[1]
            wf = jnp.zeros((d, LANES), BF16).at[:, :n_heads].set(w_in[:, 4 * width:].astype(BF16))
            fb = jnp.zeros((1, LANES), F32).at[0, :n_heads].set(forget_b[i])
            zs, q, k, vt, flog = _inproj_odd(xt, mods, 2 * layer, w_in[:, :3 * width].astype(BF16),
                                             w_in[:, 3 * width:4 * width].T.astype(BF16), wf, fb, seq)
            ya = _s5_mixer(zs, ssm_lam_re[i], ssm_lam_im[i], ssm_log_dt[i], ssm_b_re[i], ssm_b_im[i],
                           ssm_c_re[i], ssm_c_im[i], ssm_d[i], ssm_glu_w[i], ssm_glu_b[i], bsz, seq)
            yb = _forgetting_attention(q, _forget_keys(flog, k, bsz, seq), vt, bsz, seq)
            w_out = odd_w_out[i]
        xt = _outproj_norm(ya, yb, xt, mods, 2 * layer, w_out.astype(BF16), ln_g[layer, 0], ln_b[layer, 0],
                           seq, alpha)
        xt = _moe_block(xt, mods, 2 * layer + 1, router_w[layer], router_b[layer], layer, exp_w_gu, exp_b_gu,
                        exp_w_down, exp_b_down, ln_g[layer, 1], ln_b[layer, 1], seq, alpha)
    return xt.reshape(bsz, seq, d)
```
